```python
import math
import jax, jax.numpy as jnp
from jax import lax
import numpy as np

D_MODEL = 1024
BATCH = 32
SEQ = 256
DEPTH = 4
DEC_BATCH = 4
DEC_SEQ = 2048
PAST_LEN = 512

GRID_W = 64
ATTN_WIDTH = 512
N_HEADS = 8
N_KV_HEADS = 2
HEAD_DIM = 64
GQA = N_HEADS // N_KV_HEADS
ROPE_THETA = 10000.0
ROT_PAIRS = HEAD_DIM // 4
Q_BLOCK = 128
SSM_WIDTH = 512
SSM_HEADS = 8
SSM_HEAD_DIM = 64
SSM_GROUPS = 2
SSM_STATE = 64
CONV_W = 5
CHUNK = 128
N_EXPERTS = 16
EC_FACTOR = 2
EXPERT_FF = 1024
EPS = 1e-6

MIX_WIDTH = ATTN_WIDTH + SSM_WIDTH
Q_DIM = N_HEADS * HEAD_DIM
KV_DIM = N_KV_HEADS * HEAD_DIM
BC_DIM = SSM_GROUPS * SSM_STATE
CONV_DIM = SSM_WIDTH + 2 * BC_DIM
IN_DIM = Q_DIM + 2 * KV_DIM + CONV_DIM + SSM_WIDTH + 2 * SSM_HEADS
SPLITS = (Q_DIM, Q_DIM + KV_DIM, Q_DIM + 2 * KV_DIM, Q_DIM + 2 * KV_DIM + CONV_DIM,
          Q_DIM + 2 * KV_DIM + CONV_DIM + SSM_WIDTH)

kernel_name = "hybrid_attn_ssd_ec_diffusion_step"


def rmsnorm(x, g):
    x32 = x.astype(jnp.float32)
    r = x32 * lax.rsqrt(jnp.mean(x32 * x32, axis=-1, keepdims=True) + EPS)
    return (r * g.astype(jnp.float32)).astype(x.dtype)


def axial_rope(length):
    t = jnp.arange(length)
    row = (t // GRID_W).astype(jnp.float32)
    col = (t % GRID_W).astype(jnp.float32)
    inv = ROPE_THETA ** (-jnp.arange(ROT_PAIRS, dtype=jnp.float32) / ROT_PAIRS)
    ang = jnp.concatenate([row[:, None] * inv, col[:, None] * inv], axis=-1)
    return jnp.cos(ang), jnp.sin(ang)


def apply_rope(x, cos, sin):
    half = HEAD_DIM // 2
    x32 = x.astype(jnp.float32)
    x1, x2 = x32[..., :half], x32[..., half:]
    c, s = cos[None, :, None, :], sin[None, :, None, :]
    return jnp.concatenate([x1 * c - x2 * s, x2 * c + x1 * s], axis=-1).astype(x.dtype)


def block_attention(q, keys, vals):
    b, L, _, d = q.shape
    nb = L // Q_BLOCK
    qb = q.astype(jnp.float32).reshape(b, nb, Q_BLOCK, N_KV_HEADS, GQA, d).transpose(1, 0, 2, 3, 4, 5)
    k32 = keys.astype(jnp.float32)
    v32 = vals.astype(jnp.float32)
    scale = HEAD_DIM ** -0.5

    def one_block(qblk):
        s = jnp.einsum('bqkgd,bskd->bkgqs', qblk, k32) * scale
        p = jax.nn.softmax(s, axis=-1)
        return jnp.einsum('bkgqs,bskd->bqkgd', p, v32)

    o = lax.map(one_block, qb)
    return o.transpose(1, 0, 2, 3, 4, 5).reshape(b, L, N_HEADS * d).astype(q.dtype)


def centred_dwconv(u, w, bias):
    pad = CONV_W // 2
    out = lax.conv_general_dilated(
        u, w[:, None, :].astype(u.dtype), window_strides=(1,), padding=[(pad, pad)],
        dimension_numbers=('NWC', 'WIO', 'NWC'), feature_group_count=u.shape[-1])
    return out + bias.astype(u.dtype)


def ssd_scan(x, dt, A, Bm, Cm, h0):
    b, L, H, P = x.shape
    nc = L // CHUNK
    rep = H // SSM_GROUPS
    xc = x.reshape(b, nc, CHUNK, H, P)
    Bc = jnp.repeat(Bm, rep, axis=2).reshape(b, nc, CHUNK, H, SSM_STATE)
    Cc = jnp.repeat(Cm, rep, axis=2).reshape(b, nc, CHUNK, H, SSM_STATE)
    dtc = dt.reshape(b, nc, CHUNK, H)
    acs = jnp.cumsum(dtc * A, axis=2)
    seg = acs[:, :, :, None, :] - acs[:, :, None, :, :]
    lower = jnp.tril(jnp.ones((CHUNK, CHUNK), dtype=bool))[None, None, :, :, None]
    decay = jnp.exp(jnp.where(lower, seg, -jnp.inf))
    scores = jnp.einsum('bcihn,bcjhn->bcijh', Cc, Bc) * decay * dtc[:, :, None, :, :]
    y_diag = jnp.einsum('bcijh,bcjhp->bcihp', scores, xc)
    w_end = jnp.exp(acs[:, :, -1:, :] - acs) * dtc
    states = jnp.einsum('bcjhn,bcjh,bcjhp->bchpn', Bc, w_end, xc)
    chunk_decay = jnp.exp(acs[:, :, -1, :])

    def step(h, inp):
        s, dcy = inp
        return h * dcy[:, :, None, None] + s, h

    h_final, h_prev = lax.scan(step, h0, (states.transpose(1, 0, 2, 3, 4), chunk_decay.transpose(1, 0, 2)))
    h_prev = h_prev.transpose(1, 0, 2, 3, 4)
    y_off = jnp.einsum('bcihn,bchpn,bcih->bcihp', Cc, h_prev, jnp.exp(acs))
    return (y_diag + y_off).reshape(b, L, H, P), h_final


def ssd_mixer(xbc, z, dt_raw, h0, A_log, dt_bias, d_skip, norm_g):
    b, L, _ = xbc.shape
    x32 = xbc.astype(jnp.float32)
    xs = x32[..., :SSM_WIDTH].reshape(b, L, SSM_HEADS, SSM_HEAD_DIM)
    Bm = x32[..., SSM_WIDTH:SSM_WIDTH + BC_DIM].reshape(b, L, SSM_GROUPS, SSM_STATE)
    Cm = x32[..., SSM_WIDTH + BC_DIM:].reshape(b, L, SSM_GROUPS, SSM_STATE)
    A = -jnp.exp(A_log.astype(jnp.float32))
    dt = jax.nn.softplus(dt_raw.astype(jnp.float32).reshape(b, L, 2, SSM_HEADS)
                         + dt_bias.astype(jnp.float32))
    h0 = h0.astype(jnp.float32)
    y_f, h_f = ssd_scan(xs, dt[:, :, 0], A[0], Bm, Cm, h0[:, 0])
    flip = lambda t: jnp.flip(t, axis=1)
    y_b, h_b = ssd_scan(flip(xs), flip(dt[:, :, 1]), A[1], flip(Bm), flip(Cm), h0[:, 1])
    y = y_f + flip(y_b) + xs * d_skip.astype(jnp.float32)[:, None]
    y = y.reshape(b, L, SSM_WIDTH) * jax.nn.silu(z.astype(jnp.float32))
    y = rmsnorm(y, norm_g).astype(xbc.dtype)
    return y, jnp.stack([h_f, h_b], axis=1)


def expert_choice_ffn(h, w_router, w_gate, w_up, w_down):
    b, L, D = h.shape
    n = b * L
    xs = h.reshape(n, D)
    aff = jax.nn.softmax(jnp.dot(xs, w_router).astype(jnp.float32), axis=-1)
    cap = EC_FACTOR * n // N_EXPERTS
    gates, idx = lax.top_k(aff.T, cap)
    xe = xs[idx]
    hid = jax.nn.silu(jnp.einsum('ecd,edf->ecf', xe, w_gate)) * jnp.einsum('ecd,edf->ecf', xe, w_up)
    ye = jnp.einsum('ecf,efd->ecd', hid, w_down) * gates[..., None].astype(h.dtype)
    out = jnp.zeros_like(xs).at[idx.reshape(-1)].add(ye.reshape(-1, D))
    return out.reshape(b, L, D)


def trunk_layer(x, cvec, rope, ctx, lp):
    b, L, _ = x.shape
    mod = jnp.dot(jax.nn.silu(cvec), lp['ada_w']) + lp['ada_b']
    sh1, sc1, g1, sh2, sc2, g2 = jnp.split(mod[:, None, :], 6, axis=-1)
    h = rmsnorm(x, lp['norm1']) * (1 + sc1) + sh1
    proj = jnp.dot(h, lp['w_in'])
    q, k, v, xbc, z, dt_raw = jnp.split(proj, SPLITS, axis=-1)
    q = rmsnorm(q.reshape(b, L, N_HEADS, HEAD_DIM), lp['q_norm'])
    k = rmsnorm(k.reshape(b, L, N_KV_HEADS, HEAD_DIM), lp['k_norm'])
    v = v.reshape(b, L, N_KV_HEADS, HEAD_DIM)
    if ctx is None:
        keys, vals = k, v
        h0 = jnp.zeros((b, 2, SSM_HEADS, SSM_HEAD_DIM, SSM_STATE), jnp.float32)
    else:
        cos, sin = rope
        q = apply_rope(q, cos, sin)
        k_lat = apply_rope(k, cos, sin)
        k_ctx, v_ctx, h0 = ctx
        keys = jnp.concatenate([k_ctx.astype(k.dtype), k_lat], axis=1)
        vals = jnp.concatenate([v_ctx.astype(v.dtype), v], axis=1)
    attn = block_attention(q, keys, vals)
    xbc = jax.nn.silu(centred_dwconv(xbc, lp['conv_w'], lp['conv_b']))
    ssm, h_fin = ssd_mixer(xbc, z, dt_raw, h0, lp['A_log'], lp['dt_bias'], lp['d_skip'], lp['ssm_norm'])
    x = x + g1 * jnp.dot(jnp.concatenate([attn, ssm], axis=-1), lp['w_out'])
    h2 = rmsnorm(x, lp['norm2']) * (1 + sc2) + sh2
    x = x + g2 * expert_choice_ffn(h2, lp['w_router'], lp['w_gate'], lp['w_up'], lp['w_down'])
    return x, k, v, h_fin


def setup_inputs(seed: int = 0) -> dict:
    key = jax.random.key(seed)
    ks = jax.random.split(key, 26)
    f32 = jnp.float32
    nrm = lambda k, shape, s: jax.random.normal(k, shape, f32) * s
    dt0 = jnp.exp(jax.random.uniform(ks[17], (DEPTH, 2, SSM_HEADS), f32, math.log(1e-3), math.log(1e-1)))
    return {
        "x_prompt": nrm(ks[0], (BATCH, SEQ, D_MODEL), 1.0),
        "x_sample": nrm(ks[1], (DEC_BATCH, DEC_SEQ, D_MODEL), 1.0),
        "cache_k": nrm(ks[2], (DEC_BATCH, DEPTH, PAST_LEN, N_KV_HEADS, HEAD_DIM), 1.0),
        "cache_v": nrm(ks[3], (DEC_BATCH, DEPTH, PAST_LEN, N_KV_HEADS, HEAD_DIM), 1.0),
        "state_ssm": nrm(ks[4], (DEC_BATCH, DEPTH, 2, SSM_HEADS, SSM_HEAD_DIM, SSM_STATE), 0.5),
        "c": nrm(ks[5], (DEC_BATCH, D_MODEL), 1.0),
        "c_ctx": nrm(ks[6], (D_MODEL,), 1.0),
        "norm1": 1.0 + nrm(ks[7], (DEPTH, D_MODEL), 0.02),
        "norm2": 1.0 + nrm(ks[8], (DEPTH, D_MODEL), 0.02),
        "ada_w": nrm(ks[9], (DEPTH, D_MODEL, 6 * D_MODEL), 0.5 * D_MODEL ** -0.5),
        "ada_b": nrm(ks[10], (DEPTH, 6 * D_MODEL), 0.02),
        "w_in": nrm(ks[11], (DEPTH, D_MODEL, IN_DIM), D_MODEL ** -0.5),
        "conv_w": nrm(ks[12], (DEPTH, CONV_W, CONV_DIM), CONV_W ** -0.5),
        "conv_b": nrm(ks[13], (DEPTH, CONV_DIM), 0.02),
        "q_norm": 1.0 + nrm(ks[14], (DEPTH, HEAD_DIM), 0.02),
        "k_norm": 1.0 + nrm(ks[15], (DEPTH, HEAD_DIM), 0.02),
        "A_log": jnp.log(jax.random.uniform(ks[16], (DEPTH, 2, SSM_HEADS), f32, 1.0, 16.0)),
        "dt_bias": dt0 + jnp.log(-jnp.expm1(-dt0)),
        "d_skip": 1.0 + nrm(ks[18], (DEPTH, SSM_HEADS), 0.1),
        "ssm_norm": 1.0 + nrm(ks[19], (DEPTH, SSM_WIDTH), 0.02),
        "w_out": nrm(ks[20], (DEPTH, MIX_WIDTH, D_MODEL), MIX_WIDTH ** -0.5),
        "w_router": nrm(ks[21], (DEPTH, D_MODEL, N_EXPERTS), D_MODEL ** -0.5),
        "w_gate": nrm(ks[22], (DEPTH, N_EXPERTS, D_MODEL, EXPERT_FF), D_MODEL ** -0.5),
        "w_up": nrm(ks[23], (DEPTH, N_EXPERTS, D_MODEL, EXPERT_FF), D_MODEL ** -0.5),
        "w_down": nrm(ks[24], (DEPTH, N_EXPERTS, EXPERT_FF, D_MODEL), EXPERT_FF ** -0.5),
    }


def reference(x_prompt, x_sample, cache_k, cache_v, state_ssm, c, c_ctx, norm1, norm2, ada_w, ada_b,
              w_in, conv_w, conv_b, q_norm, k_norm, A_log, dt_bias, d_skip, ssm_norm, w_out,
              w_router, w_gate, w_up, w_down):
    rope = axial_rope(x_sample.shape[1])
    cvec_ctx = c_ctx[None, :]
    y_prompt = x_prompt
    y_sample = x_sample
    ks_new, vs_new, hs_new = [], [], []
    for l in range(DEPTH):
        lp = {
            'norm1': norm1[l], 'norm2': norm2[l], 'ada_w': ada_w[l], 'ada_b': ada_b[l],
            'w_in': w_in[l], 'conv_w': conv_w[l], 'conv_b': conv_b[l], 'q_norm': q_norm[l],
            'k_norm': k_norm[l], 'A_log': A_log[l], 'dt_bias': dt_bias[l], 'd_skip': d_skip[l],
            'ssm_norm': ssm_norm[l], 'w_out': w_out[l], 'w_router': w_router[l],
            'w_gate': w_gate[l], 'w_up': w_up[l], 'w_down': w_down[l],
        }
        y_prompt, k_c, v_c, h_c = trunk_layer(y_prompt, cvec_ctx, None, None, lp)
        ks_new.append(k_c)
        vs_new.append(v_c)
        hs_new.append(h_c)
        ctx = (cache_k[:, l], cache_v[:, l], state_ssm[:, l])
        y_sample, _, _, _ = trunk_layer(y_sample, c, rope, ctx, lp)
    new_cache_k = jnp.stack(ks_new, axis=1)
    new_cache_v = jnp.stack(vs_new, axis=1)
    new_state_ssm = jnp.stack(hs_new, axis=1)
    return (y_prompt, y_sample, new_cache_k, new_cache_v, new_state_ssm)
```

```python
import functools

import jax
import jax.numpy as jnp
from jax import lax
from jax.experimental import pallas as pl
from jax.experimental.pallas import tpu as pltpu

F32 = jnp.float32
BF = jnp.bfloat16
I32 = jnp.int32

GRID_W = 64
N_HEADS = 8
N_KV_HEADS = 2
HEAD_DIM = 64
GQA = N_HEADS // N_KV_HEADS
ROPE_THETA = 10000.0
ROT_PAIRS = HEAD_DIM // 4
Q_DIM = N_HEADS * HEAD_DIM
KV_DIM = N_KV_HEADS * HEAD_DIM
SSM_WIDTH = 512
SSM_HEADS = 8
SSM_HEAD_DIM = 64
SSM_GROUPS = 2
SSM_STATE = 64
CONV_W = 5
CONV_PAD = CONV_W // 2
CHUNK = 128
BC_DIM = SSM_GROUPS * SSM_STATE
CONV_DIM = SSM_WIDTH + 2 * BC_DIM
N_EXPERTS = 16
EC_FACTOR = 2
EPS = 1e-6
SCALE = HEAD_DIM ** -0.5

LANES = 128
SUBLANES = 8
TOKEN_TILE = 256
HALO = SUBLANES
VMEM_LIMIT = 56 * 1024 * 1024

NT_DIMS = (((1,), (1,)), ((), ()))


def _dot(a, b):
    return jnp.dot(a, b, preferred_element_type=F32)


def _dot_nt(a, b):
    return lax.dot_general(a, b, NT_DIMS, preferred_element_type=F32)


def _dot_exact(a, b):
    return jnp.dot(a, b, preferred_element_type=F32, precision=lax.Precision.HIGHEST)


def _silu(x):
    return x / (1.0 + jnp.exp(-x))


def _softplus(x):
    return jnp.maximum(x, 0.0) + jnp.log(1.0 + jnp.exp(-jnp.abs(x)))


def _params(sem):
    return pltpu.CompilerParams(dimension_semantics=sem, vmem_limit_bytes=VMEM_LIMIT)


def _ada_kernel(cv_ref, w_ref, b_ref, o_ref):
    s = _silu(cv_ref[...]).astype(BF)
    o_ref[...] = _dot(s, w_ref[...].astype(BF)) + b_ref[...]


def _ada_modulation(cv, ada_w, ada_b):
    depth, d, n6 = ada_w.shape
    rows = cv.shape[0]
    tn = n6 // 4
    return pl.pallas_call(
        _ada_kernel,
        grid=(depth, n6 // tn),
        in_specs=[
            pl.BlockSpec((rows, d), lambda l, j: (0, 0)),
            pl.BlockSpec((None, d, tn), lambda l, j: (l, 0, j)),
            pl.BlockSpec((None, 1, tn), lambda l, j: (l, 0, j)),
        ],
        out_specs=pl.BlockSpec((None, rows, tn), lambda l, j: (l, 0, j)),
        out_shape=jax.ShapeDtypeStruct((depth, rows, n6), F32),
        compiler_params=_params(("parallel", "parallel")),
        name="ada_modulation",
    )(cv, ada_w, ada_b.reshape(depth, 1, n6))


def _inproj_kernel(x_ref, mod_ref, n1_ref, wq_ref, wk_ref, wv_ref, wx_ref, wz_ref, wdt_ref, wdtt_ref,
                   bdq_ref, bdk_ref, gq_ref, gk_ref, cos_ref, sin_ref,
                   q_out, k_out, v_out, xbc_out, z_out, dt_out, dtt_out):
    d = x_ref.shape[-1]
    x = x_ref[...]
    mod = mod_ref[...]
    sh1 = mod[:, 0:d]
    sc1 = mod[:, d:2 * d]
    ms = jnp.mean(x * x, axis=-1, keepdims=True)
    h = x * lax.rsqrt(ms + EPS) * n1_ref[...]
    hb = (h * (1.0 + sc1) + sh1).astype(BF)
    cos = cos_ref[...]
    sin = sin_ref[...]

    def norm_rope(t, bd_ref, g_ref, cos_t, sin_t):
        ss = _dot((t * t).astype(BF), bd_ref[...])
        tn = t * lax.rsqrt(ss * (1.0 / HEAD_DIM) + EPS) * g_ref[...]
        width = t.shape[-1]
        lane = lax.broadcasted_iota(I32, tn.shape, 1)
        first_half = (lane % HEAD_DIM) < (HEAD_DIM // 2)
        partner = jnp.where(first_half,
                            pltpu.roll(tn, width - HEAD_DIM // 2, 1),
                            pltpu.roll(tn, HEAD_DIM // 2, 1))
        return tn * cos_t + partner * sin_t

    q = norm_rope(_dot(hb, wq_ref[...]), bdq_ref, gq_ref, cos, sin)
    q_out[...] = (q * SCALE).astype(BF)
    k_out[...] = norm_rope(_dot(hb, wk_ref[...]), bdk_ref, gk_ref, cos[:, :KV_DIM], sin[:, :KV_DIM])
    v_out[...] = _dot(hb, wv_ref[...])
    xbc_out[...] = _dot(hb, wx_ref[...])
    z_out[...] = _dot(hb, wz_ref[...])
    dt_out[...] = _dot(hb, wdt_ref[...])
    dtt_out[...] = _dot_nt(wdtt_ref[...], hb)


def _in_projection(x, mod3, lw, consts, geom):
    t, d = x.shape
    tm = TOKEN_TILE
    full = lambda a: pl.BlockSpec(a.shape, lambda i: (0,) * a.ndim)
    row = lambda w: pl.BlockSpec((tm, w), lambda i: (i, 0))
    weights = [lw["wq"], lw["wk"], lw["wv"], lw["wx"], lw["wz"], lw["wdt"], lw["wdtt"],
               consts["bdq"], consts["bdk"], lw["gq"], lw["gk"]]
    out_shape = [
        jax.ShapeDtypeStruct((t, Q_DIM), BF),
        jax.ShapeDtypeStruct((t, KV_DIM), F32),
        jax.ShapeDtypeStruct((t, KV_DIM), F32),
        jax.ShapeDtypeStruct((t, CONV_DIM), F32),
        jax.ShapeDtypeStruct((t, SSM_WIDTH), F32),
        jax.ShapeDtypeStruct((t, 2 * SSM_HEADS), F32),
        jax.ShapeDtypeStruct((2 * SSM_HEADS, t), F32),
    ]
    return pl.pallas_call(
        _inproj_kernel,
        grid=(t // tm,),
        in_specs=[row(d),
                  pl.BlockSpec((None, 1, mod3.shape[-1]), lambda i: (geom.mod_row(i), 0, 0)),
                  full(lw["n1"])]
                 + [full(w) for w in weights]
                 + [pl.BlockSpec((tm, Q_DIM), lambda i: (geom.rope_block(i), 0)),
                    pl.BlockSpec((tm, Q_DIM), lambda i: (geom.rope_block(i), 0))],
        out_specs=[row(Q_DIM), row(KV_DIM), row(KV_DIM), row(CONV_DIM), row(SSM_WIDTH),
                   row(2 * SSM_HEADS), pl.BlockSpec((2 * SSM_HEADS, tm), lambda i: (0, i))],
        out_shape=out_shape,
        compiler_params=_params(("parallel",)),
        name="in_projection",
    )(x, mod3, lw["n1"], *weights, consts["cos"], consts["sin"])


def _head_pair_operands(k, v):
    lane = lax.broadcasted_iota(I32, k.shape, 1)
    low = lane < HEAD_DIM
    k_sw = pltpu.roll(k, HEAD_DIM, 1)
    v_sw = pltpu.roll(v, HEAD_DIM, 1)
    zero = jnp.zeros_like(v)
    kk = [jnp.where(low, k, k_sw).astype(BF), jnp.where(low, k_sw, k).astype(BF)]
    v_lo = [jnp.where(low, v, zero).astype(BF), jnp.where(low, v_sw, zero).astype(BF)]
    v_hi = [jnp.where(low, zero, v_sw).astype(BF), jnp.where(low, zero, v).astype(BF)]
    return kk, v_lo, v_hi


def _attend(q_ref, o_ref, key_sets):
    tq = q_ref.shape[0]
    lane = lax.broadcasted_iota(I32, (tq, LANES), 1)
    low = lane < HEAD_DIM
    for pair in range(N_HEADS // 2):
        g = (2 * pair) // GQA
        q2 = q_ref[:, pair * LANES:(pair + 1) * LANES]
        zero = jnp.zeros_like(q2)
        acc = jnp.zeros((tq, LANES), F32)
        inv = []
        for half, qh in enumerate((jnp.where(low, q2, zero), jnp.where(low, zero, q2))):
            scores = [_dot_nt(qh, ks[0][g]) for ks in key_sets]
            m = scores[0].max(axis=-1, keepdims=True)
            for s in scores[1:]:
                m = jnp.maximum(m, s.max(axis=-1, keepdims=True))
            denom = jnp.zeros((tq, 1), F32)
            for s, ks in zip(scores, key_sets):
                p = jnp.exp(s - m)
                denom = denom + p.sum(axis=-1, keepdims=True)
                acc = acc + _dot(p.astype(BF), ks[1 + half][g])
            inv.append(1.0 / denom)
        o_ref[:, pair * LANES:(pair + 1) * LANES] = (acc * jnp.where(low, inv[0], inv[1])).astype(BF)


def _attn_ctx_kernel(q_ref, k_ref, v_ref, o_ref):
    _attend(q_ref, o_ref, [_head_pair_operands(k_ref[...], v_ref[...])])


def _attn_lat_kernel(q_ref, kc_ref, vc_ref, kl_ref, vl_ref, prev_ref, o_ref):
    del prev_ref
    _attend(q_ref, o_ref, [_head_pair_operands(kc_ref[...], vc_ref[...]),
                           _head_pair_operands(kl_ref[...], vl_ref[...])])


def _attention(q, k, v, cache_k4, cache_v4, layer, geom):
    t = q.shape[0]
    seq = geom.seq
    o_ctx = pl.pallas_call(
        _attn_ctx_kernel,
        grid=(geom.batch,),
        in_specs=[pl.BlockSpec((seq, Q_DIM), lambda b: (b, 0)),
                  pl.BlockSpec((seq, KV_DIM), lambda b: (b, 0)),
                  pl.BlockSpec((seq, KV_DIM), lambda b: (b, 0))],
        out_specs=pl.BlockSpec((seq, Q_DIM), lambda b: (b, 0)),
        out_shape=jax.ShapeDtypeStruct((t, Q_DIM), BF),
        compiler_params=_params(("parallel",)),
        name="attention_context",
    )(q, k, v)

    tq = TOKEN_TILE
    dseq = geom.dec_seq
    past = cache_k4.shape[2]
    q_tiles = dseq // tq
    ctx_tiles = geom.n_ctx // tq
    ctx_seqs = geom.n_ctx // dseq
    return pl.pallas_call(
        _attn_lat_kernel,
        grid=(geom.dec_batch, q_tiles),
        in_specs=[pl.BlockSpec((tq, Q_DIM), lambda b, i: (ctx_tiles + b * q_tiles + i, 0)),
                  pl.BlockSpec((None, None, past, KV_DIM), lambda b, i: (b, layer, 0, 0)),
                  pl.BlockSpec((None, None, past, KV_DIM), lambda b, i: (b, layer, 0, 0)),
                  pl.BlockSpec((dseq, KV_DIM), lambda b, i: (ctx_seqs + b, 0)),
                  pl.BlockSpec((dseq, KV_DIM), lambda b, i: (ctx_seqs + b, 0)),
                  pl.BlockSpec(memory_space=pl.ANY)],
        out_specs=pl.BlockSpec((tq, Q_DIM), lambda b, i: (ctx_tiles + b * q_tiles + i, 0)),
        out_shape=jax.ShapeDtypeStruct((t, Q_DIM), BF),
        input_output_aliases={5: 0},
        compiler_params=_params(("parallel", "arbitrary")),
        name="attention_latent",
    )(q, cache_k4, cache_v4, k, v, o_ctx)


def _ssd_kernel(*refs, n_chunks, has_h0, emit_state):
    it = iter(refs)
    xbc_ref, z_ref, dt_ref, dtt_ref = next(it), next(it), next(it), next(it)
    h0_ref = next(it) if has_h0 else None
    (cw_ref, cb_ref, alog_r_ref, alog_c_ref, dtb_r_ref, dtb_c_ref,
     dskip_ref, gn_ref, prev_ref) = (next(it) for _ in range(9))
    del prev_ref
    y_out = next(it)
    hfin_out = next(it) if emit_state else None
    pad_ref, u_ref, csc_ref, csr_ref, dtr_ref, tot_ref, st_ref, hcur_ref, y_ref = (next(it) for _ in range(9))

    seq = xbc_ref.shape[0]
    nh = SSM_HEADS
    q = CHUNK

    pad_ref[0:HALO, :] = jnp.zeros((HALO, CONV_DIM), F32)
    pad_ref[HALO:HALO + seq, :] = xbc_ref[...]
    pad_ref[HALO + seq:2 * HALO + seq, :] = jnp.zeros((HALO, CONV_DIM), F32)

    a_row = -jnp.exp(alog_r_ref[...])
    a_col = -jnp.exp(alog_c_ref[...])
    ii = lax.broadcasted_iota(I32, (q, q), 0)
    jj = lax.broadcasted_iota(I32, (q, q), 1)
    lower = jj <= ii
    upper = jj >= ii
    tri_lower = lower.astype(F32)
    tri_upper = upper.astype(F32)
    fwd_lane = lax.broadcasted_iota(I32, (q, 2 * nh), 1) < nh
    fwd_row = lax.broadcasted_iota(I32, (2 * nh, q), 0) < nh

    def chunk_stats(c, carry):
        base = pl.multiple_of(c * q, q)
        win = pad_ref[pl.ds(base, q + 2 * HALO), :]
        acc = jnp.zeros((q, CONV_DIM), F32) + cb_ref[...]
        for tap in range(CONV_W):
            off = HALO - CONV_PAD + tap
            acc = acc + win[off:off + q, :] * cw_ref[tap:tap + 1, :]
        u = _silu(acc)
        u_ref[pl.ds(base, q), :] = u

        dtc = _softplus(dt_ref[pl.ds(base, q), :] + dtb_r_ref[...])
        dtr = _softplus(dtt_ref[:, pl.ds(base, q)] + dtb_c_ref[...])
        ac = dtc * a_row
        ar = dtr * a_col
        cs_col = jnp.where(fwd_lane, _dot_exact(tri_lower, ac), _dot_exact(tri_upper, ac))
        cs_row = jnp.where(fwd_row, _dot_exact(ar, tri_upper), _dot_exact(ar, tri_lower))
        csc_ref[pl.ds(base, q), :] = cs_col
        csr_ref[:, pl.ds(base, q)] = cs_row
        dtr_ref[:, pl.ds(base, q)] = dtr
        total = jnp.where(fwd_lane[0:1, :], cs_col[q - 1:q, :], cs_col[0:1, :])
        tot_ref[c] = total
        w_end = jnp.exp(total - cs_col) * dtc

        xs_t = u[:, 0:SSM_WIDTH].T.astype(BF)
        bmat = u[:, SSM_WIDTH:SSM_WIDTH + BC_DIM]
        for dh in range(2 * nh):
            head = dh % nh
            grp = head // (nh // SSM_GROUPS)
            bw = (bmat[:, grp * SSM_STATE:(grp + 1) * SSM_STATE] * w_end[:, dh:dh + 1]).astype(BF)
            st_ref[c, dh] = _dot(xs_t[head * SSM_HEAD_DIM:(head + 1) * SSM_HEAD_DIM, :], bw)
        return carry

    lax.fori_loop(0, n_chunks, chunk_stats, 0)

    for dh in range(2 * nh):
        hcur_ref[dh] = h0_ref[dh // nh, dh % nh] if has_h0 else jnp.zeros((SSM_HEAD_DIM, SSM_STATE), F32)

    def recur(step, carry):
        for direction in range(2):
            c = step if direction == 0 else n_chunks - 1 - step
            decay = jnp.exp(tot_ref[c])
            for head in range(nh):
                dh = direction * nh + head
                new = st_ref[c, dh]
                cur = hcur_ref[dh]
                st_ref[c, dh] = cur
                hcur_ref[dh] = cur * decay[:, dh:dh + 1] + new
        return carry

    lax.fori_loop(0, n_chunks, recur, 0)
    if emit_state:
        for dh in range(2 * nh):
            hfin_out[dh // nh, dh % nh] = hcur_ref[dh]

    def chunk_out(c, carry):
        base = pl.multiple_of(c * q, q)
        u = u_ref[pl.ds(base, q), :]
        xs = u[:, 0:SSM_WIDTH]
        xs_b = xs.astype(BF)
        bmat = u[:, SSM_WIDTH:SSM_WIDTH + BC_DIM].astype(BF)
        cmat = u[:, SSM_WIDTH + BC_DIM:CONV_DIM]
        cmat_b = cmat.astype(BF)
        cs_col = csc_ref[pl.ds(base, q), :]
        cs_row = csr_ref[:, pl.ds(base, q)]
        dtr = dtr_ref[:, pl.ds(base, q)]
        e_col = jnp.exp(cs_col)
        for grp in range(SSM_GROUPS):
            gs = slice(grp * SSM_STATE, (grp + 1) * SSM_STATE)
            cb = _dot_nt(cmat_b[:, gs], bmat[:, gs])
            for head in range(grp * (nh // SSM_GROUPS), (grp + 1) * (nh // SSM_GROUPS)):
                hb = nh + head
                seg_f = jnp.where(lower, cs_col[:, head:head + 1] - cs_row[head:head + 1, :], -1e30)
                seg_b = jnp.where(upper, cs_col[:, hb:hb + 1] - cs_row[hb:hb + 1, :], -1e30)
                mix = cb * (jnp.exp(seg_f) * dtr[head:head + 1, :] + jnp.exp(seg_b) * dtr[hb:hb + 1, :])
                hs = slice(head * SSM_HEAD_DIM, (head + 1) * SSM_HEAD_DIM)
                y = _dot(mix.astype(BF), xs_b[:, hs])
                y = y + _dot_nt((cmat[:, gs] * e_col[:, head:head + 1]).astype(BF), st_ref[c, head].astype(BF))
                y = y + _dot_nt((cmat[:, gs] * e_col[:, hb:hb + 1]).astype(BF), st_ref[c, hb].astype(BF))
                y_ref[:, hs] = y
        y = y_ref[...] + xs * dskip_ref[...]
        y = y * _silu(z_ref[pl.ds(base, q), :])
        ms = jnp.mean(y * y, axis=-1, keepdims=True)
        y_out[pl.ds(base, q), :] = (y * lax.rsqrt(ms + EPS) * gn_ref[...]).astype(BF)
        return carry

    lax.fori_loop(0, n_chunks, chunk_out, 0)


def _ssd_call(xbc, z, dt, dtt, h0, layer, lw, prev, *, seq, n_seq, row_block0, emit_state, name):
    t = xbc.shape[0]
    n_chunks = seq // CHUNK
    nh2 = 2 * SSM_HEADS
    full = lambda a: pl.BlockSpec(a.shape, lambda b: (0,) * a.ndim)
    params = [lw["conv_w"], lw["conv_b"], lw["alog_r"], lw["alog_c"], lw["dtb_r"], lw["dtb_c"],
              lw["dskip"], lw["gn"]]
    in_specs = [pl.BlockSpec((seq, CONV_DIM), lambda b: (row_block0 + b, 0)),
                pl.BlockSpec((seq, SSM_WIDTH), lambda b: (row_block0 + b, 0)),
                pl.BlockSpec((seq, nh2), lambda b: (row_block0 + b, 0)),
                pl.BlockSpec((nh2, seq), lambda b: (0, row_block0 + b))]
    args = [xbc, z, dt, dtt]
    if h0 is not None:
        in_specs.append(pl.BlockSpec((None, None, 2, SSM_HEADS, SSM_HEAD_DIM, SSM_STATE),
                                     lambda b: (b, layer, 0, 0, 0, 0)))
        args.append(h0)
    in_specs += [full(p) for p in params]
    args += params
    out_specs = [pl.BlockSpec((seq, SSM_WIDTH), lambda b: (row_block0 + b, 0))]
    out_shape = [jax.ShapeDtypeStruct((t, SSM_WIDTH), BF)]
    aliases = {}
    if prev is not None:
        in_specs.append(pl.BlockSpec(memory_space=pl.ANY))
        args.append(prev)
        aliases = {len(args) - 1: 0}
    else:
        in_specs.append(pl.BlockSpec((SUBLANES, LANES), lambda b: (0, 0)))
        args.append(jnp.zeros((SUBLANES, LANES), F32))
    if emit_state:
        out_specs.append(pl.BlockSpec((None, 2, SSM_HEADS, SSM_HEAD_DIM, SSM_STATE),
                                      lambda b: (b, 0, 0, 0, 0)))
        out_shape.append(jax.ShapeDtypeStruct((n_seq, 2, SSM_HEADS, SSM_HEAD_DIM, SSM_STATE), F32))
    scratch = [
        pltpu.VMEM((seq + 2 * HALO, CONV_DIM), F32),
        pltpu.VMEM((seq, CONV_DIM), F32),
        pltpu.VMEM((seq, nh2), F32),
        pltpu.VMEM((nh2, seq), F32),
        pltpu.VMEM((nh2, seq), F32),
        pltpu.VMEM((n_chunks, 1, nh2), F32),
        pltpu.VMEM((n_chunks, nh2, SSM_HEAD_DIM, SSM_STATE), F32),
        pltpu.VMEM((nh2, SSM_HEAD_DIM, SSM_STATE), F32),
        pltpu.VMEM((CHUNK, SSM_WIDTH), F32),
    ]
    return pl.pallas_call(
        functools.partial(_ssd_kernel, n_chunks=n_chunks, has_h0=h0 is not None, emit_state=emit_state),
        grid=(n_seq,),
        in_specs=in_specs,
        out_specs=out_specs,
        out_shape=out_shape,
        scratch_shapes=scratch,
        input_output_aliases=aliases,
        compiler_params=_params(("parallel",)),
        name=name,
    )(*args)


def _ssd_mixer(xbc, z, dt, dtt, state_ssm, layer, lw, geom):
    y_ctx, h_fin = _ssd_call(xbc, z, dt, dtt, None, layer, lw, None, seq=geom.seq, n_seq=geom.batch,
                             row_block0=0, emit_state=True, name="ssd_context")
    (y_all,) = _ssd_call(xbc, z, dt, dtt, state_ssm, layer, lw, y_ctx, seq=geom.dec_seq,
                         n_seq=geom.dec_batch, row_block0=geom.n_ctx // geom.dec_seq,
                         emit_state=False, name="ssd_latent")
    return y_all, h_fin


def _outproj_kernel(a_ref, s_ref, x_ref, mod_ref, n2_ref, woa_ref, wos_ref, wrh_ref, wrl_ref,
                    x1_out, h2_out, afft_out):
    d = x_ref.shape[-1]
    mod = mod_ref[...]
    g1 = mod[:, 2 * d:3 * d]
    sh2 = mod[:, 3 * d:4 * d]
    sc2 = mod[:, 4 * d:5 * d]
    y = _dot(a_ref[...], woa_ref[...]) + _dot(s_ref[...], wos_ref[...])
    x1 = x_ref[...] + g1 * y
    x1_out[...] = x1
    ms = jnp.mean(x1 * x1, axis=-1, keepdims=True)
    h2 = x1 * lax.rsqrt(ms + EPS) * n2_ref[...]
    h2 = h2 * (1.0 + sc2) + sh2
    h2_out[...] = h2
    hh = h2.astype(BF)
    hl = (h2 - hh.astype(F32)).astype(BF)
    wh = wrh_ref[...]
    logits = _dot_nt(wh, hh) + _dot_nt(wh, hl) + _dot_nt(wrl_ref[...], hh)
    logits = logits - logits.max(axis=0, keepdims=True)
    e = jnp.exp(logits)
    afft_out[...] = e / e.sum(axis=0, keepdims=True)


def _out_projection(attn, ssm, x, mod3, lw, geom):
    t, d = x.shape
    tm = TOKEN_TILE
    full = lambda a: pl.BlockSpec(a.shape, lambda i: (0,) * a.ndim)
    row = lambda w: pl.BlockSpec((tm, w), lambda i: (i, 0))
    weights = [lw["n2"], lw["woa"], lw["wos"], lw["wrh"], lw["wrl"]]
    return pl.pallas_call(
        _outproj_kernel,
        grid=(t // tm,),
        in_specs=[row(Q_DIM), row(SSM_WIDTH), row(d),
                  pl.BlockSpec((None, 1, mod3.shape[-1]), lambda i: (geom.mod_row(i), 0, 0))]
                 + [full(w) for w in weights],
        out_specs=[row(d), row(d), pl.BlockSpec((N_EXPERTS, tm), lambda i: (0, i))],
        out_shape=[jax.ShapeDtypeStruct((t, d), F32), jax.ShapeDtypeStruct((t, d), F32),
                   jax.ShapeDtypeStruct((N_EXPERTS, t), F32)],
        compiler_params=_params(("parallel",)),
        name="out_projection",
    )(attn, ssm, x, mod3, *weights)


def _route_kernel(aff_ref, mask_out, *, cap):
    a = aff_ref[...]
    n_e, rows, _ = a.shape
    bits = pltpu.bitcast(a, I32)

    def count_ge(v):
        hit = (bits >= v).astype(I32)
        return hit.sum(axis=2, keepdims=True).sum(axis=1, keepdims=True)

    def search(step, lo):
        cand = lo | (1 << (30 - step))
        return jnp.where(count_ge(cand) >= cap, cand, lo)

    thr = lax.fori_loop(0, 31, search, jnp.zeros((n_e, 1, 1), I32))
    gt = bits > thr
    eq = bits == thr
    need = cap - gt.astype(I32).sum(axis=2, keepdims=True).sum(axis=1, keepdims=True)

    eq2 = eq.astype(BF).reshape(n_e * rows, LANES)
    li = lax.broadcasted_iota(I32, (LANES, LANES), 0)
    lj = lax.broadcasted_iota(I32, (LANES, LANES), 1)
    within = _dot(eq2, (li <= lj).astype(BF))
    row_tot = _dot(eq2, jnp.ones((LANES, LANES), BF)).astype(BF)
    ri = lax.broadcasted_iota(I32, (n_e * rows, n_e * rows), 0)
    rj = lax.broadcasted_iota(I32, (n_e * rows, n_e * rows), 1)
    before = ((ri // rows == rj // rows) & (rj < ri)).astype(BF)
    rank = within + _dot(before, row_tot) - eq2.astype(F32)
    take = eq & (rank.reshape(n_e, rows, LANES) < need.astype(F32))
    mask_out[...] = (gt | take).astype(I32)


def _route(aff_t, n_sets, cap):
    n_e, t = aff_t.shape
    n = t // n_sets
    rows = n // LANES
    aff4 = aff_t.reshape(n_e, n_sets, rows, LANES)
    mask = pl.pallas_call(
        functools.partial(_route_kernel, cap=cap),
        grid=(n_sets,),
        in_specs=[pl.BlockSpec((n_e, None, rows, LANES), lambda s: (0, s, 0, 0))],
        out_specs=pl.BlockSpec((None, n_e, rows, LANES), lambda s: (s, 0, 0, 0)),
        out_shape=jax.ShapeDtypeStruct((n_sets, n_e, rows, LANES), I32),
        compiler_params=_params(("parallel",)),
        name="expert_choice_route",
    )(aff4)
    return mask.reshape(n_sets, n_e, n)


def _expert_kernel(xe_ref, gate_ref, wg_ref, wu_ref, wd_ref, y_out, acc_ref):
    f = pl.program_id(1)
    x = xe_ref[...]
    g = _dot(x, wg_ref[...].astype(BF))
    u = _dot(x, wu_ref[...].astype(BF))
    hid = (_silu(g) * u).astype(BF)
    part = _dot(hid, wd_ref[...].astype(BF))

    @pl.when(f == 0)
    def _():
        acc_ref[...] = part

    @pl.when(f > 0)
    def _():
        acc_ref[...] += part

    @pl.when(f == pl.num_programs(1) - 1)
    def _():
        y_out[...] = acc_ref[...] * gate_ref[...]


def _expert_ffn(xe, gates, w_gate, w_up, w_down, layer):
    n_e, m, d = xe.shape
    ff = w_gate.shape[-1]
    tf = 256
    return pl.pallas_call(
        _expert_kernel,
        grid=(n_e, ff // tf),
        in_specs=[pl.BlockSpec((None, m, d), lambda e, f: (e, 0, 0)),
                  pl.BlockSpec((None, m, 1), lambda e, f: (e, 0, 0)),
                  pl.BlockSpec((None, None, d, tf), lambda e, f: (layer, e, 0, f)),
                  pl.BlockSpec((None, None, d, tf), lambda e, f: (layer, e, 0, f)),
                  pl.BlockSpec((None, None, tf, d), lambda e, f: (layer, e, f, 0))],
        out_specs=pl.BlockSpec((None, m, d), lambda e, f: (e, 0, 0)),
        out_shape=jax.ShapeDtypeStruct((n_e, m, d), F32),
        scratch_shapes=[pltpu.VMEM((m, d), F32)],
        compiler_params=_params(("parallel", "arbitrary")),
        name="expert_swiglu",
    )(xe, gates, w_gate, w_up, w_down)


class _Geometry:
    def __init__(self, batch, seq, dec_batch, dec_seq):
        self.batch, self.seq, self.dec_batch, self.dec_seq = batch, seq, dec_batch, dec_seq
        self.n_ctx = batch * seq
        self.n_lat = dec_batch * dec_seq
        tm = TOKEN_TILE
        assert seq % tm == 0 and dec_seq % tm == 0 and self.n_ctx % dec_seq == 0
        assert seq % CHUNK == 0 and dec_seq % CHUNK == 0
        self.ctx_tiles = self.n_ctx // tm
        self.lat_tiles_per_seq = dec_seq // tm

    def mod_row(self, i):
        return jnp.where(i < self.ctx_tiles, 0, 1 + (i - self.ctx_tiles) // self.lat_tiles_per_seq)

    def rope_block(self, i):
        return jnp.where(i < self.ctx_tiles, 0, 1 + (i - self.ctx_tiles) % self.lat_tiles_per_seq)


def _rope_tables(dec_seq):
    t = jnp.arange(dec_seq)
    row = (t // GRID_W).astype(F32)
    col = (t % GRID_W).astype(F32)
    inv = ROPE_THETA ** (-jnp.arange(ROT_PAIRS, dtype=F32) / ROT_PAIRS)
    ang = jnp.concatenate([row[:, None] * inv, col[:, None] * inv], axis=-1)
    cos = jnp.tile(jnp.concatenate([jnp.cos(ang), jnp.cos(ang)], axis=-1), (1, N_HEADS))
    sin = jnp.tile(jnp.concatenate([-jnp.sin(ang), jnp.sin(ang)], axis=-1), (1, N_HEADS))
    ident = jnp.ones((TOKEN_TILE, Q_DIM), F32)
    return (jnp.concatenate([ident, cos], axis=0),
            jnp.concatenate([jnp.zeros_like(ident), sin], axis=0))


def _block_diag_ones(width):
    i = jnp.arange(width)
    return (i[:, None] // HEAD_DIM == i[None, :] // HEAD_DIM).astype(BF)


def _layer_weights(l, norm1, norm2, w_in, conv_w, conv_b, q_norm, k_norm, a_log, dt_bias, d_skip,
                   ssm_norm, w_out, w_router):
    w = w_in[l].astype(BF)
    o = 0
    parts = {}
    for name, width in (("wq", Q_DIM), ("wk", KV_DIM), ("wv", KV_DIM), ("wx", CONV_DIM),
                        ("wz", SSM_WIDTH), ("wdt", 2 * SSM_HEADS)):
        parts[name] = w[:, o:o + width]
        o += width
    wo = w_out[l].astype(BF)
    wr = w_router[l].T
    wrh = wr.astype(BF)
    nh2 = 2 * SSM_HEADS
    parts.update(
        wdtt=parts["wdt"].T,
        n1=norm1[l][None, :], n2=norm2[l][None, :],
        gq=jnp.tile(q_norm[l], N_HEADS)[None, :], gk=jnp.tile(k_norm[l], N_KV_HEADS)[None, :],
        conv_w=conv_w[l], conv_b=conv_b[l][None, :],
        alog_r=a_log[l].reshape(1, nh2), alog_c=a_log[l].reshape(nh2, 1),
        dtb_r=dt_bias[l].reshape(1, nh2), dtb_c=dt_bias[l].reshape(nh2, 1),
        dskip=jnp.repeat(d_skip[l], SSM_HEAD_DIM)[None, :], gn=ssm_norm[l][None, :],
        woa=wo[:Q_DIM], wos=wo[Q_DIM:],
        wrh=wrh, wrl=(wr - wrh.astype(F32)).astype(BF),
    )
    return parts


def _moe(h2, aff_t, w_gate, w_up, w_down, layer, n_sets):
    t, d = h2.shape
    n = t // n_sets
    cap = EC_FACTOR * n // N_EXPERTS
    mask = _route(aff_t, n_sets, cap)
    pos = jnp.cumsum(mask, axis=-1) - 1
    tok = jnp.broadcast_to(jnp.arange(n, dtype=I32), mask.shape)
    e_ix = jnp.broadcast_to(jnp.arange(N_EXPERTS, dtype=I32)[None, :, None], mask.shape)
    s_ix = jnp.broadcast_to(jnp.arange(n_sets, dtype=I32)[:, None, None], mask.shape)
    idx = jnp.zeros((n_sets, N_EXPERTS, cap + 1), I32).at[
        s_ix, e_ix, jnp.where(mask > 0, pos, cap)].set(tok)[..., :cap]
    gidx = idx + (jnp.arange(n_sets, dtype=I32) * n)[:, None, None]
    gidx = gidx.transpose(1, 0, 2).reshape(N_EXPERTS, n_sets * cap)
    gates = jnp.take_along_axis(aff_t, gidx, axis=1)
    xe = h2.astype(BF)[gidx]
    ye = _expert_ffn(xe, gates[..., None], w_gate, w_up, w_down, layer)
    return jnp.zeros((t, d), F32).at[gidx.reshape(-1)].add(ye.reshape(-1, d))


def kernel(x_prompt, x_sample, cache_k, cache_v, state_ssm, c, c_ctx, norm1, norm2, ada_w, ada_b,
           w_in, conv_w, conv_b, q_norm, k_norm, A_log, dt_bias, d_skip, ssm_norm, w_out,
           w_router, w_gate, w_up, w_down):
    batch, seq, d = x_prompt.shape
    dec_batch, dec_seq, _ = x_sample.shape
    depth = w_in.shape[0]
    past = cache_k.shape[2]
    geom = _Geometry(batch, seq, dec_batch, dec_seq)
    assert geom.n_ctx == geom.n_lat, "routed sets are stacked as two equal halves"

    rows = -(-(1 + dec_batch) // SUBLANES) * SUBLANES
    cv = jnp.zeros((rows, d), F32).at[0].set(c_ctx).at[1:1 + dec_batch].set(c)
    mod = _ada_modulation(cv, ada_w, ada_b)
    cos, sin = _rope_tables(dec_seq)
    consts = dict(cos=cos, sin=sin, bdq=_block_diag_ones(Q_DIM), bdk=_block_diag_ones(KV_DIM))
    cache_k4 = cache_k.reshape(dec_batch, depth, past, KV_DIM)
    cache_v4 = cache_v.reshape(dec_batch, depth, past, KV_DIM)

    x = jnp.concatenate([x_prompt.reshape(geom.n_ctx, d), x_sample.reshape(geom.n_lat, d)], axis=0)
    ks_new, vs_new, hs_new = [], [], []
    for l in range(depth):
        lw = _layer_weights(l, norm1, norm2, w_in, conv_w, conv_b, q_norm, k_norm, A_log, dt_bias,
                            d_skip, ssm_norm, w_out, w_router)
        mod3 = mod[l][:, None, :]
        q, k, v, xbc, z, dt, dtt = _in_projection(x, mod3, lw, consts, geom)
        attn = _attention(q, k, v, cache_k4, cache_v4, l, geom)
        ssm, h_fin = _ssd_mixer(xbc, z, dt, dtt, state_ssm, l, lw, geom)
        x1, h2, aff_t = _out_projection(attn, ssm, x, mod3, lw, geom)
        moe = _moe(h2, aff_t, w_gate, w_up, w_down, l, 2)
        g2 = mod[l][:, 5 * d:6 * d]
        g2_rows = jnp.concatenate([jnp.broadcast_to(g2[0], (geom.n_ctx, d)),
                                   jnp.repeat(g2[1:1 + dec_batch], dec_seq, axis=0)], axis=0)
        x = x1 + g2_rows * moe
        ks_new.append(k[:geom.n_ctx].reshape(batch, seq, N_KV_HEADS, HEAD_DIM))
        vs_new.append(v[:geom.n_ctx].reshape(batch, seq, N_KV_HEADS, HEAD_DIM))
        hs_new.append(h_fin)
    y_prompt = x[:geom.n_ctx].reshape(batch, seq, d)
    y_sample = x[geom.n_ctx:].reshape(dec_batch, dec_seq, d)
    return (y_prompt, y_sample, jnp.stack(ks_new, axis=1), jnp.stack(vs_new, axis=1),
            jnp.stack(hs_new, axis=1))
```

```python
import functools

import jax
import jax.numpy as jnp
from jax import lax
from jax.experimental import pallas as pl
from jax.experimental.pallas import tpu as pltpu

F32 = jnp.float32
BF = jnp.bfloat16
I32 = jnp.int32

GRID_W = 64
N_HEADS = 8
N_KV_HEADS = 2
HEAD_DIM = 64
GQA = N_HEADS // N_KV_HEADS
ROPE_THETA = 10000.0
ROT_PAIRS = HEAD_DIM // 4
Q_DIM = N_HEADS * HEAD_DIM
KV_DIM = N_KV_HEADS * HEAD_DIM
SSM_WIDTH = 512
SSM_HEADS = 8
SSM_HEAD_DIM = 64
SSM_GROUPS = 2
SSM_STATE = 64
CONV_W = 5
CONV_PAD = CONV_W // 2
CHUNK = 128
BC_DIM = SSM_GROUPS * SSM_STATE
CONV_DIM = SSM_WIDTH + 2 * BC_DIM
N_EXPERTS = 16
EC_FACTOR = 2
EPS = 1e-6
SCALE = HEAD_DIM ** -0.5

LANES = 128
SUBLANES = 8
TOKEN_TILE = 256
HALO = SUBLANES
VMEM_LIMIT = 56 * 1024 * 1024

NT_DIMS = (((1,), (1,)), ((), ()))


def _dot(a, b):
    return jnp.dot(a, b, preferred_element_type=F32)


def _dot_nt(a, b):
    return lax.dot_general(a, b, NT_DIMS, preferred_element_type=F32)


def _dot_exact(a, b):
    return jnp.dot(a, b, preferred_element_type=F32, precision=lax.Precision.HIGHEST)


def _silu(x):
    return x / (1.0 + jnp.exp(-x))


def _softplus(x):
    return jnp.maximum(x, 0.0) + jnp.log(1.0 + jnp.exp(-jnp.abs(x)))


def _params(sem):
    return pltpu.CompilerParams(dimension_semantics=sem, vmem_limit_bytes=VMEM_LIMIT)


def _ada_kernel(cv_ref, w_ref, b_ref, o_ref):
    s = _silu(cv_ref[...]).astype(BF)
    o_ref[...] = _dot(s, w_ref[...].astype(BF)) + b_ref[...]


def _ada_modulation(cv, ada_w, ada_b):
    depth, d, n6 = ada_w.shape
    rows = cv.shape[0]
    tn = n6 // 4
    return pl.pallas_call(
        _ada_kernel,
        grid=(depth, n6 // tn),
        in_specs=[
            pl.BlockSpec((rows, d), lambda l, j: (0, 0)),
            pl.BlockSpec((None, d, tn), lambda l, j: (l, 0, j)),
            pl.BlockSpec((None, 1, tn), lambda l, j: (l, 0, j)),
        ],
        out_specs=pl.BlockSpec((None, rows, tn), lambda l, j: (l, 0, j)),
        out_shape=jax.ShapeDtypeStruct((depth, rows, n6), F32),
        compiler_params=_params(("parallel", "parallel")),
        name="ada_modulation",
    )(cv, ada_w, ada_b.reshape(depth, 1, n6))


def _inproj_kernel(x_ref, mod_ref, n1_ref, wq_ref, wk_ref, wv_ref, wx_ref, wz_ref, wdt_ref, wdtt_ref,
                   bdq_ref, bdk_ref, gq_ref, gk_ref, cos_ref, sin_ref,
                   q_out, k_out, v_out, xbc_out, z_out, dt_out, dtt_out):
    d = x_ref.shape[-1]
    x = x_ref[...]
    mod = mod_ref[...]
    sh1 = mod[:, 0:d]
    sc1 = mod[:, d:2 * d]
    ms = jnp.mean(x * x, axis=-1, keepdims=True)
    h = x * lax.rsqrt(ms + EPS) * n1_ref[...]
    hb = (h * (1.0 + sc1) + sh1).astype(BF)
    cos = cos_ref[...]
    sin = sin_ref[...]

    def norm_rope(t, bd_ref, g_ref, cos_t, sin_t):
        ss = _dot((t * t).astype(BF), bd_ref[...])
        tn = t * lax.rsqrt(ss * (1.0 / HEAD_DIM) + EPS) * g_ref[...]
        width = t.shape[-1]
        lane = lax.broadcasted_iota(I32, tn.shape, 1)
        first_half = (lane % HEAD_DIM) < (HEAD_DIM // 2)
        partner = jnp.where(first_half,
                            pltpu.roll(tn, width - HEAD_DIM // 2, 1),
                            pltpu.roll(tn, HEAD_DIM // 2, 1))
        return tn * cos_t + partner * sin_t

    q = norm_rope(_dot(hb, wq_ref[...]), bdq_ref, gq_ref, cos, sin)
    q_out[...] = (q * SCALE).astype(BF)
    k_out[...] = norm_rope(_dot(hb, wk_ref[...]), bdk_ref, gk_ref, cos[:, :KV_DIM], sin[:, :KV_DIM])
    v_out[...] = _dot(hb, wv_ref[...])
    xbc_out[...] = _dot(hb, wx_ref[...])
    z_out[...] = _dot(hb, wz_ref[...])
    dt_out[...] = _dot(hb, wdt_ref[...])
    dtt_out[...] = _dot_nt(wdtt_ref[...], hb)


def _in_projection(x, mod3, lw, consts, geom):
    t, d = x.shape
    tm = TOKEN_TILE
    full = lambda a: pl.BlockSpec(a.shape, lambda i: (0,) * a.ndim)
    row = lambda w: pl.BlockSpec((tm, w), lambda i: (i, 0))
    weights = [lw["wq"], lw["wk"], lw["wv"], lw["wx"], lw["wz"], lw["wdt"], lw["wdtt"],
               consts["bdq"], consts["bdk"], lw["gq"], lw["gk"]]
    out_shape = [
        jax.ShapeDtypeStruct((t, Q_DIM), BF),
        jax.ShapeDtypeStruct((t, KV_DIM), F32),
        jax.ShapeDtypeStruct((t, KV_DIM), F32),
        jax.ShapeDtypeStruct((t, CONV_DIM), F32),
        jax.ShapeDtypeStruct((t, SSM_WIDTH), F32),
        jax.ShapeDtypeStruct((t, 2 * SSM_HEADS), F32),
        jax.ShapeDtypeStruct((2 * SSM_HEADS, t), F32),
    ]
    return pl.pallas_call(
        _inproj_kernel,
        grid=(t // tm,),
        in_specs=[row(d),
                  pl.BlockSpec((None, 1, mod3.shape[-1]), lambda i: (geom.mod_row(i), 0, 0)),
                  full(lw["n1"])]
                 + [full(w) for w in weights]
                 + [pl.BlockSpec((tm, Q_DIM), lambda i: (geom.rope_block(i), 0)),
                    pl.BlockSpec((tm, Q_DIM), lambda i: (geom.rope_block(i), 0))],
        out_specs=[row(Q_DIM), row(KV_DIM), row(KV_DIM), row(CONV_DIM), row(SSM_WIDTH),
                   row(2 * SSM_HEADS), pl.BlockSpec((2 * SSM_HEADS, tm), lambda i: (0, i))],
        out_shape=out_shape,
        compiler_params=_params(("parallel",)),
        name="in_projection",
    )(x, mod3, lw["n1"], *weights, consts["cos"], consts["sin"])


def _head_pair_operands(k, v):
    lane = lax.broadcasted_iota(I32, k.shape, 1)
    low = lane < HEAD_DIM
    k_sw = pltpu.roll(k, HEAD_DIM, 1)
    v_sw = pltpu.roll(v, HEAD_DIM, 1)
    zero = jnp.zeros_like(v)
    kk = [jnp.where(low, k, k_sw).astype(BF), jnp.where(low, k_sw, k).astype(BF)]
    v_lo = [jnp.where(low, v, zero).astype(BF), jnp.where(low, v_sw, zero).astype(BF)]
    v_hi = [jnp.where(low, zero, v_sw).astype(BF), jnp.where(low, zero, v).astype(BF)]
    return kk, v_lo, v_hi


def _attend(q_ref, o_ref, key_sets):
    tq = q_ref.shape[0]
    lane = lax.broadcasted_iota(I32, (tq, LANES), 1)
    low = lane < HEAD_DIM
    for pair in range(N_HEADS // 2):
        g = (2 * pair) // GQA
        q2 = q_ref[:, pair * LANES:(pair + 1) * LANES]
        zero = jnp.zeros_like(q2)
        acc = jnp.zeros((tq, LANES), F32)
        inv = []
        for half, qh in enumerate((jnp.where(low, q2, zero), jnp.where(low, zero, q2))):
            scores = [_dot_nt(qh, ks[0][g]) for ks in key_sets]
            m = scores[0].max(axis=-1, keepdims=True)
            for s in scores[1:]:
                m = jnp.maximum(m, s.max(axis=-1, keepdims=True))
            denom = jnp.zeros((tq, 1), F32)
            for s, ks in zip(scores, key_sets):
                p = jnp.exp(s - m)
                denom = denom + p.sum(axis=-1, keepdims=True)
                acc = acc + _dot(p.astype(BF), ks[1 + half][g])
            inv.append(1.0 / denom)
        o_ref[:, pair * LANES:(pair + 1) * LANES] = (acc * jnp.where(low, inv[0], inv[1])).astype(BF)


def _attn_ctx_kernel(q_ref, k_ref, v_ref, o_ref):
    _attend(q_ref, o_ref, [_head_pair_operands(k_ref[...], v_ref[...])])


def _attn_lat_kernel(q_ref, kc_ref, vc_ref, kl_ref, vl_ref, o_ref):
    _attend(q_ref, o_ref, [_head_pair_operands(kc_ref[...], vc_ref[...]),
                           _head_pair_operands(kl_ref[...], vl_ref[...])])


def _attention(q, k, v, cache_k4, cache_v4, layer, geom):
    seq = geom.seq
    o_ctx = pl.pallas_call(
        _attn_ctx_kernel,
        grid=(geom.batch,),
        in_specs=[pl.BlockSpec((seq, Q_DIM), lambda b: (b, 0)),
                  pl.BlockSpec((seq, KV_DIM), lambda b: (b, 0)),
                  pl.BlockSpec((seq, KV_DIM), lambda b: (b, 0))],
        out_specs=pl.BlockSpec((seq, Q_DIM), lambda b: (b, 0)),
        out_shape=jax.ShapeDtypeStruct((geom.n_ctx, Q_DIM), BF),
        compiler_params=_params(("parallel",)),
        name="attention_context",
    )(q, k, v)

    tq = TOKEN_TILE
    dseq = geom.dec_seq
    past = cache_k4.shape[2]
    q_tiles = dseq // tq
    ctx_tiles = geom.n_ctx // tq
    ctx_seqs = geom.n_ctx // dseq
    o_lat = pl.pallas_call(
        _attn_lat_kernel,
        grid=(geom.dec_batch, q_tiles),
        in_specs=[pl.BlockSpec((tq, Q_DIM), lambda b, i: (ctx_tiles + b * q_tiles + i, 0)),
                  pl.BlockSpec((None, None, past, KV_DIM), lambda b, i: (b, layer, 0, 0)),
                  pl.BlockSpec((None, None, past, KV_DIM), lambda b, i: (b, layer, 0, 0)),
                  pl.BlockSpec((dseq, KV_DIM), lambda b, i: (ctx_seqs + b, 0)),
                  pl.BlockSpec((dseq, KV_DIM), lambda b, i: (ctx_seqs + b, 0))],
        out_specs=pl.BlockSpec((tq, Q_DIM), lambda b, i: (b * q_tiles + i, 0)),
        out_shape=jax.ShapeDtypeStruct((geom.n_lat, Q_DIM), BF),
        compiler_params=_params(("parallel", "arbitrary")),
        name="attention_latent",
    )(q, cache_k4, cache_v4, k, v)
    return o_ctx, o_lat


def _ssd_kernel(*refs, n_chunks, has_h0, emit_state):
    it = iter(refs)
    xbc_ref, z_ref, dt_ref, dtt_ref = next(it), next(it), next(it), next(it)
    h0_ref = next(it) if has_h0 else None
    (cw_ref, cb_ref, alog_r_ref, alog_c_ref, dtb_r_ref, dtb_c_ref,
     dskip_ref, gn_ref) = (next(it) for _ in range(8))
    y_out = next(it)
    hfin_out = next(it) if emit_state else None
    pad_ref, u_ref, csc_ref, csr_ref, dtr_ref, tot_ref, st_ref, hcur_ref, y_ref = (next(it) for _ in range(9))

    seq = xbc_ref.shape[0]
    nh = SSM_HEADS
    q = CHUNK

    pad_ref[0:HALO, :] = jnp.zeros((HALO, CONV_DIM), F32)
    pad_ref[HALO:HALO + seq, :] = xbc_ref[...]
    pad_ref[HALO + seq:2 * HALO + seq, :] = jnp.zeros((HALO, CONV_DIM), F32)

    a_row = -jnp.exp(alog_r_ref[...])
    a_col = -jnp.exp(alog_c_ref[...])
    ii = lax.broadcasted_iota(I32, (q, q), 0)
    jj = lax.broadcasted_iota(I32, (q, q), 1)
    lower = jj <= ii
    upper = jj >= ii
    tri_lower = lower.astype(F32)
    tri_upper = upper.astype(F32)
    fwd_lane = lax.broadcasted_iota(I32, (q, 2 * nh), 1) < nh
    fwd_row = lax.broadcasted_iota(I32, (2 * nh, q), 0) < nh

    def chunk_stats(c, carry):
        base = pl.multiple_of(c * q, q)
        win = pad_ref[pl.ds(base, q + 2 * HALO), :]
        acc = jnp.zeros((q, CONV_DIM), F32) + cb_ref[...]
        for tap in range(CONV_W):
            off = HALO - CONV_PAD + tap
            acc = acc + win[off:off + q, :] * cw_ref[tap:tap + 1, :]
        u = _silu(acc)
        u_ref[pl.ds(base, q), :] = u

        dtc = _softplus(dt_ref[pl.ds(base, q), :] + dtb_r_ref[...])
        dtr = _softplus(dtt_ref[:, pl.ds(base, q)] + dtb_c_ref[...])
        ac = dtc * a_row
        ar = dtr * a_col
        cs_col = jnp.where(fwd_lane, _dot_exact(tri_lower, ac), _dot_exact(tri_upper, ac))
        cs_row = jnp.where(fwd_row, _dot_exact(ar, tri_upper), _dot_exact(ar, tri_lower))
        csc_ref[pl.ds(base, q), :] = cs_col
        csr_ref[:, pl.ds(base, q)] = cs_row
        dtr_ref[:, pl.ds(base, q)] = dtr
        total = jnp.where(fwd_lane[0:1, :], cs_col[q - 1:q, :], cs_col[0:1, :])
        tot_ref[c] = total
        w_end = jnp.exp(total - cs_col) * dtc

        xs_t = u[:, 0:SSM_WIDTH].T.astype(BF)
        bmat = u[:, SSM_WIDTH:SSM_WIDTH + BC_DIM]
        for dh in range(2 * nh):
            head = dh % nh
            grp = head // (nh // SSM_GROUPS)
            bw = (bmat[:, grp * SSM_STATE:(grp + 1) * SSM_STATE] * w_end[:, dh:dh + 1]).astype(BF)
            st_ref[c, dh] = _dot(xs_t[head * SSM_HEAD_DIM:(head + 1) * SSM_HEAD_DIM, :], bw)
        return carry

    lax.fori_loop(0, n_chunks, chunk_stats, 0)

    for dh in range(2 * nh):
        hcur_ref[dh] = h0_ref[dh // nh, dh % nh] if has_h0 else jnp.zeros((SSM_HEAD_DIM, SSM_STATE), F32)

    def recur(step, carry):
        for direction in range(2):
            c = step if direction == 0 else n_chunks - 1 - step
            decay = jnp.exp(tot_ref[c])
            for head in range(nh):
                dh = direction * nh + head
                new = st_ref[c, dh]
                cur = hcur_ref[dh]
                st_ref[c, dh] = cur
                hcur_ref[dh] = cur * decay[:, dh:dh + 1] + new
        return carry

    lax.fori_loop(0, n_chunks, recur, 0)
    if emit_state:
        for dh in range(2 * nh):
            hfin_out[dh // nh, dh % nh] = hcur_ref[dh]

    def chunk_out(c, carry):
        base = pl.multiple_of(c * q, q)
        u = u_ref[pl.ds(base, q), :]
        xs = u[:, 0:SSM_WIDTH]
        xs_b = xs.astype(BF)
        bmat = u[:, SSM_WIDTH:SSM_WIDTH + BC_DIM].astype(BF)
        cmat = u[:, SSM_WIDTH + BC_DIM:CONV_DIM]
        cmat_b = cmat.astype(BF)
        cs_col = csc_ref[pl.ds(base, q), :]
        cs_row = csr_ref[:, pl.ds(base, q)]
        dtr = dtr_ref[:, pl.ds(base, q)]
        e_col = jnp.exp(cs_col)
        for grp in range(SSM_GROUPS):
            gs = slice(grp * SSM_STATE, (grp + 1) * SSM_STATE)
            cb = _dot_nt(cmat_b[:, gs], bmat[:, gs])
            for head in range(grp * (nh // SSM_GROUPS), (grp + 1) * (nh // SSM_GROUPS)):
                hb = nh + head
                seg_f = jnp.where(lower, cs_col[:, head:head + 1] - cs_row[head:head + 1, :], -1e30)
                seg_b = jnp.where(upper, cs_col[:, hb:hb + 1] - cs_row[hb:hb + 1, :], -1e30)
                mix = cb * (jnp.exp(seg_f) * dtr[head:head + 1, :] + jnp.exp(seg_b) * dtr[hb:hb + 1, :])
                hs = slice(head * SSM_HEAD_DIM, (head + 1) * SSM_HEAD_DIM)
                y = _dot(mix.astype(BF), xs_b[:, hs])
                y = y + _dot_nt((cmat[:, gs] * e_col[:, head:head + 1]).astype(BF), st_ref[c, head].astype(BF))
                y = y + _dot_nt((cmat[:, gs] * e_col[:, hb:hb + 1]).astype(BF), st_ref[c, hb].astype(BF))
                y_ref[:, hs] = y
        y = y_ref[...] + xs * dskip_ref[...]
        y = y * _silu(z_ref[pl.ds(base, q), :])
        ms = jnp.mean(y * y, axis=-1, keepdims=True)
        y_out[pl.ds(base, q), :] = (y * lax.rsqrt(ms + EPS) * gn_ref[...]).astype(BF)
        return carry

    lax.fori_loop(0, n_chunks, chunk_out, 0)


def _ssd_call(xbc, z, dt, dtt, h0, layer, lw, *, seq, n_seq, row_block0, emit_state, name):
    n_chunks = seq // CHUNK
    nh2 = 2 * SSM_HEADS
    full = lambda a: pl.BlockSpec(a.shape, lambda b: (0,) * a.ndim)
    params = [lw["conv_w"], lw["conv_b"], lw["alog_r"], lw["alog_c"], lw["dtb_r"], lw["dtb_c"],
              lw["dskip"], lw["gn"]]
    in_specs = [pl.BlockSpec((seq, CONV_DIM), lambda b: (row_block0 + b, 0)),
                pl.BlockSpec((seq, SSM_WIDTH), lambda b: (row_block0 + b, 0)),
                pl.BlockSpec((seq, nh2), lambda b: (row_block0 + b, 0)),
                pl.BlockSpec((nh2, seq), lambda b: (0, row_block0 + b))]
    args = [xbc, z, dt, dtt]
    if h0 is not None:
        in_specs.append(pl.BlockSpec((None, None, 2, SSM_HEADS, SSM_HEAD_DIM, SSM_STATE),
                                     lambda b: (b, layer, 0, 0, 0, 0)))
        args.append(h0)
    in_specs += [full(p) for p in params]
    args += params
    out_specs = [pl.BlockSpec((seq, SSM_WIDTH), lambda b: (b, 0))]
    out_shape = [jax.ShapeDtypeStruct((n_seq * seq, SSM_WIDTH), BF)]
    if emit_state:
        out_specs.append(pl.BlockSpec((None, 2, SSM_HEADS, SSM_HEAD_DIM, SSM_STATE),
                                      lambda b: (b, 0, 0, 0, 0)))
        out_shape.append(jax.ShapeDtypeStruct((n_seq, 2, SSM_HEADS, SSM_HEAD_DIM, SSM_STATE), F32))
    scratch = [
        pltpu.VMEM((seq + 2 * HALO, CONV_DIM), F32),
        pltpu.VMEM((seq, CONV_DIM), F32),
        pltpu.VMEM((seq, nh2), F32),
        pltpu.VMEM((nh2, seq), F32),
        pltpu.VMEM((nh2, seq), F32),
        pltpu.VMEM((n_chunks, 1, nh2), F32),
        pltpu.VMEM((n_chunks, nh2, SSM_HEAD_DIM, SSM_STATE), F32),
        pltpu.VMEM((nh2, SSM_HEAD_DIM, SSM_STATE), F32),
        pltpu.VMEM((CHUNK, SSM_WIDTH), F32),
    ]
    return pl.pallas_call(
        functools.partial(_ssd_kernel, n_chunks=n_chunks, has_h0=h0 is not None, emit_state=emit_state),
        grid=(n_seq,),
        in_specs=in_specs,
        out_specs=out_specs,
        out_shape=out_shape,
        scratch_shapes=scratch,
        compiler_params=_params(("parallel",)),
        name=name,
    )(*args)


def _ssd_mixer(xbc, z, dt, dtt, state_ssm, layer, lw, geom):
    y_ctx, h_fin = _ssd_call(xbc, z, dt, dtt, None, layer, lw, seq=geom.seq, n_seq=geom.batch,
                             row_block0=0, emit_state=True, name="ssd_context")
    (y_lat,) = _ssd_call(xbc, z, dt, dtt, state_ssm, layer, lw, seq=geom.dec_seq,
                         n_seq=geom.dec_batch, row_block0=geom.n_ctx // geom.dec_seq,
                         emit_state=False, name="ssd_latent")
    return y_ctx, y_lat, h_fin


def _outproj_kernel(ac_ref, al_ref, sc_ref, sl_ref, x_ref, mod_ref, n2_ref, woa_ref, wos_ref, wrh_ref, wrl_ref,
                    x1_out, h2_out, afft_out, *, ctx_tiles):
    d = x_ref.shape[-1]
    mod = mod_ref[...]
    g1 = mod[:, 2 * d:3 * d]
    sh2 = mod[:, 3 * d:4 * d]
    sc2 = mod[:, 4 * d:5 * d]
    is_ctx = pl.program_id(0) < ctx_tiles
    attn = jnp.where(is_ctx, ac_ref[...], al_ref[...])
    ssm = jnp.where(is_ctx, sc_ref[...], sl_ref[...])
    y = _dot(attn, woa_ref[...]) + _dot(ssm, wos_ref[...])
    x1 = x_ref[...] + g1 * y
    x1_out[...] = x1
    ms = jnp.mean(x1 * x1, axis=-1, keepdims=True)
    h2 = x1 * lax.rsqrt(ms + EPS) * n2_ref[...]
    h2 = h2 * (1.0 + sc2) + sh2
    hh = h2.astype(BF)
    bits = pltpu.bitcast(hh.astype(F32), I32)
    h2_out[...] = lax.shift_right_logical(bits[:, :d // 2], 16) | bits[:, d // 2:]
    hl = (h2 - hh.astype(F32)).astype(BF)
    wh = wrh_ref[...]
    logits = _dot_nt(wh, hh) + _dot_nt(wh, hl) + _dot_nt(wrl_ref[...], hh)
    logits = logits - logits.max(axis=0, keepdims=True)
    e = jnp.exp(logits)
    afft_out[...] = e / e.sum(axis=0, keepdims=True)


def _out_projection(attn_c, attn_l, ssm_c, ssm_l, x, mod3, lw, geom):
    t, d = x.shape
    tm = TOKEN_TILE
    nc = geom.ctx_tiles
    full = lambda a: pl.BlockSpec(a.shape, lambda i: (0,) * a.ndim)
    row = lambda w: pl.BlockSpec((tm, w), lambda i: (i, 0))
    ctx_row = lambda w: pl.BlockSpec((tm, w), lambda i: (jnp.minimum(i, nc - 1), 0))
    lat_row = lambda w: pl.BlockSpec((tm, w), lambda i: (jnp.maximum(i - nc, 0), 0))
    weights = [lw["n2"], lw["woa"], lw["wos"], lw["wrh"], lw["wrl"]]
    return pl.pallas_call(
        functools.partial(_outproj_kernel, ctx_tiles=nc),
        grid=(t // tm,),
        in_specs=[ctx_row(Q_DIM), lat_row(Q_DIM), ctx_row(SSM_WIDTH), lat_row(SSM_WIDTH), row(d),
                  pl.BlockSpec((None, 1, mod3.shape[-1]), lambda i: (geom.mod_row(i), 0, 0))]
                 + [full(w) for w in weights],
        out_specs=[row(d), row(d // 2), pl.BlockSpec((N_EXPERTS, tm), lambda i: (0, i))],
        out_shape=[jax.ShapeDtypeStruct((t, d), F32), jax.ShapeDtypeStruct((t, d // 2), I32),
                   jax.ShapeDtypeStruct((N_EXPERTS, t), F32)],
        compiler_params=_params(("parallel",)),
        name="out_projection",
    )(attn_c, attn_l, ssm_c, ssm_l, x, mod3, *weights)


def _flat_cumsum(m2, rows):
    gr = m2.shape[0]
    li = lax.broadcasted_iota(I32, (LANES, LANES), 0)
    lj = lax.broadcasted_iota(I32, (LANES, LANES), 1)
    within = _dot_exact(m2, (li <= lj).astype(F32))
    row_tot = _dot_exact(m2, jnp.ones((LANES, LANES), F32))
    ri = lax.broadcasted_iota(I32, (gr, gr), 0)
    rj = lax.broadcasted_iota(I32, (gr, gr), 1)
    before = ((ri // rows == rj // rows) & (rj < ri)).astype(F32)
    return within + _dot_exact(before, row_tot)


def _plan_kernel(aff_ref, idx_out, gate_out, dslot_out, start_out, cnt_out, *, cap):
    a = aff_ref[...]
    n_e, rows, _ = a.shape

    def count(hit):
        return hit.astype(I32).sum(axis=2, keepdims=True).sum(axis=1, keepdims=True)

    def search(step, lo):
        cand = lo | (1 << (30 - step))
        return jnp.where(count(a >= lax.bitcast_convert_type(cand, F32)) >= cap, cand, lo)

    thr_bits = lax.fori_loop(0, 31, search, jnp.zeros((n_e, 1, 1), I32))
    thr = lax.bitcast_convert_type(thr_bits, F32)
    gt = a > thr
    eq = a == thr
    need = cap - count(gt)
    eqf = eq.astype(F32).reshape(n_e * rows, LANES)
    rank = (_flat_cumsum(eqf, rows) - eqf).reshape(n_e, rows, LANES)
    mask = gt | (eq & (rank < need.astype(F32)))
    mf = mask.astype(F32)

    c3 = _flat_cumsum(mf.reshape(n_e * rows, LANES), rows).reshape(n_e, rows, LANES)
    ahead = []
    run = jnp.zeros((rows, LANES), F32)
    for e in range(n_e):
        ahead.append(run)
        run = run + mf[e]
    cnt = run
    start = _flat_cumsum(cnt, rows) - cnt
    start_out[...] = start.astype(I32)
    cnt_out[...] = cnt.astype(I32)

    s_lane = lax.broadcasted_iota(I32, (1, cap), 1).astype(F32)
    r_col = lax.broadcasted_iota(I32, (LANES, cap), 0).astype(F32)
    pad = jnp.zeros((LANES - rows, LANES), F32)

    def lanes_first(xm):
        return (jnp.concatenate([xm, pad], axis=0) if rows < LANES else xm).T

    for e in range(n_e):
        c_e = c3[e]
        row_of = (c_e[:, LANES - 1:LANES] <= s_lane).astype(F32).sum(axis=0, keepdims=True)
        onehot = (r_col == row_of).astype(F32)
        stacked = jnp.concatenate([lanes_first(c_e), lanes_first(a[e]), lanes_first(start + ahead[e])], axis=0)
        sel = _dot_exact(stacked, onehot)
        lane_of = (sel[0:LANES] <= s_lane).astype(F32).sum(axis=0, keepdims=True)
        pick = r_col == lane_of
        idx_out[e:e + 1, :] = (row_of * LANES + lane_of).astype(I32)
        gate_out[e:e + 1, :] = jnp.where(pick, sel[LANES:2 * LANES], 0.0).sum(axis=0, keepdims=True)
        dslot_out[e:e + 1, :] = jnp.where(pick, sel[2 * LANES:3 * LANES], 0.0).sum(axis=0, keepdims=True).astype(I32)


def _plan(aff_t, n_sets, cap):
    n_e, t = aff_t.shape
    n = t // n_sets
    rows = n // LANES
    assert rows <= LANES
    aff4 = aff_t.reshape(n_e, n_sets, rows, LANES)
    slot_spec = pl.BlockSpec((None, n_e, cap), lambda s: (s, 0, 0))
    tok_spec = pl.BlockSpec((None, rows, LANES), lambda s: (s, 0, 0))
    slot_shape = lambda dt: jax.ShapeDtypeStruct((n_sets, n_e, cap), dt)
    tok_shape = jax.ShapeDtypeStruct((n_sets, rows, LANES), I32)
    return pl.pallas_call(
        functools.partial(_plan_kernel, cap=cap),
        grid=(n_sets,),
        in_specs=[pl.BlockSpec((n_e, None, rows, LANES), lambda s: (0, s, 0, 0))],
        out_specs=[slot_spec, slot_spec, slot_spec, tok_spec, tok_spec],
        out_shape=[slot_shape(I32), slot_shape(F32), slot_shape(I32), tok_shape, tok_shape],
        compiler_params=_params(("parallel",)),
        name="expert_choice_plan",
    )(aff4)


DMA_UNROLL = 8
COMBINE_ROWS = 256


def _expert_kernel(gidx_ref, dest_ref, h2p_hbm, gate_ref, wg_ref, wu_ref, wd_ref, z_hbm,
                   xw_ref, xb_ref, acc_ref, sem_g, sem_s):
    e = pl.program_id(0)
    f = pl.program_id(1)
    m_rows, half = xw_ref.shape
    base = e * m_rows

    @pl.when(f == 0)
    def _():
        def issue(j, carry):
            for u in range(DMA_UNROLL):
                m = j * DMA_UNROLL + u
                pltpu.make_async_copy(h2p_hbm.at[pl.ds(gidx_ref[base + m], 1), :],
                                      xw_ref.at[pl.ds(m, 1), :], sem_g).start()
            return carry

        lax.fori_loop(0, m_rows // DMA_UNROLL, issue, 0)
        pltpu.make_async_copy(h2p_hbm.at[pl.ds(0, m_rows), :], xw_ref, sem_g).wait()
        w = xw_ref[...]
        xb_ref[:, :half] = lax.bitcast_convert_type(w << 16, F32).astype(BF)
        xb_ref[:, half:] = lax.bitcast_convert_type(w & -65536, F32).astype(BF)

    x = xb_ref[...]
    g = _dot(x, wg_ref[...].astype(BF))
    u = _dot(x, wu_ref[...].astype(BF))
    hid = (_silu(g) * u).astype(BF)
    part = _dot(hid, wd_ref[...].astype(BF))

    @pl.when(f == 0)
    def _():
        acc_ref[...] = part

    @pl.when(f > 0)
    def _():
        acc_ref[...] += part

    @pl.when(f == pl.num_programs(1) - 1)
    def _():
        acc_ref[...] = acc_ref[...] * gate_ref[...]

        def issue(j, carry):
            for u in range(DMA_UNROLL):
                m = j * DMA_UNROLL + u
                pltpu.make_async_copy(acc_ref.at[pl.ds(m, 1), :],
                                      z_hbm.at[pl.ds(dest_ref[base + m], 1), :], sem_s).start()
            return carry

        lax.fori_loop(0, m_rows // DMA_UNROLL, issue, 0)
        pltpu.make_async_copy(acc_ref, z_hbm.at[pl.ds(0, m_rows), :], sem_s).wait()


def _expert_ffn(gidx, dest, h2p, gates, w_gate, w_up, w_down, layer):
    n_e, m, _ = gates.shape
    d = 2 * h2p.shape[1]
    ff = w_gate.shape[-1]
    tf = 256
    assert m % DMA_UNROLL == 0 and (n_e * m) % COMBINE_ROWS == 0
    return pl.pallas_call(
        _expert_kernel,
        grid_spec=pltpu.PrefetchScalarGridSpec(
            num_scalar_prefetch=2,
            grid=(n_e, ff // tf),
            in_specs=[pl.BlockSpec(memory_space=pl.ANY),
                      pl.BlockSpec((None, m, 1), lambda e, f, gi, de: (e, 0, 0)),
                      pl.BlockSpec((None, None, d, tf), lambda e, f, gi, de: (layer, e, 0, f)),
                      pl.BlockSpec((None, None, d, tf), lambda e, f, gi, de: (layer, e, 0, f)),
                      pl.BlockSpec((None, None, tf, d), lambda e, f, gi, de: (layer, e, f, 0))],
            out_specs=pl.BlockSpec(memory_space=pl.ANY),
            scratch_shapes=[pltpu.VMEM((m, d // 2), I32), pltpu.VMEM((m, d), BF), pltpu.VMEM((m, d), F32),
                            pltpu.SemaphoreType.DMA, pltpu.SemaphoreType.DMA]),
        out_shape=jax.ShapeDtypeStruct((n_e * m, d), F32),
        compiler_params=_params(("arbitrary", "arbitrary")),
        name="expert_swiglu",
    )(gidx, dest, h2p, gates, w_gate, w_up, w_down)


def _combine_kernel(lo_ref, hi_ref, z_hbm, start_ref, end_ref, x1_ref, mod_ref, o_ref, zbuf, acc_ref, sem):
    i = pl.program_id(0)
    d = x1_ref.shape[-1]
    ch = zbuf.shape[1]
    lo = lo_ref[i]
    hi = hi_ref[i]

    def chunk_copy(kc, slot):
        return pltpu.make_async_copy(z_hbm.at[pl.ds(pl.multiple_of(kc * ch, ch), ch), :], zbuf.at[slot],
                                     sem.at[slot])

    @pl.when(hi > lo)
    def _():
        chunk_copy(lo, 0).start()

    acc_ref[...] = jnp.zeros_like(acc_ref)
    start = start_ref[...]
    end = end_ref[...]

    def body(kc, carry):
        slot = (kc - lo) % 2
        chunk_copy(kc, slot).wait()

        @pl.when(kc + 1 < hi)
        def _():
            chunk_copy(kc + 1, 1 - slot).start()

        k = kc * ch + lax.broadcasted_iota(I32, (1, ch), 1)
        own = jnp.where((k >= start) & (k < end), 1.0, 0.0).astype(BF)
        zc = zbuf[slot]
        zh = zc.astype(BF)
        zl = (zc - zh.astype(F32)).astype(BF)
        acc_ref[...] += _dot(own, zh) + _dot(own, zl)
        return carry

    lax.fori_loop(lo, hi, body, 0)
    g2 = mod_ref[...][:, 5 * d:6 * d]
    o_ref[...] = x1_ref[...] + g2 * acc_ref[...]


def _combine(z, start, end, tile_lo, tile_hi, x1, mod3, geom):
    t, d = x1.shape
    tm = TOKEN_TILE
    return pl.pallas_call(
        _combine_kernel,
        grid_spec=pltpu.PrefetchScalarGridSpec(
            num_scalar_prefetch=2,
            grid=(t // tm,),
            in_specs=[pl.BlockSpec(memory_space=pl.ANY),
                      pl.BlockSpec((tm, 1), lambda i, lo, hi: (i, 0)),
                      pl.BlockSpec((tm, 1), lambda i, lo, hi: (i, 0)),
                      pl.BlockSpec((tm, d), lambda i, lo, hi: (i, 0)),
                      pl.BlockSpec((None, 1, mod3.shape[-1]), lambda i, lo, hi: (geom.mod_row(i), 0, 0))],
            out_specs=pl.BlockSpec((tm, d), lambda i, lo, hi: (i, 0)),
            scratch_shapes=[pltpu.VMEM((2, COMBINE_ROWS, d), F32), pltpu.VMEM((tm, d), F32),
                            pltpu.SemaphoreType.DMA((2,))]),
        out_shape=jax.ShapeDtypeStruct((t, d), F32),
        compiler_params=_params(("arbitrary",)),
        name="gated_combine",
    )(tile_lo, tile_hi, z, start, end, x1, mod3)


class _Geometry:
    def __init__(self, batch, seq, dec_batch, dec_seq):
        self.batch, self.seq, self.dec_batch, self.dec_seq = batch, seq, dec_batch, dec_seq
        self.n_ctx = batch * seq
        self.n_lat = dec_batch * dec_seq
        tm = TOKEN_TILE
        assert seq % tm == 0 and dec_seq % tm == 0 and self.n_ctx % dec_seq == 0
        assert seq % CHUNK == 0 and dec_seq % CHUNK == 0
        self.ctx_tiles = self.n_ctx // tm
        self.lat_tiles_per_seq = dec_seq // tm

    def mod_row(self, i):
        return jnp.where(i < self.ctx_tiles, 0, 1 + (i - self.ctx_tiles) // self.lat_tiles_per_seq)

    def rope_block(self, i):
        return jnp.where(i < self.ctx_tiles, 0, 1 + (i - self.ctx_tiles) % self.lat_tiles_per_seq)


def _rope_tables(dec_seq):
    t = jnp.arange(dec_seq)
    row = (t // GRID_W).astype(F32)
    col = (t % GRID_W).astype(F32)
    inv = ROPE_THETA ** (-jnp.arange(ROT_PAIRS, dtype=F32) / ROT_PAIRS)
    ang = jnp.concatenate([row[:, None] * inv, col[:, None] * inv], axis=-1)
    cos = jnp.tile(jnp.concatenate([jnp.cos(ang), jnp.cos(ang)], axis=-1), (1, N_HEADS))
    sin = jnp.tile(jnp.concatenate([-jnp.sin(ang), jnp.sin(ang)], axis=-1), (1, N_HEADS))
    ident = jnp.ones((TOKEN_TILE, Q_DIM), F32)
    return (jnp.concatenate([ident, cos], axis=0),
            jnp.concatenate([jnp.zeros_like(ident), sin], axis=0))


def _block_diag_ones(width):
    i = jnp.arange(width)
    return (i[:, None] // HEAD_DIM == i[None, :] // HEAD_DIM).astype(BF)


def _layer_weights(l, norm1, norm2, w_in, conv_w, conv_b, q_norm, k_norm, a_log, dt_bias, d_skip,
                   ssm_norm, w_out, w_router):
    w = w_in[l].astype(BF)
    o = 0
    parts = {}
    for name, width in (("wq", Q_DIM), ("wk", KV_DIM), ("wv", KV_DIM), ("wx", CONV_DIM),
                        ("wz", SSM_WIDTH), ("wdt", 2 * SSM_HEADS)):
        parts[name] = w[:, o:o + width]
        o += width
    wo = w_out[l].astype(BF)
    wr = w_router[l].T
    wrh = wr.astype(BF)
    nh2 = 2 * SSM_HEADS
    parts.update(
        wdtt=parts["wdt"].T,
        n1=norm1[l][None, :], n2=norm2[l][None, :],
        gq=jnp.tile(q_norm[l], N_HEADS)[None, :], gk=jnp.tile(k_norm[l], N_KV_HEADS)[None, :],
        conv_w=conv_w[l], conv_b=conv_b[l][None, :],
        alog_r=a_log[l].reshape(1, nh2), alog_c=a_log[l].reshape(nh2, 1),
        dtb_r=dt_bias[l].reshape(1, nh2), dtb_c=dt_bias[l].reshape(nh2, 1),
        dskip=jnp.repeat(d_skip[l], SSM_HEAD_DIM)[None, :], gn=ssm_norm[l][None, :],
        woa=wo[:Q_DIM], wos=wo[Q_DIM:],
        wrh=wrh, wrl=(wr - wrh.astype(F32)).astype(BF),
    )
    return parts


def _moe(h2p, aff_t, x1, mod3, w_gate, w_up, w_down, layer, n_sets, geom):
    t = h2p.shape[0]
    n = t // n_sets
    cap = EC_FACTOR * n // N_EXPERTS
    pairs = N_EXPERTS * cap
    idx, gates, dslot, start, cnt = _plan(aff_t, n_sets, cap)
    set_ix = jnp.arange(n_sets, dtype=I32)[:, None, None]
    expert_major = lambda a: a.transpose(1, 0, 2).reshape(N_EXPERTS, n_sets * cap)
    gidx = expert_major(idx + set_ix * n).reshape(-1)
    dest = expert_major(dslot + set_ix * pairs).reshape(-1)
    z = _expert_ffn(gidx, dest, h2p, expert_major(gates)[..., None], w_gate, w_up, w_down, layer)
    start_g = (start.reshape(n_sets, n) + set_ix[:, :, 0] * pairs).reshape(t, 1)
    end_g = start_g + cnt.reshape(t, 1)
    tile_lo = start_g[::TOKEN_TILE, 0] // COMBINE_ROWS
    tile_hi = (end_g[TOKEN_TILE - 1::TOKEN_TILE, 0] + COMBINE_ROWS - 1) // COMBINE_ROWS
    return _combine(z, start_g, end_g, tile_lo, tile_hi, x1, mod3, geom)


def kernel(x_prompt, x_sample, cache_k, cache_v, state_ssm, c, c_ctx, norm1, norm2, ada_w, ada_b,
           w_in, conv_w, conv_b, q_norm, k_norm, A_log, dt_bias, d_skip, ssm_norm, w_out,
           w_router, w_gate, w_up, w_down):
    batch, seq, d = x_prompt.shape
    dec_batch, dec_seq, _ = x_sample.shape
    depth = w_in.shape[0]
    past = cache_k.shape[2]
    geom = _Geometry(batch, seq, dec_batch, dec_seq)
    assert geom.n_ctx == geom.n_lat, "routed sets are stacked as two equal halves"

    rows = -(-(1 + dec_batch) // SUBLANES) * SUBLANES
    cv = jnp.zeros((rows, d), F32).at[0].set(c_ctx).at[1:1 + dec_batch].set(c)
    mod = _ada_modulation(cv, ada_w, ada_b)
    cos, sin = _rope_tables(dec_seq)
    consts = dict(cos=cos, sin=sin, bdq=_block_diag_ones(Q_DIM), bdk=_block_diag_ones(KV_DIM))
    cache_k4 = cache_k.reshape(dec_batch, depth, past, KV_DIM)
    cache_v4 = cache_v.reshape(dec_batch, depth, past, KV_DIM)

    x = jnp.concatenate([x_prompt.reshape(geom.n_ctx, d), x_sample.reshape(geom.n_lat, d)], axis=0)
    ks_new, vs_new, hs_new = [], [], []
    for l in range(depth):
        lw = _layer_weights(l, norm1, norm2, w_in, conv_w, conv_b, q_norm, k_norm, A_log, dt_bias,
                            d_skip, ssm_norm, w_out, w_router)
        mod3 = mod[l][:, None, :]
        q, k, v, xbc, z, dt, dtt = _in_projection(x, mod3, lw, consts, geom)
        attn_c, attn_l = _attention(q, k, v, cache_k4, cache_v4, l, geom)
        ssm_c, ssm_l, h_fin = _ssd_mixer(xbc, z, dt, dtt, state_ssm, l, lw, geom)
        x1, h2p, aff_t = _out_projection(attn_c, attn_l, ssm_c, ssm_l, x, mod3, lw, geom)
        x = _moe(h2p, aff_t, x1, mod3, w_gate, w_up, w_down, l, 2, geom)
        ks_new.append(k[:geom.n_ctx].reshape(batch, seq, N_KV_HEADS, HEAD_DIM))
        vs_new.append(v[:geom.n_ctx].reshape(batch, seq, N_KV_HEADS, HEAD_DIM))
        hs_new.append(h_fin)
    y_prompt = x[:geom.n_ctx].reshape(batch, seq, d)
    y_sample = x[geom.n_ctx:].reshape(dec_batch, dec_seq, d)
    return (y_prompt, y_sample, jnp.stack(ks_new, axis=1), jnp.stack(vs_new, axis=1),
            jnp.stack(hs_new, axis=1))
```

```python
import functools

import jax
import jax.numpy as jnp
from jax import lax
from jax.experimental import pallas as pl
from jax.experimental.pallas import tpu as pltpu

F32 = jnp.float32
BF = jnp.bfloat16
I32 = jnp.int32

GRID_W = 64
N_HEADS = 8
N_KV_HEADS = 2
HEAD_DIM = 64
GQA = N_HEADS // N_KV_HEADS
ROPE_THETA = 10000.0
ROT_PAIRS = HEAD_DIM // 4
Q_DIM = N_HEADS * HEAD_DIM
KV_DIM = N_KV_HEADS * HEAD_DIM
SSM_WIDTH = 512
SSM_HEADS = 8
SSM_HEAD_DIM = 64
SSM_GROUPS = 2
SSM_STATE = 64
CONV_W = 5
CONV_PAD = CONV_W // 2
CHUNK = 128
BC_DIM = SSM_GROUPS * SSM_STATE
CONV_DIM = SSM_WIDTH + 2 * BC_DIM
N_EXPERTS = 16
EC_FACTOR = 2
EPS = 1e-6
SCALE = HEAD_DIM ** -0.5

LANES = 128
SUBLANES = 8
TOKEN_TILE = 512
QUERY_TILE = 256
HALO = SUBLANES
VMEM_LIMIT = 56 * 1024 * 1024

NT_DIMS = (((1,), (1,)), ((), ()))


def _dot(a, b):
    return jnp.dot(a, b, preferred_element_type=F32)


def _dot_nt(a, b):
    return lax.dot_general(a, b, NT_DIMS, preferred_element_type=F32)


def _dot_exact(a, b):
    return jnp.dot(a, b, preferred_element_type=F32, precision=lax.Precision.HIGHEST)


def _silu(x):
    return x / (1.0 + jnp.exp(-x))


def _softplus(x):
    return jnp.maximum(x, 0.0) + jnp.log(1.0 + jnp.exp(-jnp.abs(x)))


def _params(sem):
    return pltpu.CompilerParams(dimension_semantics=sem, vmem_limit_bytes=VMEM_LIMIT)


def _ada_kernel(cv_ref, w_ref, b_ref, o_ref):
    s = _silu(cv_ref[...]).astype(BF)
    o_ref[...] = _dot(s, w_ref[...].astype(BF)) + b_ref[...]


def _ada_modulation(cv, ada_w, ada_b):
    depth, d, n6 = ada_w.shape
    rows = cv.shape[0]
    tn = n6 // 4
    return pl.pallas_call(
        _ada_kernel,
        grid=(depth, n6 // tn),
        in_specs=[
            pl.BlockSpec((rows, d), lambda l, j: (0, 0)),
            pl.BlockSpec((None, d, tn), lambda l, j: (l, 0, j)),
            pl.BlockSpec((None, 1, tn), lambda l, j: (l, 0, j)),
        ],
        out_specs=pl.BlockSpec((None, rows, tn), lambda l, j: (l, 0, j)),
        out_shape=jax.ShapeDtypeStruct((depth, rows, n6), F32),
        compiler_params=_params(("parallel", "parallel")),
        name="ada_modulation",
    )(cv, ada_w, ada_b.reshape(depth, 1, n6))


def _inproj_kernel(x_ref, mod_ref, n1_ref, wq_ref, wk_ref, wv_ref, wx_ref, wz_ref, wdt_ref, wdtt_ref,
                   bdq_ref, bdk_ref, gq_ref, gk_ref, cos_ref, sin_ref,
                   q_out, k_out, v_out, xbc_out, z_out, dt_out, dtt_out):
    d = x_ref.shape[-1]
    x = x_ref[...]
    mod = mod_ref[...]
    sh1 = mod[:, 0:d]
    sc1 = mod[:, d:2 * d]
    ms = jnp.mean(x * x, axis=-1, keepdims=True)
    h = x * lax.rsqrt(ms + EPS) * n1_ref[...]
    hb = (h * (1.0 + sc1) + sh1).astype(BF)
    cos = cos_ref[...]
    sin = sin_ref[...]

    def norm_rope(t, bd_ref, g_ref, cos_t, sin_t):
        ss = _dot((t * t).astype(BF), bd_ref[...])
        tn = t * lax.rsqrt(ss * (1.0 / HEAD_DIM) + EPS) * g_ref[...]
        width = t.shape[-1]
        lane = lax.broadcasted_iota(I32, tn.shape, 1)
        first_half = (lane % HEAD_DIM) < (HEAD_DIM // 2)
        partner = jnp.where(first_half,
                            pltpu.roll(tn, width - HEAD_DIM // 2, 1),
                            pltpu.roll(tn, HEAD_DIM // 2, 1))
        return tn * cos_t + partner * sin_t

    q = norm_rope(_dot(hb, wq_ref[...]), bdq_ref, gq_ref, cos, sin)
    q_out[...] = (q * SCALE).astype(BF)
    k_out[...] = norm_rope(_dot(hb, wk_ref[...]), bdk_ref, gk_ref, cos[:, :KV_DIM], sin[:, :KV_DIM])
    v_out[...] = _dot(hb, wv_ref[...])
    xbc_out[...] = _dot(hb, wx_ref[...])
    z_out[...] = _dot(hb, wz_ref[...])
    dt_out[...] = _dot(hb, wdt_ref[...])
    dtt_out[...] = _dot_nt(wdtt_ref[...], hb)


def _in_projection(x, mod3, lw, consts, geom):
    t, d = x.shape
    tm = TOKEN_TILE
    full = lambda a: pl.BlockSpec(a.shape, lambda i: (0,) * a.ndim)
    row = lambda w: pl.BlockSpec((tm, w), lambda i: (i, 0))
    weights = [lw["wq"], lw["wk"], lw["wv"], lw["wx"], lw["wz"], lw["wdt"], lw["wdtt"],
               consts["bdq"], consts["bdk"], lw["gq"], lw["gk"]]
    out_shape = [
        jax.ShapeDtypeStruct((t, Q_DIM), BF),
        jax.ShapeDtypeStruct((t, KV_DIM), F32),
        jax.ShapeDtypeStruct((t, KV_DIM), F32),
        jax.ShapeDtypeStruct((t, CONV_DIM), F32),
        jax.ShapeDtypeStruct((t, SSM_WIDTH), F32),
        jax.ShapeDtypeStruct((t, 2 * SSM_HEADS), F32),
        jax.ShapeDtypeStruct((2 * SSM_HEADS, t), F32),
    ]
    return pl.pallas_call(
        _inproj_kernel,
        grid=(t // tm,),
        in_specs=[row(d),
                  pl.BlockSpec((None, 1, mod3.shape[-1]), lambda i: (geom.mod_row(i), 0, 0)),
                  full(lw["n1"])]
                 + [full(w) for w in weights]
                 + [pl.BlockSpec((tm, Q_DIM), lambda i: (geom.rope_block(i), 0)),
                    pl.BlockSpec((tm, Q_DIM), lambda i: (geom.rope_block(i), 0))],
        out_specs=[row(Q_DIM), row(KV_DIM), row(KV_DIM), row(CONV_DIM), row(SSM_WIDTH),
                   row(2 * SSM_HEADS), pl.BlockSpec((2 * SSM_HEADS, tm), lambda i: (0, i))],
        out_shape=out_shape,
        compiler_params=_params(("parallel",)),
        name="in_projection",
    )(x, mod3, lw["n1"], *weights, consts["cos"], consts["sin"])


def _head_pair_operands(k, v):
    lane = lax.broadcasted_iota(I32, k.shape, 1)
    low = lane < HEAD_DIM
    k_sw = pltpu.roll(k, HEAD_DIM, 1)
    v_sw = pltpu.roll(v, HEAD_DIM, 1)
    zero = jnp.zeros_like(v)
    kk = [jnp.where(low, k, k_sw).astype(BF), jnp.where(low, k_sw, k).astype(BF)]
    v_lo = [jnp.where(low, v, zero).astype(BF), jnp.where(low, v_sw, zero).astype(BF)]
    v_hi = [jnp.where(low, zero, v_sw).astype(BF), jnp.where(low, zero, v).astype(BF)]
    return kk, v_lo, v_hi


def _attend(q_ref, o_ref, key_sets):
    tq = q_ref.shape[0]
    lane = lax.broadcasted_iota(I32, (tq, LANES), 1)
    low = lane < HEAD_DIM
    for pair in range(N_HEADS // 2):
        g = (2 * pair) // GQA
        q2 = q_ref[:, pair * LANES:(pair + 1) * LANES]
        zero = jnp.zeros_like(q2)
        acc = jnp.zeros((tq, LANES), F32)
        inv = []
        for half, qh in enumerate((jnp.where(low, q2, zero), jnp.where(low, zero, q2))):
            scores = [_dot_nt(qh, ks[0][g]) for ks in key_sets]
            m = scores[0].max(axis=-1, keepdims=True)
            for s in scores[1:]:
                m = jnp.maximum(m, s.max(axis=-1, keepdims=True))
            denom = jnp.zeros((tq, 1), F32)
            for s, ks in zip(scores, key_sets):
                p = jnp.exp(s - m)
                denom = denom + p.sum(axis=-1, keepdims=True)
                acc = acc + _dot(p.astype(BF), ks[1 + half][g])
            inv.append(1.0 / denom)
        o_ref[:, pair * LANES:(pair + 1) * LANES] = (acc * jnp.where(low, inv[0], inv[1])).astype(BF)


def _attn_ctx_kernel(q_ref, k_ref, v_ref, o_ref):
    _attend(q_ref, o_ref, [_head_pair_operands(k_ref[...], v_ref[...])])


def _attn_lat_kernel(q_ref, kc_ref, vc_ref, kl_ref, vl_ref, o_ref):
    _attend(q_ref, o_ref, [_head_pair_operands(kc_ref[...], vc_ref[...]),
                           _head_pair_operands(kl_ref[...], vl_ref[...])])


def _attention(q, k, v, cache_k4, cache_v4, layer, geom):
    seq = geom.seq
    o_ctx = pl.pallas_call(
        _attn_ctx_kernel,
        grid=(geom.batch,),
        in_specs=[pl.BlockSpec((seq, Q_DIM), lambda b: (b, 0)),
                  pl.BlockSpec((seq, KV_DIM), lambda b: (b, 0)),
                  pl.BlockSpec((seq, KV_DIM), lambda b: (b, 0))],
        out_specs=pl.BlockSpec((seq, Q_DIM), lambda b: (b, 0)),
        out_shape=jax.ShapeDtypeStruct((geom.n_ctx, Q_DIM), BF),
        compiler_params=_params(("parallel",)),
        name="attention_context",
    )(q, k, v)

    tq = QUERY_TILE
    dseq = geom.dec_seq
    past = cache_k4.shape[2]
    q_tiles = dseq // tq
    ctx_tiles = geom.n_ctx // tq
    ctx_seqs = geom.n_ctx // dseq
    o_lat = pl.pallas_call(
        _attn_lat_kernel,
        grid=(geom.dec_batch, q_tiles),
        in_specs=[pl.BlockSpec((tq, Q_DIM), lambda b, i: (ctx_tiles + b * q_tiles + i, 0)),
                  pl.BlockSpec((None, None, past, KV_DIM), lambda b, i: (b, layer, 0, 0)),
                  pl.BlockSpec((None, None, past, KV_DIM), lambda b, i: (b, layer, 0, 0)),
                  pl.BlockSpec((dseq, KV_DIM), lambda b, i: (ctx_seqs + b, 0)),
                  pl.BlockSpec((dseq, KV_DIM), lambda b, i: (ctx_seqs + b, 0))],
        out_specs=pl.BlockSpec((tq, Q_DIM), lambda b, i: (b * q_tiles + i, 0)),
        out_shape=jax.ShapeDtypeStruct((geom.n_lat, Q_DIM), BF),
        compiler_params=_params(("parallel", "arbitrary")),
        name="attention_latent",
    )(q, cache_k4, cache_v4, k, v)
    return o_ctx, o_lat


def _ssd_kernel(*refs, n_chunks, has_h0, emit_state):
    it = iter(refs)
    xbc_ref, z_ref, dt_ref, dtt_ref = next(it), next(it), next(it), next(it)
    h0_ref = next(it) if has_h0 else None
    (cw_ref, cb_ref, alog_r_ref, alog_c_ref, dtb_r_ref, dtb_c_ref,
     dskip_ref, gn_ref) = (next(it) for _ in range(8))
    y_out = next(it)
    hfin_out = next(it) if emit_state else None
    pad_ref, u_ref, csc_ref, csr_ref, dtr_ref, tot_ref, st_ref, hcur_ref, y_ref = (next(it) for _ in range(9))

    seq = xbc_ref.shape[0]
    nh = SSM_HEADS
    q = CHUNK

    pad_ref[0:HALO, :] = jnp.zeros((HALO, CONV_DIM), F32)
    pad_ref[HALO:HALO + seq, :] = xbc_ref[...]
    pad_ref[HALO + seq:2 * HALO + seq, :] = jnp.zeros((HALO, CONV_DIM), F32)

    a_row = -jnp.exp(alog_r_ref[...])
    a_col = -jnp.exp(alog_c_ref[...])
    ii = lax.broadcasted_iota(I32, (q, q), 0)
    jj = lax.broadcasted_iota(I32, (q, q), 1)
    lower = jj <= ii
    upper = jj >= ii
    tri_lower = lower.astype(F32)
    tri_upper = upper.astype(F32)
    fwd_lane = lax.broadcasted_iota(I32, (q, 2 * nh), 1) < nh
    fwd_row = lax.broadcasted_iota(I32, (2 * nh, q), 0) < nh

    def chunk_stats(c, carry):
        base = pl.multiple_of(c * q, q)
        win = pad_ref[pl.ds(base, q + 2 * HALO), :]
        acc = jnp.zeros((q, CONV_DIM), F32) + cb_ref[...]
        for tap in range(CONV_W):
            off = HALO - CONV_PAD + tap
            acc = acc + win[off:off + q, :] * cw_ref[tap:tap + 1, :]
        u = _silu(acc)
        u_ref[pl.ds(base, q), :] = u

        dtc = _softplus(dt_ref[pl.ds(base, q), :] + dtb_r_ref[...])
        dtr = _softplus(dtt_ref[:, pl.ds(base, q)] + dtb_c_ref[...])
        ac = dtc * a_row
        ar = dtr * a_col
        cs_col = jnp.where(fwd_lane, _dot_exact(tri_lower, ac), _dot_exact(tri_upper, ac))
        cs_row = jnp.where(fwd_row, _dot_exact(ar, tri_upper), _dot_exact(ar, tri_lower))
        csc_ref[pl.ds(base, q), :] = cs_col
        csr_ref[:, pl.ds(base, q)] = cs_row
        dtr_ref[:, pl.ds(base, q)] = dtr
        total = jnp.where(fwd_lane[0:1, :], cs_col[q - 1:q, :], cs_col[0:1, :])
        tot_ref[c] = total
        w_end = jnp.exp(total - cs_col) * dtc

        xs_t = u[:, 0:SSM_WIDTH].T.astype(BF)
        bmat = u[:, SSM_WIDTH:SSM_WIDTH + BC_DIM]
        for dh in range(2 * nh):
            head = dh % nh
            grp = head // (nh // SSM_GROUPS)
            bw = (bmat[:, grp * SSM_STATE:(grp + 1) * SSM_STATE] * w_end[:, dh:dh + 1]).astype(BF)
            st_ref[c, dh] = _dot(xs_t[head * SSM_HEAD_DIM:(head + 1) * SSM_HEAD_DIM, :], bw)
        return carry

    lax.fori_loop(0, n_chunks, chunk_stats, 0)

    for dh in range(2 * nh):
        hcur_ref[dh] = h0_ref[dh // nh, dh % nh] if has_h0 else jnp.zeros((SSM_HEAD_DIM, SSM_STATE), F32)

    def recur(step, carry):
        for direction in range(2):
            c = step if direction == 0 else n_chunks - 1 - step
            decay = jnp.exp(tot_ref[c])
            for head in range(nh):
                dh = direction * nh + head
                new = st_ref[c, dh]
                cur = hcur_ref[dh]
                st_ref[c, dh] = cur
                hcur_ref[dh] = cur * decay[:, dh:dh + 1] + new
        return carry

    lax.fori_loop(0, n_chunks, recur, 0)
    if emit_state:
        for dh in range(2 * nh):
            hfin_out[dh // nh, dh % nh] = hcur_ref[dh]

    def chunk_out(c, carry):
        base = pl.multiple_of(c * q, q)
        u = u_ref[pl.ds(base, q), :]
        xs = u[:, 0:SSM_WIDTH]
        xs_b = xs.astype(BF)
        bmat = u[:, SSM_WIDTH:SSM_WIDTH + BC_DIM].astype(BF)
        cmat = u[:, SSM_WIDTH + BC_DIM:CONV_DIM]
        cmat_b = cmat.astype(BF)
        cs_col = csc_ref[pl.ds(base, q), :]
        cs_row = csr_ref[:, pl.ds(base, q)]
        dtr = dtr_ref[:, pl.ds(base, q)]
        e_col = jnp.exp(cs_col)
        for grp in range(SSM_GROUPS):
            gs = slice(grp * SSM_STATE, (grp + 1) * SSM_STATE)
            cb = _dot_nt(cmat_b[:, gs], bmat[:, gs])
            for head in range(grp * (nh // SSM_GROUPS), (grp + 1) * (nh // SSM_GROUPS)):
                hb = nh + head
                seg_f = jnp.where(lower, cs_col[:, head:head + 1] - cs_row[head:head + 1, :], -1e30)
                seg_b = jnp.where(upper, cs_col[:, hb:hb + 1] - cs_row[hb:hb + 1, :], -1e30)
                mix = cb * (jnp.exp(seg_f) * dtr[head:head + 1, :] + jnp.exp(seg_b) * dtr[hb:hb + 1, :])
                hs = slice(head * SSM_HEAD_DIM, (head + 1) * SSM_HEAD_DIM)
                y = _dot(mix.astype(BF), xs_b[:, hs])
                y = y + _dot_nt((cmat[:, gs] * e_col[:, head:head + 1]).astype(BF), st_ref[c, head].astype(BF))
                y = y + _dot_nt((cmat[:, gs] * e_col[:, hb:hb + 1]).astype(BF), st_ref[c, hb].astype(BF))
                y_ref[:, hs] = y
        y = y_ref[...] + xs * dskip_ref[...]
        y = y * _silu(z_ref[pl.ds(base, q), :])
        ms = jnp.mean(y * y, axis=-1, keepdims=True)
        y_out[pl.ds(base, q), :] = (y * lax.rsqrt(ms + EPS) * gn_ref[...]).astype(BF)
        return carry

    lax.fori_loop(0, n_chunks, chunk_out, 0)


def _ssd_call(xbc, z, dt, dtt, h0, layer, lw, *, seq, n_seq, row_block0, emit_state, name):
    n_chunks = seq // CHUNK
    nh2 = 2 * SSM_HEADS
    full = lambda a: pl.BlockSpec(a.shape, lambda b: (0,) * a.ndim)
    params = [lw["conv_w"], lw["conv_b"], lw["alog_r"], lw["alog_c"], lw["dtb_r"], lw["dtb_c"],
              lw["dskip"], lw["gn"]]
    in_specs = [pl.BlockSpec((seq, CONV_DIM), lambda b: (row_block0 + b, 0)),
                pl.BlockSpec((seq, SSM_WIDTH), lambda b: (row_block0 + b, 0)),
                pl.BlockSpec((seq, nh2), lambda b: (row_block0 + b, 0)),
                pl.BlockSpec((nh2, seq), lambda b: (0, row_block0 + b))]
    args = [xbc, z, dt, dtt]
    if h0 is not None:
        in_specs.append(pl.BlockSpec((None, None, 2, SSM_HEADS, SSM_HEAD_DIM, SSM_STATE),
                                     lambda b: (b, layer, 0, 0, 0, 0)))
        args.append(h0)
    in_specs += [full(p) for p in params]
    args += params
    out_specs = [pl.BlockSpec((seq, SSM_WIDTH), lambda b: (b, 0))]
    out_shape = [jax.ShapeDtypeStruct((n_seq * seq, SSM_WIDTH), BF)]
    if emit_state:
        out_specs.append(pl.BlockSpec((None, 2, SSM_HEADS, SSM_HEAD_DIM, SSM_STATE),
                                      lambda b: (b, 0, 0, 0, 0)))
        out_shape.append(jax.ShapeDtypeStruct((n_seq, 2, SSM_HEADS, SSM_HEAD_DIM, SSM_STATE), F32))
    scratch = [
        pltpu.VMEM((seq + 2 * HALO, CONV_DIM), F32),
        pltpu.VMEM((seq, CONV_DIM), F32),
        pltpu.VMEM((seq, nh2), F32),
        pltpu.VMEM((nh2, seq), F32),
        pltpu.VMEM((nh2, seq), F32),
        pltpu.VMEM((n_chunks, 1, nh2), F32),
        pltpu.VMEM((n_chunks, nh2, SSM_HEAD_DIM, SSM_STATE), F32),
        pltpu.VMEM((nh2, SSM_HEAD_DIM, SSM_STATE), F32),
        pltpu.VMEM((CHUNK, SSM_WIDTH), F32),
    ]
    return pl.pallas_call(
        functools.partial(_ssd_kernel, n_chunks=n_chunks, has_h0=h0 is not None, emit_state=emit_state),
        grid=(n_seq,),
        in_specs=in_specs,
        out_specs=out_specs,
        out_shape=out_shape,
        scratch_shapes=scratch,
        compiler_params=_params(("parallel",)),
        name=name,
    )(*args)


def _ssd_mixer(xbc, z, dt, dtt, state_ssm, layer, lw, geom):
    y_ctx, h_fin = _ssd_call(xbc, z, dt, dtt, None, layer, lw, seq=geom.seq, n_seq=geom.batch,
                             row_block0=0, emit_state=True, name="ssd_context")
    (y_lat,) = _ssd_call(xbc, z, dt, dtt, state_ssm, layer, lw, seq=geom.dec_seq,
                         n_seq=geom.dec_batch, row_block0=geom.n_ctx // geom.dec_seq,
                         emit_state=False, name="ssd_latent")
    return y_ctx, y_lat, h_fin


def _outproj_kernel(ac_ref, al_ref, sc_ref, sl_ref, x_ref, mod_ref, n2_ref, woa_ref, wos_ref, wrh_ref, wrl_ref,
                    x1_out, h2_out, afft_out, *, ctx_tiles):
    d = x_ref.shape[-1]
    mod = mod_ref[...]
    g1 = mod[:, 2 * d:3 * d]
    sh2 = mod[:, 3 * d:4 * d]
    sc2 = mod[:, 4 * d:5 * d]
    is_ctx = pl.program_id(0) < ctx_tiles
    attn = jnp.where(is_ctx, ac_ref[...], al_ref[...])
    ssm = jnp.where(is_ctx, sc_ref[...], sl_ref[...])
    y = _dot(attn, woa_ref[...]) + _dot(ssm, wos_ref[...])
    x1 = x_ref[...] + g1 * y
    x1_out[...] = x1
    ms = jnp.mean(x1 * x1, axis=-1, keepdims=True)
    h2 = x1 * lax.rsqrt(ms + EPS) * n2_ref[...]
    h2 = h2 * (1.0 + sc2) + sh2
    h2_out[...] = h2
    hh = h2.astype(BF)
    hl = (h2 - hh.astype(F32)).astype(BF)
    wh = wrh_ref[...]
    logits = _dot_nt(wh, hh) + _dot_nt(wh, hl) + _dot_nt(wrl_ref[...], hh)
    logits = logits - logits.max(axis=0, keepdims=True)
    e = jnp.exp(logits)
    afft_out[...] = e / e.sum(axis=0, keepdims=True)


def _out_projection(attn_c, attn_l, ssm_c, ssm_l, x, mod3, lw, geom):
    t, d = x.shape
    tm = TOKEN_TILE
    nc = geom.ctx_tiles
    full = lambda a: pl.BlockSpec(a.shape, lambda i: (0,) * a.ndim)
    row = lambda w: pl.BlockSpec((tm, w), lambda i: (i, 0))
    ctx_row = lambda w: pl.BlockSpec((tm, w), lambda i: (jnp.minimum(i, nc - 1), 0))
    lat_row = lambda w: pl.BlockSpec((tm, w), lambda i: (jnp.maximum(i - nc, 0), 0))
    weights = [lw["n2"], lw["woa"], lw["wos"], lw["wrh"], lw["wrl"]]
    return pl.pallas_call(
        functools.partial(_outproj_kernel, ctx_tiles=nc),
        grid=(t // tm,),
        in_specs=[ctx_row(Q_DIM), lat_row(Q_DIM), ctx_row(SSM_WIDTH), lat_row(SSM_WIDTH), row(d),
                  pl.BlockSpec((None, 1, mod3.shape[-1]), lambda i: (geom.mod_row(i), 0, 0))]
                 + [full(w) for w in weights],
        out_specs=[row(d), row(d), pl.BlockSpec((N_EXPERTS, tm), lambda i: (0, i))],
        out_shape=[jax.ShapeDtypeStruct((t, d), F32), jax.ShapeDtypeStruct((t, d), F32),
                   jax.ShapeDtypeStruct((N_EXPERTS, t), F32)],
        compiler_params=_params(("parallel",)),
        name="out_projection",
    )(attn_c, attn_l, ssm_c, ssm_l, x, mod3, *weights)


def _flat_cumsum(m2, rows):
    gr = m2.shape[0]
    li = lax.broadcasted_iota(I32, (LANES, LANES), 0)
    lj = lax.broadcasted_iota(I32, (LANES, LANES), 1)
    mb = m2.astype(BF)
    within = _dot(mb, (li <= lj).astype(BF))
    row_tot = _dot(mb, jnp.ones((LANES, LANES), BF))
    ri = lax.broadcasted_iota(I32, (gr, gr), 0)
    rj = lax.broadcasted_iota(I32, (gr, gr), 1)
    before = ((ri // rows == rj // rows) & (rj < ri)).astype(BF)
    hi, lo = _split_base128(row_tot)
    return within + (LANES * _dot(before, hi) + _dot(before, lo))


def _split_base128(v):
    hi = jnp.floor(v * (1.0 / LANES))
    return hi.astype(BF), (v - hi * LANES).astype(BF)


def _plan_kernel(aff_ref, idx_out, gate_out, dslot_out, start_out, cnt_out, *, cap):
    a = aff_ref[...]
    n_e, rows, _ = a.shape

    def count(hit):
        return hit.astype(I32).sum(axis=2, keepdims=True).sum(axis=1, keepdims=True)

    def search(step, lo):
        cand = lo | (1 << (30 - step))
        return jnp.where(count(a >= lax.bitcast_convert_type(cand, F32)) >= cap, cand, lo)

    thr_bits = lax.fori_loop(0, 31, search, jnp.zeros((n_e, 1, 1), I32))
    thr = lax.bitcast_convert_type(thr_bits, F32)
    gt = a > thr
    eq = a == thr
    need = cap - count(gt)
    eqf = eq.astype(F32).reshape(n_e * rows, LANES)
    rank = (_flat_cumsum(eqf, rows) - eqf).reshape(n_e, rows, LANES)
    mask = gt | (eq & (rank < need.astype(F32)))
    mf = mask.astype(F32)

    c3 = _flat_cumsum(mf.reshape(n_e * rows, LANES), rows).reshape(n_e, rows, LANES)
    ahead = []
    run = jnp.zeros((rows, LANES), F32)
    for e in range(n_e):
        ahead.append(run)
        run = run + mf[e]
    cnt = run
    start = _flat_cumsum(cnt, rows) - cnt
    start_out[...] = start.astype(I32)
    cnt_out[...] = cnt.astype(I32)

    s_lane = lax.broadcasted_iota(I32, (1, cap), 1).astype(F32)
    r_col = lax.broadcasted_iota(I32, (LANES, cap), 0).astype(F32)
    pad = jnp.zeros((LANES - rows, LANES), F32)

    def lanes_first(xm):
        return (jnp.concatenate([xm, pad], axis=0) if rows < LANES else xm).T

    for e in range(n_e):
        c_e = c3[e]
        row_of = (c_e[:, LANES - 1:LANES] <= s_lane).astype(F32).sum(axis=0, keepdims=True)
        onehot = (r_col == row_of).astype(BF)
        a_t = lanes_first(a[e])
        a1 = a_t.astype(BF)
        r1 = a_t - a1.astype(F32)
        a2 = r1.astype(BF)
        a3 = (r1 - a2.astype(F32)).astype(BF)
        pieces = _split_base128(lanes_first(c_e)) + (a1, a2, a3) + _split_base128(lanes_first(start + ahead[e]))
        sel = _dot(jnp.concatenate(pieces, axis=0), onehot)
        part = [sel[j * LANES:(j + 1) * LANES] for j in range(len(pieces))]
        sel_c = LANES * part[0] + part[1]
        sel_a = part[2] + (part[3] + part[4])
        sel_d = LANES * part[5] + part[6]
        lane_of = (sel_c <= s_lane).astype(F32).sum(axis=0, keepdims=True)
        pick = r_col == lane_of
        idx_out[e:e + 1, :] = (row_of * LANES + lane_of).astype(I32)
        gate_out[e:e + 1, :] = jnp.where(pick, sel_a, 0.0).sum(axis=0, keepdims=True)
        dslot_out[e:e + 1, :] = jnp.where(pick, sel_d, 0.0).sum(axis=0, keepdims=True).astype(I32)


def _plan(aff_t, n_sets, cap):
    n_e, t = aff_t.shape
    n = t // n_sets
    rows = n // LANES
    assert rows <= LANES
    aff4 = aff_t.reshape(n_e, n_sets, rows, LANES)
    slot_spec = pl.BlockSpec((None, n_e, cap), lambda s: (s, 0, 0))
    tok_spec = pl.BlockSpec((None, rows, LANES), lambda s: (s, 0, 0))
    slot_shape = lambda dt: jax.ShapeDtypeStruct((n_sets, n_e, cap), dt)
    tok_shape = jax.ShapeDtypeStruct((n_sets, rows, LANES), I32)
    return pl.pallas_call(
        functools.partial(_plan_kernel, cap=cap),
        grid=(n_sets,),
        in_specs=[pl.BlockSpec((n_e, None, rows, LANES), lambda s: (0, s, 0, 0))],
        out_specs=[slot_spec, slot_spec, slot_spec, tok_spec, tok_spec],
        out_shape=[slot_shape(I32), slot_shape(F32), slot_shape(I32), tok_shape, tok_shape],
        compiler_params=_params(("parallel",)),
        name="expert_choice_plan",
    )(aff4)


DMA_UNROLL = 8
ROW_BLOCK = 512
COMBINE_ROWS = 256
COMBINE_BUFFERS = 4


def _expert_kernel(gidx_ref, dest_ref, h2p_hbm, gate_ref, wg_ref, wu_ref, wd_ref, z_hbm,
                   xw_a, xw_b, xb_ref, acc_a, acc_b, wgb_ref, wub_ref, wdb_ref, sem_g, sem_s, *, n_f, row_block):
    e = pl.program_id(0)
    f = pl.program_id(1)
    n_e = pl.num_programs(0)
    m_rows = xw_a.shape[0]
    n_blocks = m_rows // row_block
    per_block = m_rows // (n_f * n_blocks)

    def vmem_row(buf, m8, k):
        tile = pl.multiple_of(m8 + (k // SUBLANES) * SUBLANES, SUBLANES)
        return buf.at[pl.ds(tile, SUBLANES), :].at[pl.ds(k % SUBLANES, 1), :]

    def gather_copy(entry, m8, k, xw):
        return pltpu.make_async_copy(h2p_hbm.at[pl.ds(gidx_ref[entry], 1), :], vmem_row(xw, m8, k), sem_g)

    def scatter_copy(entry, m8, k, acc):
        return pltpu.make_async_copy(vmem_row(acc, m8, k), z_hbm.at[pl.ds(dest_ref[entry], 1), :], sem_s)

    def wait_gathers(xw):
        pltpu.make_async_copy(h2p_hbm.at[pl.ds(0, m_rows), :], xw, sem_g).wait()

    def wait_scatters(acc):
        pltpu.make_async_copy(acc, z_hbm.at[pl.ds(0, m_rows), :], sem_s).wait()

    def issue_all(make):
        def body(j, carry):
            for k in range(DMA_UNROLL):
                make(j * DMA_UNROLL, k).start()
            return carry
        lax.fori_loop(0, m_rows // DMA_UNROLL, body, 0)

    def run(xw_cur, xw_next, acc_cur, acc_prev):
        @pl.when((e == 0) & (f == 0))
        def _():
            acc_prev[...] = jnp.zeros_like(acc_prev)
            issue_all(lambda m8, k: gather_copy(m8 + k, m8, k, xw_cur))

        @pl.when(f == 0)
        def _():
            wait_gathers(xw_cur)
            xb_ref[...] = xw_cur[...].astype(BF)
            acc_cur[...] = jnp.zeros_like(acc_cur)

        wgb_ref[...] = wg_ref[...].astype(BF)
        wub_ref[...] = wu_ref[...].astype(BF)
        wdb_ref[...] = wd_ref[...].astype(BF)
        last = f == n_f - 1

        def block(rb, carry):
            r0 = pl.multiple_of(rb * row_block, row_block)
            m0 = (f * n_blocks + rb) * per_block

            def issue(part, parts=4):
                for k in range(part * per_block // parts, (part + 1) * per_block // parts):
                    gather_copy((e + 1) * m_rows + m0 + k, m0, k, xw_next).start()
                    scatter_copy(e * m_rows + m0 + k, m0, k, acc_prev).start()

            x = xb_ref[pl.ds(r0, row_block), :]
            issue(0)
            g = _dot(x, wgb_ref[...])
            issue(1)
            u = _dot(x, wub_ref[...])
            issue(2)
            hid = (_silu(g) * u).astype(BF)
            total = acc_cur[pl.ds(r0, row_block), :] + _dot(hid, wdb_ref[...])
            issue(3)
            gate = jnp.where(last, gate_ref[pl.ds(r0, row_block), :], 1.0)
            acc_cur[pl.ds(r0, row_block), :] = total * gate
            return carry

        lax.fori_loop(0, n_blocks, block, 0)

        @pl.when(last)
        def _():
            wait_scatters(acc_prev)

        @pl.when(last & (e == n_e - 1))
        def _():
            wait_gathers(xw_next)
            issue_all(lambda m8, k: scatter_copy((e + 1) * m_rows + m8 + k, m8, k, acc_cur))
            wait_scatters(acc_cur)

    @pl.when(e % 2 == 0)
    def _():
        run(xw_a, xw_b, acc_a, acc_b)

    @pl.when(e % 2 == 1)
    def _():
        run(xw_b, xw_a, acc_b, acc_a)


def _expert_ffn(gidx, dest, h2p, gates, w_gate, w_up, w_down, layer):
    n_e, m, _ = gates.shape
    d = h2p.shape[1]
    ff = w_gate.shape[-1]
    tf = 256
    n_f = ff // tf
    row_block = min(ROW_BLOCK, m)
    assert m % DMA_UNROLL == 0 and (n_e * m) % COMBINE_ROWS == 0 and m % (n_f * (m // row_block)) == 0
    gidx_ext = jnp.concatenate([gidx, jnp.arange(m, dtype=I32)])
    dest_ext = jnp.concatenate([n_e * m + jnp.arange(m, dtype=I32), dest])
    return pl.pallas_call(
        functools.partial(_expert_kernel, n_f=n_f, row_block=row_block),
        grid_spec=pltpu.PrefetchScalarGridSpec(
            num_scalar_prefetch=2,
            grid=(n_e, n_f),
            in_specs=[pl.BlockSpec(memory_space=pl.ANY),
                      pl.BlockSpec((None, m, 1), lambda e, f, gi, de: (e, 0, 0)),
                      pl.BlockSpec((None, None, d, tf), lambda e, f, gi, de: (layer, e, 0, f)),
                      pl.BlockSpec((None, None, d, tf), lambda e, f, gi, de: (layer, e, 0, f)),
                      pl.BlockSpec((None, None, tf, d), lambda e, f, gi, de: (layer, e, f, 0))],
            out_specs=pl.BlockSpec(memory_space=pl.ANY),
            scratch_shapes=[pltpu.VMEM((m, d), F32), pltpu.VMEM((m, d), F32),
                            pltpu.VMEM((m, d), BF), pltpu.VMEM((m, d), F32), pltpu.VMEM((m, d), F32),
                            pltpu.VMEM((d, tf), BF), pltpu.VMEM((d, tf), BF), pltpu.VMEM((tf, d), BF),
                            pltpu.SemaphoreType.DMA, pltpu.SemaphoreType.DMA]),
        out_shape=jax.ShapeDtypeStruct(((n_e + 1) * m, d), F32),
        compiler_params=_params(("arbitrary", "arbitrary")),
        name="expert_swiglu",
    )(gidx_ext, dest_ext, h2p, gates, w_gate, w_up, w_down)


def _combine_kernel(lo_ref, hi_ref, z_hbm, start_ref, end_ref, x1_ref, mod_ref, o_ref, zbuf, acc_ref, sem,
                    state, *, n_chunks):
    i = pl.program_id(0)
    d = x1_ref.shape[-1]
    n_buf, ch, _ = zbuf.shape
    lo = lo_ref[i]
    hi = hi_ref[i]

    def chunk_copy(kc):
        slot = kc % n_buf
        return pltpu.make_async_copy(z_hbm.at[pl.ds(pl.multiple_of(kc * ch, ch), ch), :], zbuf.at[slot],
                                     sem.at[slot])

    @pl.when(i == 0)
    def _():
        state[0] = 0
        state[1] = 0

    acc_ref[...] = jnp.zeros_like(acc_ref)
    start = start_ref[...]
    end = end_ref[...]

    def body(kc, carry):
        upto = jnp.minimum(kc + n_buf, n_chunks)

        def start_one(j, c):
            chunk_copy(j).start()
            return c

        lax.fori_loop(state[0], upto, start_one, 0)
        state[0] = jnp.maximum(state[0], upto)

        @pl.when(kc >= state[1])
        def _():
            chunk_copy(kc).wait()
            state[1] = kc + 1

        slot = kc % n_buf
        k = kc * ch + lax.broadcasted_iota(I32, (1, ch), 1)
        own = jnp.where((k >= start) & (k < end), 1.0, 0.0).astype(BF)
        zc = zbuf[slot]
        zh = zc.astype(BF)
        zl = (zc - zh.astype(F32)).astype(BF)
        acc_ref[...] += _dot(own, zh) + _dot(own, zl)
        return carry

    lax.fori_loop(lo, hi, body, 0)
    g2 = mod_ref[...][:, 5 * d:6 * d]
    o_ref[...] = x1_ref[...] + g2 * acc_ref[...]


def _combine(z, n_rows, start, end, tile_lo, tile_hi, x1, mod3, geom):
    t, d = x1.shape
    tm = TOKEN_TILE
    return pl.pallas_call(
        functools.partial(_combine_kernel, n_chunks=n_rows // COMBINE_ROWS),
        grid_spec=pltpu.PrefetchScalarGridSpec(
            num_scalar_prefetch=2,
            grid=(t // tm,),
            in_specs=[pl.BlockSpec(memory_space=pl.ANY),
                      pl.BlockSpec((tm, 1), lambda i, lo, hi: (i, 0)),
                      pl.BlockSpec((tm, 1), lambda i, lo, hi: (i, 0)),
                      pl.BlockSpec((tm, d), lambda i, lo, hi: (i, 0)),
                      pl.BlockSpec((None, 1, mod3.shape[-1]), lambda i, lo, hi: (geom.mod_row(i), 0, 0))],
            out_specs=pl.BlockSpec((tm, d), lambda i, lo, hi: (i, 0)),
            scratch_shapes=[pltpu.VMEM((COMBINE_BUFFERS, COMBINE_ROWS, d), F32), pltpu.VMEM((tm, d), F32),
                            pltpu.SemaphoreType.DMA((COMBINE_BUFFERS,)), pltpu.SMEM((2,), I32)]),
        out_shape=jax.ShapeDtypeStruct((t, d), F32),
        compiler_params=_params(("arbitrary",)),
        name="gated_combine",
    )(tile_lo, tile_hi, z, start, end, x1, mod3)


class _Geometry:
    def __init__(self, batch, seq, dec_batch, dec_seq):
        self.batch, self.seq, self.dec_batch, self.dec_seq = batch, seq, dec_batch, dec_seq
        self.n_ctx = batch * seq
        self.n_lat = dec_batch * dec_seq
        tm = TOKEN_TILE
        assert self.n_ctx % tm == 0 and dec_seq % tm == 0 and self.n_ctx % dec_seq == 0
        assert seq % CHUNK == 0 and dec_seq % CHUNK == 0 and dec_seq % QUERY_TILE == 0
        self.ctx_tiles = self.n_ctx // tm
        self.lat_tiles_per_seq = dec_seq // tm

    def mod_row(self, i):
        return jnp.where(i < self.ctx_tiles, 0, 1 + (i - self.ctx_tiles) // self.lat_tiles_per_seq)

    def rope_block(self, i):
        return jnp.where(i < self.ctx_tiles, 0, 1 + (i - self.ctx_tiles) % self.lat_tiles_per_seq)


def _rope_tables(dec_seq):
    t = jnp.arange(dec_seq)
    row = (t // GRID_W).astype(F32)
    col = (t % GRID_W).astype(F32)
    inv = ROPE_THETA ** (-jnp.arange(ROT_PAIRS, dtype=F32) / ROT_PAIRS)
    ang = jnp.concatenate([row[:, None] * inv, col[:, None] * inv], axis=-1)
    cos = jnp.tile(jnp.concatenate([jnp.cos(ang), jnp.cos(ang)], axis=-1), (1, N_HEADS))
    sin = jnp.tile(jnp.concatenate([-jnp.sin(ang), jnp.sin(ang)], axis=-1), (1, N_HEADS))
    ident = jnp.ones((TOKEN_TILE, Q_DIM), F32)
    return (jnp.concatenate([ident, cos], axis=0),
            jnp.concatenate([jnp.zeros_like(ident), sin], axis=0))


def _block_diag_ones(width):
    i = jnp.arange(width)
    return (i[:, None] // HEAD_DIM == i[None, :] // HEAD_DIM).astype(BF)


def _layer_weights(l, norm1, norm2, w_in, conv_w, conv_b, q_norm, k_norm, a_log, dt_bias, d_skip,
                   ssm_norm, w_out, w_router):
    w = w_in[l].astype(BF)
    o = 0
    parts = {}
    for name, width in (("wq", Q_DIM), ("wk", KV_DIM), ("wv", KV_DIM), ("wx", CONV_DIM),
                        ("wz", SSM_WIDTH), ("wdt", 2 * SSM_HEADS)):
        parts[name] = w[:, o:o + width]
        o += width
    wo = w_out[l].astype(BF)
    wr = w_router[l].T
    wrh = wr.astype(BF)
    nh2 = 2 * SSM_HEADS
    parts.update(
        wdtt=parts["wdt"].T,
        n1=norm1[l][None, :], n2=norm2[l][None, :],
        gq=jnp.tile(q_norm[l], N_HEADS)[None, :], gk=jnp.tile(k_norm[l], N_KV_HEADS)[None, :],
        conv_w=conv_w[l], conv_b=conv_b[l][None, :],
        alog_r=a_log[l].reshape(1, nh2), alog_c=a_log[l].reshape(nh2, 1),
        dtb_r=dt_bias[l].reshape(1, nh2), dtb_c=dt_bias[l].reshape(nh2, 1),
        dskip=jnp.repeat(d_skip[l], SSM_HEAD_DIM)[None, :], gn=ssm_norm[l][None, :],
        woa=wo[:Q_DIM], wos=wo[Q_DIM:],
        wrh=wrh, wrl=(wr - wrh.astype(F32)).astype(BF),
    )
    return parts


def _moe(h2p, aff_t, x1, mod3, w_gate, w_up, w_down, layer, n_sets, geom):
    t = h2p.shape[0]
    n = t // n_sets
    cap = EC_FACTOR * n // N_EXPERTS
    pairs = N_EXPERTS * cap
    idx, gates, dslot, start, cnt = _plan(aff_t, n_sets, cap)
    set_ix = jnp.arange(n_sets, dtype=I32)[:, None, None]
    expert_major = lambda a: a.transpose(1, 0, 2).reshape(N_EXPERTS, n_sets * cap)
    gidx = expert_major(idx + set_ix * n).reshape(-1)
    dest = expert_major(dslot + set_ix * pairs).reshape(-1)
    z = _expert_ffn(gidx, dest, h2p, expert_major(gates)[..., None], w_gate, w_up, w_down, layer)
    start_g = (start.reshape(n_sets, n) + set_ix[:, :, 0] * pairs).reshape(t, 1)
    end_g = start_g + cnt.reshape(t, 1)
    tile_lo = start_g[::TOKEN_TILE, 0] // COMBINE_ROWS
    tile_hi = (end_g[TOKEN_TILE - 1::TOKEN_TILE, 0] + COMBINE_ROWS - 1) // COMBINE_ROWS
    return _combine(z, n_sets * pairs, start_g, end_g, tile_lo, tile_hi, x1, mod3, geom)


def kernel(x_prompt, x_sample, cache_k, cache_v, state_ssm, c, c_ctx, norm1, norm2, ada_w, ada_b,
           w_in, conv_w, conv_b, q_norm, k_norm, A_log, dt_bias, d_skip, ssm_norm, w_out,
           w_router, w_gate, w_up, w_down):
    batch, seq, d = x_prompt.shape
    dec_batch, dec_seq, _ = x_sample.shape
    depth = w_in.shape[0]
    past = cache_k.shape[2]
    geom = _Geometry(batch, seq, dec_batch, dec_seq)
    assert geom.n_ctx == geom.n_lat, "routed sets are stacked as two equal halves"

    rows = -(-(1 + dec_batch) // SUBLANES) * SUBLANES
    cv = jnp.zeros((rows, d), F32).at[0].set(c_ctx).at[1:1 + dec_batch].set(c)
    mod = _ada_modulation(cv, ada_w, ada_b)
    cos, sin = _rope_tables(dec_seq)
    consts = dict(cos=cos, sin=sin, bdq=_block_diag_ones(Q_DIM), bdk=_block_diag_ones(KV_DIM))
    cache_k4 = cache_k.reshape(dec_batch, depth, past, KV_DIM)
    cache_v4 = cache_v.reshape(dec_batch, depth, past, KV_DIM)

    x = jnp.concatenate([x_prompt.reshape(geom.n_ctx, d), x_sample.reshape(geom.n_lat, d)], axis=0)
    ks_new, vs_new, hs_new = [], [], []
    for l in range(depth):
        lw = _layer_weights(l, norm1, norm2, w_in, conv_w, conv_b, q_norm, k_norm, A_log, dt_bias,
                            d_skip, ssm_norm, w_out, w_router)
        mod3 = mod[l][:, None, :]
        q, k, v, xbc, z, dt, dtt = _in_projection(x, mod3, lw, consts, geom)
        attn_c, attn_l = _attention(q, k, v, cache_k4, cache_v4, l, geom)
        ssm_c, ssm_l, h_fin = _ssd_mixer(xbc, z, dt, dtt, state_ssm, l, lw, geom)
        x1, h2p, aff_t = _out_projection(attn_c, attn_l, ssm_c, ssm_l, x, mod3, lw, geom)
        x = _moe(h2p, aff_t, x1, mod3, w_gate, w_up, w_down, l, 2, geom)
        ks_new.append(k[:geom.n_ctx].reshape(batch, seq, N_KV_HEADS, HEAD_DIM))
        vs_new.append(v[:geom.n_ctx].reshape(batch, seq, N_KV_HEADS, HEAD_DIM))
        hs_new.append(h_fin)
    y_prompt = x[:geom.n_ctx].reshape(batch, seq, d)
    y_sample = x[geom.n_ctx:].reshape(dec_batch, dec_seq, d)
    return (y_prompt, y_sample, jnp.stack(ks_new, axis=1), jnp.stack(vs_new, axis=1),
            jnp.stack(hs_new, axis=1))
```

```python
import functools

import jax
import jax.numpy as jnp
from jax import lax
from jax.experimental import pallas as pl
from jax.experimental.pallas import tpu as pltpu

F32 = jnp.float32
BF = jnp.bfloat16
I32 = jnp.int32

GRID_W = 64
N_HEADS = 8
N_KV_HEADS = 2
HEAD_DIM = 64
GQA = N_HEADS // N_KV_HEADS
ROPE_THETA = 10000.0
ROT_PAIRS = HEAD_DIM // 4
Q_DIM = N_HEADS * HEAD_DIM
KV_DIM = N_KV_HEADS * HEAD_DIM
SSM_WIDTH = 512
SSM_HEADS = 8
SSM_HEAD_DIM = 64
SSM_GROUPS = 2
SSM_STATE = 64
CONV_W = 5
CONV_PAD = CONV_W // 2
CHUNK = 128
BC_DIM = SSM_GROUPS * SSM_STATE
CONV_DIM = SSM_WIDTH + 2 * BC_DIM
N_EXPERTS = 16
EC_FACTOR = 2
EPS = 1e-6
SCALE = HEAD_DIM ** -0.5
LOG2_E = 1.4426950408889634

LANES = 128
SUBLANES = 8
TOKEN_TILE = 512
QUERY_TILE = 256
HALO = SUBLANES
VMEM_LIMIT = 56 * 1024 * 1024

NT_DIMS = (((1,), (1,)), ((), ()))


def _dot(a, b):
    return jnp.dot(a, b, preferred_element_type=F32)


def _dot_nt(a, b):
    return lax.dot_general(a, b, NT_DIMS, preferred_element_type=F32)


def _dot_exact(a, b):
    return jnp.dot(a, b, preferred_element_type=F32, precision=lax.Precision.HIGHEST)


def _silu(x):
    return x / (1.0 + jnp.exp(-x))


def _softplus(x):
    return jnp.maximum(x, 0.0) + jnp.log(1.0 + jnp.exp(-jnp.abs(x)))


def _params(sem):
    return pltpu.CompilerParams(dimension_semantics=sem, vmem_limit_bytes=VMEM_LIMIT)


def _ada_kernel(cv_ref, w_ref, b_ref, o_ref):
    s = _silu(cv_ref[...]).astype(BF)
    o_ref[...] = _dot(s, w_ref[...].astype(BF)) + b_ref[...]


def _ada_modulation(cv, ada_w, ada_b):
    depth, d, n6 = ada_w.shape
    rows = cv.shape[0]
    tn = n6 // 4
    return pl.pallas_call(
        _ada_kernel,
        grid=(depth, n6 // tn),
        in_specs=[
            pl.BlockSpec((rows, d), lambda l, j: (0, 0)),
            pl.BlockSpec((None, d, tn), lambda l, j: (l, 0, j)),
            pl.BlockSpec((None, 1, tn), lambda l, j: (l, 0, j)),
        ],
        out_specs=pl.BlockSpec((None, rows, tn), lambda l, j: (l, 0, j)),
        out_shape=jax.ShapeDtypeStruct((depth, rows, n6), F32),
        compiler_params=_params(("parallel", "parallel")),
        name="ada_modulation",
    )(cv, ada_w, ada_b.reshape(depth, 1, n6))


def _inproj_kernel(x_ref, mod_ref, n1_ref, wq_ref, wk_ref, wv_ref, wx_ref, wz_ref, wdt_ref, wdtt_ref,
                   bdq_ref, bdk_ref, gq_ref, gk_ref, cos_ref, sin_ref,
                   q_out, k_out, v_out, xbc_out, z_out, dt_out, dtt_out):
    d = x_ref.shape[-1]
    x = x_ref[...]
    mod = mod_ref[...]
    sh1 = mod[:, 0:d]
    sc1 = mod[:, d:2 * d]
    ms = jnp.mean(x * x, axis=-1, keepdims=True)
    h = x * lax.rsqrt(ms + EPS) * n1_ref[...]
    hb = (h * (1.0 + sc1) + sh1).astype(BF)
    cos = cos_ref[...]
    sin = sin_ref[...]

    def norm_rope(t, bd_ref, g_ref, cos_t, sin_t):
        ss = _dot((t * t).astype(BF), bd_ref[...])
        tn = t * lax.rsqrt(ss * (1.0 / HEAD_DIM) + EPS) * g_ref[...]
        width = t.shape[-1]
        lane = lax.broadcasted_iota(I32, tn.shape, 1)
        first_half = (lane % HEAD_DIM) < (HEAD_DIM // 2)
        partner = jnp.where(first_half,
                            pltpu.roll(tn, width - HEAD_DIM // 2, 1),
                            pltpu.roll(tn, HEAD_DIM // 2, 1))
        return tn * cos_t + partner * sin_t

    q = norm_rope(_dot(hb, wq_ref[...]), bdq_ref, gq_ref, cos, sin)
    q_out[...] = (q * (SCALE * LOG2_E)).astype(BF)
    k_out[...] = norm_rope(_dot(hb, wk_ref[...]), bdk_ref, gk_ref, cos[:, :KV_DIM], sin[:, :KV_DIM])
    v_out[...] = _dot(hb, wv_ref[...])
    xbc_out[...] = _dot(hb, wx_ref[...])
    z_out[...] = _dot(hb, wz_ref[...])
    dt_out[...] = _dot(hb, wdt_ref[...])
    dtt_out[...] = _dot_nt(wdtt_ref[...], hb)


def _in_projection(x, mod3, lw, consts, geom):
    t, d = x.shape
    tm = TOKEN_TILE
    full = lambda a: pl.BlockSpec(a.shape, lambda i: (0,) * a.ndim)
    row = lambda w: pl.BlockSpec((tm, w), lambda i: (i, 0))
    weights = [lw["wq"], lw["wk"], lw["wv"], lw["wx"], lw["wz"], lw["wdt"], lw["wdtt"],
               consts["bdq"], consts["bdk"], lw["gq"], lw["gk"]]
    out_shape = [
        jax.ShapeDtypeStruct((t, Q_DIM), BF),
        jax.ShapeDtypeStruct((t, KV_DIM), F32),
        jax.ShapeDtypeStruct((t, KV_DIM), F32),
        jax.ShapeDtypeStruct((t, CONV_DIM), F32),
        jax.ShapeDtypeStruct((t, SSM_WIDTH), F32),
        jax.ShapeDtypeStruct((t, 2 * SSM_HEADS), F32),
        jax.ShapeDtypeStruct((2 * SSM_HEADS, t), F32),
    ]
    return pl.pallas_call(
        _inproj_kernel,
        grid=(t // tm,),
        in_specs=[row(d),
                  pl.BlockSpec((None, 1, mod3.shape[-1]), lambda i: (geom.mod_row(i), 0, 0)),
                  full(lw["n1"])]
                 + [full(w) for w in weights]
                 + [pl.BlockSpec((tm, Q_DIM), lambda i: (geom.rope_block(i), 0)),
                    pl.BlockSpec((tm, Q_DIM), lambda i: (geom.rope_block(i), 0))],
        out_specs=[row(Q_DIM), row(KV_DIM), row(KV_DIM), row(CONV_DIM), row(SSM_WIDTH),
                   row(2 * SSM_HEADS), pl.BlockSpec((2 * SSM_HEADS, tm), lambda i: (0, i))],
        out_shape=out_shape,
        compiler_params=_params(("parallel",)),
        name="in_projection",
    )(x, mod3, lw["n1"], *weights, consts["cos"], consts["sin"])


def _head_pair_operands(k, v):
    lane = lax.broadcasted_iota(I32, k.shape, 1)
    low = lane < HEAD_DIM
    k_sw = pltpu.roll(k, HEAD_DIM, 1)
    v_sw = pltpu.roll(v, HEAD_DIM, 1)
    ones_last = jnp.where(lane == LANES - 1, 1.0, 0.0)
    ones_first = jnp.where(lane == 0, 1.0, 0.0)
    kk = jnp.stack([jnp.where(low, k, k_sw), jnp.where(low, k_sw, k)]).astype(BF)
    v_lo = jnp.stack([jnp.where(low, v, ones_last), jnp.where(low, v_sw, ones_last)]).astype(BF)
    v_hi = jnp.stack([jnp.where(low, ones_first, v_sw), jnp.where(low, ones_first, v)]).astype(BF)
    return kk, v_lo, v_hi


def _attend(q_ref, o_ref, key_sets):
    tq = q_ref.shape[0]
    lane = lax.broadcasted_iota(I32, (tq, LANES), 1)
    low = lane < HEAD_DIM
    for pair in range(N_HEADS // 2):
        g = (2 * pair) // GQA
        q2 = q_ref[:, pair * LANES:(pair + 1) * LANES]
        zero = jnp.zeros_like(q2)
        accs = []
        for half, qh in enumerate((jnp.where(low, q2, zero), jnp.where(low, zero, q2))):
            scores = [_dot_nt(qh, ks[0][g]) for ks in key_sets]
            m = scores[0].max(axis=-1, keepdims=True)
            for s in scores[1:]:
                m = jnp.maximum(m, s.max(axis=-1, keepdims=True))
            acc = jnp.zeros((tq, LANES), F32)
            for s, ks in zip(scores, key_sets):
                acc = acc + _dot(jnp.exp2(s - m).astype(BF), ks[1 + half][g])
            accs.append(acc)
        out = jnp.where(low, accs[0] * (1.0 / accs[0][:, LANES - 1:LANES]), accs[1] * (1.0 / accs[1][:, 0:1]))
        o_ref[:, pair * LANES:(pair + 1) * LANES] = out.astype(BF)


def _attn_ctx_kernel(q_ref, k_ref, v_ref, o_ref):
    _attend(q_ref, o_ref, [_head_pair_operands(k_ref[...], v_ref[...])])


def _attn_lat_kernel(q_ref, kc_ref, vc_ref, kl_ref, vl_ref, o_ref, *ops):
    @pl.when(pl.program_id(1) == 0)
    def _():
        for dst, src in zip(ops, _head_pair_operands(kc_ref[...], vc_ref[...])
                            + _head_pair_operands(kl_ref[...], vl_ref[...])):
            dst[...] = src

    _attend(q_ref, o_ref, [ops[0:3], ops[3:6]])


def _attention(q, k, v, cache_k4, cache_v4, layer, geom):
    seq = geom.seq
    o_ctx = pl.pallas_call(
        _attn_ctx_kernel,
        grid=(geom.batch,),
        in_specs=[pl.BlockSpec((seq, Q_DIM), lambda b: (b, 0)),
                  pl.BlockSpec((seq, KV_DIM), lambda b: (b, 0)),
                  pl.BlockSpec((seq, KV_DIM), lambda b: (b, 0))],
        out_specs=pl.BlockSpec((seq, Q_DIM), lambda b: (b, 0)),
        out_shape=jax.ShapeDtypeStruct((geom.n_ctx, Q_DIM), BF),
        compiler_params=_params(("parallel",)),
        name="attention_context",
    )(q, k, v)

    tq = QUERY_TILE
    dseq = geom.dec_seq
    past = cache_k4.shape[2]
    q_tiles = dseq // tq
    ctx_tiles = geom.n_ctx // tq
    ctx_seqs = geom.n_ctx // dseq
    o_lat = pl.pallas_call(
        _attn_lat_kernel,
        grid=(geom.dec_batch, q_tiles),
        in_specs=[pl.BlockSpec((tq, Q_DIM), lambda b, i: (ctx_tiles + b * q_tiles + i, 0)),
                  pl.BlockSpec((None, None, past, KV_DIM), lambda b, i: (b, layer, 0, 0)),
                  pl.BlockSpec((None, None, past, KV_DIM), lambda b, i: (b, layer, 0, 0)),
                  pl.BlockSpec((dseq, KV_DIM), lambda b, i: (ctx_seqs + b, 0)),
                  pl.BlockSpec((dseq, KV_DIM), lambda b, i: (ctx_seqs + b, 0))],
        out_specs=pl.BlockSpec((tq, Q_DIM), lambda b, i: (b * q_tiles + i, 0)),
        out_shape=jax.ShapeDtypeStruct((geom.n_lat, Q_DIM), BF),
        scratch_shapes=[pltpu.VMEM((N_KV_HEADS, s, KV_DIM), BF) for s in (past, past, past, dseq, dseq, dseq)],
        compiler_params=_params(("parallel", "arbitrary")),
        name="attention_latent",
    )(q, cache_k4, cache_v4, k, v)
    return o_ctx, o_lat


def _ssd_kernel(*refs, n_chunks, has_h0, emit_state):
    it = iter(refs)
    xbc_ref, z_ref, dt_ref, dtt_ref = next(it), next(it), next(it), next(it)
    h0_ref = next(it) if has_h0 else None
    (cw_ref, cb_ref, alog_r_ref, alog_c_ref, dtb_r_ref, dtb_c_ref,
     dskip_ref, gn_ref) = (next(it) for _ in range(8))
    y_out = next(it)
    hfin_out = next(it) if emit_state else None
    pad_ref, u_ref, csc_ref, csr_ref, dtr_ref, tot_ref, st_ref, hcur_ref, y_ref = (next(it) for _ in range(9))

    seq = xbc_ref.shape[0]
    nh = SSM_HEADS
    q = CHUNK

    pad_ref[0:HALO, :] = jnp.zeros((HALO, CONV_DIM), F32)
    pad_ref[HALO:HALO + seq, :] = xbc_ref[...]
    pad_ref[HALO + seq:2 * HALO + seq, :] = jnp.zeros((HALO, CONV_DIM), F32)

    a_row = -jnp.exp(alog_r_ref[...])
    a_col = -jnp.exp(alog_c_ref[...])
    ii = lax.broadcasted_iota(I32, (q, q), 0)
    jj = lax.broadcasted_iota(I32, (q, q), 1)
    lower = jj <= ii
    upper = jj >= ii
    tri_lower = lower.astype(F32)
    tri_upper = upper.astype(F32)
    fwd_lane = lax.broadcasted_iota(I32, (q, 2 * nh), 1) < nh
    fwd_row = lax.broadcasted_iota(I32, (2 * nh, q), 0) < nh

    def chunk_stats(c, carry):
        base = pl.multiple_of(c * q, q)
        win = pad_ref[pl.ds(base, q + 2 * HALO), :]
        acc = jnp.zeros((q, CONV_DIM), F32) + cb_ref[...]
        for tap in range(CONV_W):
            off = HALO - CONV_PAD + tap
            acc = acc + win[off:off + q, :] * cw_ref[tap:tap + 1, :]
        u = _silu(acc)
        u_ref[pl.ds(base, q), :] = u

        dtc = _softplus(dt_ref[pl.ds(base, q), :] + dtb_r_ref[...])
        dtr = _softplus(dtt_ref[:, pl.ds(base, q)] + dtb_c_ref[...])
        ac = dtc * a_row
        ar = dtr * a_col
        cs_col = jnp.where(fwd_lane, _dot_exact(tri_lower, ac), _dot_exact(tri_upper, ac))
        cs_row = jnp.where(fwd_row, _dot_exact(ar, tri_upper), _dot_exact(ar, tri_lower))
        csc_ref[pl.ds(base, q), :] = cs_col
        csr_ref[:, pl.ds(base, q)] = cs_row
        dtr_ref[:, pl.ds(base, q)] = dtr
        total = jnp.where(fwd_lane[0:1, :], cs_col[q - 1:q, :], cs_col[0:1, :])
        tot_ref[c] = total
        total_c = jnp.where(fwd_row[:, 0:1], cs_row[:, q - 1:q], cs_row[:, 0:1])
        w_end = jnp.exp(total_c - cs_row) * dtr

        xs_t = u[:, 0:SSM_WIDTH].T
        bmat = u[:, SSM_WIDTH:SSM_WIDTH + BC_DIM].astype(BF)
        b_grp = [bmat[:, grp * SSM_STATE:(grp + 1) * SSM_STATE] for grp in range(SSM_GROUPS)]
        for dh in range(2 * nh):
            head = dh % nh
            xw = (xs_t[head * SSM_HEAD_DIM:(head + 1) * SSM_HEAD_DIM, :] * w_end[dh:dh + 1, :]).astype(BF)
            st_ref[c, dh] = _dot(xw, b_grp[head // (nh // SSM_GROUPS)])
        return carry

    lax.fori_loop(0, n_chunks, chunk_stats, 0)

    for dh in range(2 * nh):
        hcur_ref[dh] = h0_ref[dh // nh, dh % nh] if has_h0 else jnp.zeros((SSM_HEAD_DIM, SSM_STATE), F32)

    def recur(step, carry):
        for direction in range(2):
            c = step if direction == 0 else n_chunks - 1 - step
            decay = jnp.exp(tot_ref[c])
            for head in range(nh):
                dh = direction * nh + head
                new = st_ref[c, dh]
                cur = hcur_ref[dh]
                st_ref[c, dh] = cur
                hcur_ref[dh] = cur * decay[:, dh:dh + 1] + new
        return carry

    lax.fori_loop(0, n_chunks, recur, 0)
    if emit_state:
        for dh in range(2 * nh):
            hfin_out[dh // nh, dh % nh] = hcur_ref[dh]

    def chunk_out(c, carry):
        base = pl.multiple_of(c * q, q)
        u = u_ref[pl.ds(base, q), :]
        xs = u[:, 0:SSM_WIDTH]
        xs_b = xs.astype(BF)
        bmat = u[:, SSM_WIDTH:SSM_WIDTH + BC_DIM].astype(BF)
        cmat = u[:, SSM_WIDTH + BC_DIM:CONV_DIM]
        cmat_b = cmat.astype(BF)
        cs_col = csc_ref[pl.ds(base, q), :]
        cs_row = csr_ref[:, pl.ds(base, q)]
        dtr = dtr_ref[:, pl.ds(base, q)]
        e_col = jnp.exp(cs_col)
        for grp in range(SSM_GROUPS):
            gs = slice(grp * SSM_STATE, (grp + 1) * SSM_STATE)
            cb = _dot_nt(cmat_b[:, gs], bmat[:, gs])
            for head in range(grp * (nh // SSM_GROUPS), (grp + 1) * (nh // SSM_GROUPS)):
                hb = nh + head
                seg_f = jnp.where(lower, cs_col[:, head:head + 1] - cs_row[head:head + 1, :], -1e30)
                seg_b = jnp.where(upper, cs_col[:, hb:hb + 1] - cs_row[hb:hb + 1, :], -1e30)
                mix = cb * (jnp.exp(seg_f) * dtr[head:head + 1, :] + jnp.exp(seg_b) * dtr[hb:hb + 1, :])
                hs = slice(head * SSM_HEAD_DIM, (head + 1) * SSM_HEAD_DIM)
                y = _dot(mix.astype(BF), xs_b[:, hs])
                y = y + _dot_nt((cmat[:, gs] * e_col[:, head:head + 1]).astype(BF), st_ref[c, head].astype(BF))
                y = y + _dot_nt((cmat[:, gs] * e_col[:, hb:hb + 1]).astype(BF), st_ref[c, hb].astype(BF))
                y_ref[:, hs] = y
        y = y_ref[...] + xs * dskip_ref[...]
        y = y * _silu(z_ref[pl.ds(base, q), :])
        ms = jnp.mean(y * y, axis=-1, keepdims=True)
        y_out[pl.ds(base, q), :] = (y * lax.rsqrt(ms + EPS) * gn_ref[...]).astype(BF)
        return carry

    lax.fori_loop(0, n_chunks, chunk_out, 0)


def _ssd_call(xbc, z, dt, dtt, h0, layer, lw, *, seq, n_seq, row_block0, emit_state, name):
    n_chunks = seq // CHUNK
    nh2 = 2 * SSM_HEADS
    full = lambda a: pl.BlockSpec(a.shape, lambda b: (0,) * a.ndim)
    params = [lw["conv_w"], lw["conv_b"], lw["alog_r"], lw["alog_c"], lw["dtb_r"], lw["dtb_c"],
              lw["dskip"], lw["gn"]]
    in_specs = [pl.BlockSpec((seq, CONV_DIM), lambda b: (row_block0 + b, 0)),
                pl.BlockSpec((seq, SSM_WIDTH), lambda b: (row_block0 + b, 0)),
                pl.BlockSpec((seq, nh2), lambda b: (row_block0 + b, 0)),
                pl.BlockSpec((nh2, seq), lambda b: (0, row_block0 + b))]
    args = [xbc, z, dt, dtt]
    if h0 is not None:
        in_specs.append(pl.BlockSpec((None, None, 2, SSM_HEADS, SSM_HEAD_DIM, SSM_STATE),
                                     lambda b: (b, layer, 0, 0, 0, 0)))
        args.append(h0)
    in_specs += [full(p) for p in params]
    args += params
    out_specs = [pl.BlockSpec((seq, SSM_WIDTH), lambda b: (b, 0))]
    out_shape = [jax.ShapeDtypeStruct((n_seq * seq, SSM_WIDTH), BF)]
    if emit_state:
        out_specs.append(pl.BlockSpec((None, 2, SSM_HEADS, SSM_HEAD_DIM, SSM_STATE),
                                      lambda b: (b, 0, 0, 0, 0)))
        out_shape.append(jax.ShapeDtypeStruct((n_seq, 2, SSM_HEADS, SSM_HEAD_DIM, SSM_STATE), F32))
    scratch = [
        pltpu.VMEM((seq + 2 * HALO, CONV_DIM), F32),
        pltpu.VMEM((seq, CONV_DIM), F32),
        pltpu.VMEM((seq, nh2), F32),
        pltpu.VMEM((nh2, seq), F32),
        pltpu.VMEM((nh2, seq), F32),
        pltpu.VMEM((n_chunks, 1, nh2), F32),
        pltpu.VMEM((n_chunks, nh2, SSM_HEAD_DIM, SSM_STATE), F32),
        pltpu.VMEM((nh2, SSM_HEAD_DIM, SSM_STATE), F32),
        pltpu.VMEM((CHUNK, SSM_WIDTH), F32),
    ]
    return pl.pallas_call(
        functools.partial(_ssd_kernel, n_chunks=n_chunks, has_h0=h0 is not None, emit_state=emit_state),
        grid=(n_seq,),
        in_specs=in_specs,
        out_specs=out_specs,
        out_shape=out_shape,
        scratch_shapes=scratch,
        compiler_params=_params(("parallel",)),
        name=name,
    )(*args)


def _ssd_mixer(xbc, z, dt, dtt, state_ssm, layer, lw, geom):
    y_ctx, h_fin = _ssd_call(xbc, z, dt, dtt, None, layer, lw, seq=geom.seq, n_seq=geom.batch,
                             row_block0=0, emit_state=True, name="ssd_context")
    (y_lat,) = _ssd_call(xbc, z, dt, dtt, state_ssm, layer, lw, seq=geom.dec_seq,
                         n_seq=geom.dec_batch, row_block0=geom.n_ctx // geom.dec_seq,
                         emit_state=False, name="ssd_latent")
    return y_ctx, y_lat, h_fin


def _outproj_kernel(ac_ref, al_ref, sc_ref, sl_ref, x_ref, mod_ref, n2_ref, woa_ref, wos_ref, wrh_ref, wrl_ref,
                    x1_out, h2_out, afft_out, *, ctx_tiles):
    d = x_ref.shape[-1]
    mod = mod_ref[...]
    g1 = mod[:, 2 * d:3 * d]
    sh2 = mod[:, 3 * d:4 * d]
    sc2 = mod[:, 4 * d:5 * d]
    is_ctx = pl.program_id(0) < ctx_tiles
    attn = jnp.where(is_ctx, ac_ref[...], al_ref[...])
    ssm = jnp.where(is_ctx, sc_ref[...], sl_ref[...])
    y = _dot(attn, woa_ref[...]) + _dot(ssm, wos_ref[...])
    x1 = x_ref[...] + g1 * y
    x1_out[...] = x1
    ms = jnp.mean(x1 * x1, axis=-1, keepdims=True)
    h2 = x1 * lax.rsqrt(ms + EPS) * n2_ref[...]
    h2 = h2 * (1.0 + sc2) + sh2
    h2_out[...] = h2
    hh = h2.astype(BF)
    hl = (h2 - hh.astype(F32)).astype(BF)
    wh = wrh_ref[...]
    logits = _dot_nt(wh, hh) + _dot_nt(wh, hl) + _dot_nt(wrl_ref[...], hh)
    logits = logits - logits.max(axis=0, keepdims=True)
    e = jnp.exp(logits)
    afft_out[...] = e / e.sum(axis=0, keepdims=True)


def _out_projection(attn_c, attn_l, ssm_c, ssm_l, x, mod3, lw, geom):
    t, d = x.shape
    tm = TOKEN_TILE
    nc = geom.ctx_tiles
    full = lambda a: pl.BlockSpec(a.shape, lambda i: (0,) * a.ndim)
    row = lambda w: pl.BlockSpec((tm, w), lambda i: (i, 0))
    ctx_row = lambda w: pl.BlockSpec((tm, w), lambda i: (jnp.minimum(i, nc - 1), 0))
    lat_row = lambda w: pl.BlockSpec((tm, w), lambda i: (jnp.maximum(i - nc, 0), 0))
    weights = [lw["n2"], lw["woa"], lw["wos"], lw["wrh"], lw["wrl"]]
    return pl.pallas_call(
        functools.partial(_outproj_kernel, ctx_tiles=nc),
        grid=(t // tm,),
        in_specs=[ctx_row(Q_DIM), lat_row(Q_DIM), ctx_row(SSM_WIDTH), lat_row(SSM_WIDTH), row(d),
                  pl.BlockSpec((None, 1, mod3.shape[-1]), lambda i: (geom.mod_row(i), 0, 0))]
                 + [full(w) for w in weights],
        out_specs=[row(d), row(d), pl.BlockSpec((N_EXPERTS, tm), lambda i: (0, i))],
        out_shape=[jax.ShapeDtypeStruct((t, d), F32), jax.ShapeDtypeStruct((t, d), F32),
                   jax.ShapeDtypeStruct((N_EXPERTS, t), F32)],
        compiler_params=_params(("parallel",)),
        name="out_projection",
    )(attn_c, attn_l, ssm_c, ssm_l, x, mod3, *weights)


def _flat_cumsum(m2, rows):
    gr = m2.shape[0]
    li = lax.broadcasted_iota(I32, (LANES, LANES), 0)
    lj = lax.broadcasted_iota(I32, (LANES, LANES), 1)
    mb = m2.astype(BF)
    within = _dot(mb, (li <= lj).astype(BF))
    row_tot = _dot(mb, jnp.ones((LANES, LANES), BF))
    ri = lax.broadcasted_iota(I32, (gr, gr), 0)
    rj = lax.broadcasted_iota(I32, (gr, gr), 1)
    before = ((ri // rows == rj // rows) & (rj < ri)).astype(BF)
    hi, lo = _split_base128(row_tot)
    return within + (LANES * _dot(before, hi) + _dot(before, lo))


def _split_base128(v):
    hi = jnp.floor(v * (1.0 / LANES))
    return hi.astype(BF), (v - hi * LANES).astype(BF)


def _plan_kernel(aff_ref, idx_out, gate_out, dslot_out, start_out, cnt_out, *, cap):
    a = aff_ref[...]
    n_e, rows, _ = a.shape

    def count(hit):
        return hit.astype(I32).sum(axis=2, keepdims=True).sum(axis=1, keepdims=True)

    def search(step, lo):
        cand = lo | (1 << (30 - step))
        return jnp.where(count(a >= lax.bitcast_convert_type(cand, F32)) >= cap, cand, lo)

    thr_bits = lax.fori_loop(0, 31, search, jnp.zeros((n_e, 1, 1), I32))
    thr = lax.bitcast_convert_type(thr_bits, F32)
    gt = a > thr
    eq = a == thr
    need = cap - count(gt)
    eqf = eq.astype(F32).reshape(n_e * rows, LANES)
    rank = (_flat_cumsum(eqf, rows) - eqf).reshape(n_e, rows, LANES)
    mask = gt | (eq & (rank < need.astype(F32)))
    mf = mask.astype(F32)

    c3 = _flat_cumsum(mf.reshape(n_e * rows, LANES), rows).reshape(n_e, rows, LANES)
    ahead = []
    run = jnp.zeros((rows, LANES), F32)
    for e in range(n_e):
        ahead.append(run)
        run = run + mf[e]
    cnt = run
    start = _flat_cumsum(cnt, rows) - cnt
    start_out[...] = start.astype(I32)
    cnt_out[...] = cnt.astype(I32)

    s_lane = lax.broadcasted_iota(I32, (1, cap), 1).astype(F32)
    r_col = lax.broadcasted_iota(I32, (LANES, cap), 0).astype(F32)
    pad = jnp.zeros((LANES - rows, LANES), F32)

    def lanes_first(xm):
        return (jnp.concatenate([xm, pad], axis=0) if rows < LANES else xm).T

    for e in range(n_e):
        c_e = c3[e]
        row_of = (c_e[:, LANES - 1:LANES] <= s_lane).astype(F32).sum(axis=0, keepdims=True)
        onehot = (r_col == row_of).astype(BF)
        a_t = lanes_first(a[e])
        a1 = a_t.astype(BF)
        r1 = a_t - a1.astype(F32)
        a2 = r1.astype(BF)
        a3 = (r1 - a2.astype(F32)).astype(BF)
        pieces = _split_base128(lanes_first(c_e)) + (a1, a2, a3) + _split_base128(lanes_first(start + ahead[e]))
        sel = _dot(jnp.concatenate(pieces, axis=0), onehot)
        part = [sel[j * LANES:(j + 1) * LANES] for j in range(len(pieces))]
        sel_c = LANES * part[0] + part[1]
        sel_a = part[2] + (part[3] + part[4])
        sel_d = LANES * part[5] + part[6]
        lane_of = (sel_c <= s_lane).astype(F32).sum(axis=0, keepdims=True)
        pick = r_col == lane_of
        idx_out[e:e + 1, :] = (row_of * LANES + lane_of).astype(I32)
        gate_out[e:e + 1, :] = jnp.where(pick, sel_a, 0.0).sum(axis=0, keepdims=True)
        dslot_out[e:e + 1, :] = jnp.where(pick, sel_d, 0.0).sum(axis=0, keepdims=True).astype(I32)


def _plan(aff_t, n_sets, cap):
    n_e, t = aff_t.shape
    n = t // n_sets
    rows = n // LANES
    assert rows <= LANES
    aff4 = aff_t.reshape(n_e, n_sets, rows, LANES)
    slot_spec = pl.BlockSpec((None, n_e, cap), lambda s: (s, 0, 0))
    tok_spec = pl.BlockSpec((None, rows, LANES), lambda s: (s, 0, 0))
    slot_shape = lambda dt: jax.ShapeDtypeStruct((n_sets, n_e, cap), dt)
    tok_shape = jax.ShapeDtypeStruct((n_sets, rows, LANES), I32)
    return pl.pallas_call(
        functools.partial(_plan_kernel, cap=cap),
        grid=(n_sets,),
        in_specs=[pl.BlockSpec((n_e, None, rows, LANES), lambda s: (0, s, 0, 0))],
        out_specs=[slot_spec, slot_spec, slot_spec, tok_spec, tok_spec],
        out_shape=[slot_shape(I32), slot_shape(F32), slot_shape(I32), tok_shape, tok_shape],
        compiler_params=_params(("parallel",)),
        name="expert_choice_plan",
    )(aff4)


DMA_UNROLL = 8
ROW_BLOCK = 512
COMBINE_ROWS = 256
COMBINE_BUFFERS = 4


def _expert_kernel(gidx_ref, dest_ref, h2p_hbm, gate_ref, wg_ref, wu_ref, wd_ref, z_hbm,
                   xw_a, xw_b, xb_ref, acc_a, acc_b, wgb_ref, wub_ref, wdb_ref, sem_g, sem_s, *, n_f, row_block):
    e = pl.program_id(0)
    f = pl.program_id(1)
    n_e = pl.num_programs(0)
    m_rows = xw_a.shape[0]
    n_blocks = m_rows // row_block
    per_block = m_rows // (n_f * n_blocks)

    def vmem_row(buf, m8, k):
        tile = pl.multiple_of(m8 + (k // SUBLANES) * SUBLANES, SUBLANES)
        return buf.at[pl.ds(tile, SUBLANES), :].at[pl.ds(k % SUBLANES, 1), :]

    def gather_copy(entry, m8, k, xw):
        return pltpu.make_async_copy(h2p_hbm.at[pl.ds(gidx_ref[entry], 1), :], vmem_row(xw, m8, k), sem_g)

    def scatter_copy(entry, m8, k, acc):
        return pltpu.make_async_copy(vmem_row(acc, m8, k), z_hbm.at[pl.ds(dest_ref[entry], 1), :], sem_s)

    def wait_gathers(xw):
        pltpu.make_async_copy(h2p_hbm.at[pl.ds(0, m_rows), :], xw, sem_g).wait()

    def wait_scatters(acc):
        pltpu.make_async_copy(acc, z_hbm.at[pl.ds(0, m_rows), :], sem_s).wait()

    def issue_all(make):
        def body(j, carry):
            for k in range(DMA_UNROLL):
                make(j * DMA_UNROLL, k).start()
            return carry
        lax.fori_loop(0, m_rows // DMA_UNROLL, body, 0)

    def run(xw_cur, xw_next, acc_cur, acc_prev):
        @pl.when((e == 0) & (f == 0))
        def _():
            acc_prev[...] = jnp.zeros_like(acc_prev)
            issue_all(lambda m8, k: gather_copy(m8 + k, m8, k, xw_cur))

        @pl.when(f == 0)
        def _():
            wait_gathers(xw_cur)
            xb_ref[...] = xw_cur[...].astype(BF)
            acc_cur[...] = jnp.zeros_like(acc_cur)

        wgb_ref[...] = wg_ref[...].astype(BF)
        wub_ref[...] = wu_ref[...].astype(BF)
        wdb_ref[...] = wd_ref[...].astype(BF)
        last = f == n_f - 1

        def block(rb, carry):
            r0 = pl.multiple_of(rb * row_block, row_block)
            m0 = (f * n_blocks + rb) * per_block

            for k in range(per_block):
                gather_copy((e + 1) * m_rows + m0 + k, m0, k, xw_next).start()
                scatter_copy(e * m_rows + m0 + k, m0, k, acc_prev).start()
            x = xb_ref[pl.ds(r0, row_block), :]
            g = _dot(x, wgb_ref[...])
            u = _dot(x, wub_ref[...])
            hid = (_silu(g) * u).astype(BF)
            total = acc_cur[pl.ds(r0, row_block), :] + _dot(hid, wdb_ref[...])
            gate = jnp.where(last, gate_ref[pl.ds(r0, row_block), :], 1.0)
            acc_cur[pl.ds(r0, row_block), :] = total * gate
            return carry

        lax.fori_loop(0, n_blocks, block, 0)

        @pl.when(last)
        def _():
            wait_scatters(acc_prev)

        @pl.when(last & (e == n_e - 1))
        def _():
            wait_gathers(xw_next)
            issue_all(lambda m8, k: scatter_copy((e + 1) * m_rows + m8 + k, m8, k, acc_cur))
            wait_scatters(acc_cur)

    @pl.when(e % 2 == 0)
    def _():
        run(xw_a, xw_b, acc_a, acc_b)

    @pl.when(e % 2 == 1)
    def _():
        run(xw_b, xw_a, acc_b, acc_a)


def _expert_ffn(gidx, dest, h2p, gates, w_gate, w_up, w_down, layer):
    n_e, m, _ = gates.shape
    d = h2p.shape[1]
    ff = w_gate.shape[-1]
    tf = 256
    n_f = ff // tf
    row_block = min(ROW_BLOCK, m)
    assert m % DMA_UNROLL == 0 and (n_e * m) % COMBINE_ROWS == 0 and m % (n_f * (m // row_block)) == 0
    gidx_ext = jnp.concatenate([gidx, jnp.arange(m, dtype=I32)])
    dest_ext = jnp.concatenate([n_e * m + jnp.arange(m, dtype=I32), dest])
    return pl.pallas_call(
        functools.partial(_expert_kernel, n_f=n_f, row_block=row_block),
        grid_spec=pltpu.PrefetchScalarGridSpec(
            num_scalar_prefetch=2,
            grid=(n_e, n_f),
            in_specs=[pl.BlockSpec(memory_space=pl.ANY),
                      pl.BlockSpec((None, m, 1), lambda e, f, gi, de: (e, 0, 0)),
                      pl.BlockSpec((None, None, d, tf), lambda e, f, gi, de: (layer, e, 0, f)),
                      pl.BlockSpec((None, None, d, tf), lambda e, f, gi, de: (layer, e, 0, f)),
                      pl.BlockSpec((None, None, tf, d), lambda e, f, gi, de: (layer, e, f, 0))],
            out_specs=pl.BlockSpec(memory_space=pl.ANY),
            scratch_shapes=[pltpu.VMEM((m, d), F32), pltpu.VMEM((m, d), F32),
                            pltpu.VMEM((m, d), BF), pltpu.VMEM((m, d), F32), pltpu.VMEM((m, d), F32),
                            pltpu.VMEM((d, tf), BF), pltpu.VMEM((d, tf), BF), pltpu.VMEM((tf, d), BF),
                            pltpu.SemaphoreType.DMA, pltpu.SemaphoreType.DMA]),
        out_shape=jax.ShapeDtypeStruct(((n_e + 1) * m, d), F32),
        compiler_params=_params(("arbitrary", "arbitrary")),
        name="expert_swiglu",
    )(gidx_ext, dest_ext, h2p, gates, w_gate, w_up, w_down)


def _combine_kernel(lo_ref, hi_ref, z_hbm, start_ref, end_ref, x1_ref, mod_ref, o_ref, zbuf, acc_ref, sem,
                    state, *, n_chunks):
    i = pl.program_id(0)
    d = x1_ref.shape[-1]
    n_buf, ch, _ = zbuf.shape
    lo = lo_ref[i]
    hi = hi_ref[i]

    def chunk_copy(kc):
        slot = kc % n_buf
        return pltpu.make_async_copy(z_hbm.at[pl.ds(pl.multiple_of(kc * ch, ch), ch), :], zbuf.at[slot],
                                     sem.at[slot])

    @pl.when(i == 0)
    def _():
        state[0] = 0
        state[1] = 0

    acc_ref[...] = jnp.zeros_like(acc_ref)
    start = start_ref[...]
    end = end_ref[...]

    def body(kc, carry):
        upto = jnp.minimum(kc + n_buf, n_chunks)

        def start_one(j, c):
            chunk_copy(j).start()
            return c

        lax.fori_loop(state[0], upto, start_one, 0)
        state[0] = jnp.maximum(state[0], upto)

        @pl.when(kc >= state[1])
        def _():
            chunk_copy(kc).wait()
            state[1] = kc + 1

        slot = kc % n_buf
        k = kc * ch + lax.broadcasted_iota(I32, (1, ch), 1)
        own = jnp.where((k >= start) & (k < end), 1.0, 0.0).astype(BF)
        zc = zbuf[slot]
        zh = zc.astype(BF)
        zl = (zc - zh.astype(F32)).astype(BF)
        acc_ref[...] += _dot(own, zh) + _dot(own, zl)
        return carry

    lax.fori_loop(lo, hi, body, 0)
    g2 = mod_ref[...][:, 5 * d:6 * d]
    o_ref[...] = x1_ref[...] + g2 * acc_ref[...]


def _combine(z, n_rows, start, end, tile_lo, tile_hi, x1, mod3, geom):
    t, d = x1.shape
    tm = TOKEN_TILE
    return pl.pallas_call(
        functools.partial(_combine_kernel, n_chunks=n_rows // COMBINE_ROWS),
        grid_spec=pltpu.PrefetchScalarGridSpec(
            num_scalar_prefetch=2,
            grid=(t // tm,),
            in_specs=[pl.BlockSpec(memory_space=pl.ANY),
                      pl.BlockSpec((tm, 1), lambda i, lo, hi: (i, 0)),
                      pl.BlockSpec((tm, 1), lambda i, lo, hi: (i, 0)),
                      pl.BlockSpec((tm, d), lambda i, lo, hi: (i, 0)),
                      pl.BlockSpec((None, 1, mod3.shape[-1]), lambda i, lo, hi: (geom.mod_row(i), 0, 0))],
            out_specs=pl.BlockSpec((tm, d), lambda i, lo, hi: (i, 0)),
            scratch_shapes=[pltpu.VMEM((COMBINE_BUFFERS, COMBINE_ROWS, d), F32), pltpu.VMEM((tm, d), F32),
                            pltpu.SemaphoreType.DMA((COMBINE_BUFFERS,)), pltpu.SMEM((2,), I32)]),
        out_shape=jax.ShapeDtypeStruct((t, d), F32),
        compiler_params=_params(("arbitrary",)),
        name="gated_combine",
    )(tile_lo, tile_hi, z, start, end, x1, mod3)


class _Geometry:
    def __init__(self, batch, seq, dec_batch, dec_seq):
        self.batch, self.seq, self.dec_batch, self.dec_seq = batch, seq, dec_batch, dec_seq
        self.n_ctx = batch * seq
        self.n_lat = dec_batch * dec_seq
        tm = TOKEN_TILE
        assert self.n_ctx % tm == 0 and dec_seq % tm == 0 and self.n_ctx % dec_seq == 0
        assert seq % CHUNK == 0 and dec_seq % CHUNK == 0 and dec_seq % QUERY_TILE == 0
        self.ctx_tiles = self.n_ctx // tm
        self.lat_tiles_per_seq = dec_seq // tm

    def mod_row(self, i):
        return jnp.where(i < self.ctx_tiles, 0, 1 + (i - self.ctx_tiles) // self.lat_tiles_per_seq)

    def rope_block(self, i):
        return jnp.where(i < self.ctx_tiles, 0, 1 + (i - self.ctx_tiles) % self.lat_tiles_per_seq)


def _rope_tables(dec_seq):
    t = jnp.arange(dec_seq)
    row = (t // GRID_W).astype(F32)
    col = (t % GRID_W).astype(F32)
    inv = ROPE_THETA ** (-jnp.arange(ROT_PAIRS, dtype=F32) / ROT_PAIRS)
    ang = jnp.concatenate([row[:, None] * inv, col[:, None] * inv], axis=-1)
    cos = jnp.tile(jnp.concatenate([jnp.cos(ang), jnp.cos(ang)], axis=-1), (1, N_HEADS))
    sin = jnp.tile(jnp.concatenate([-jnp.sin(ang), jnp.sin(ang)], axis=-1), (1, N_HEADS))
    ident = jnp.ones((TOKEN_TILE, Q_DIM), F32)
    return (jnp.concatenate([ident, cos], axis=0),
            jnp.concatenate([jnp.zeros_like(ident), sin], axis=0))


def _block_diag_ones(width):
    i = jnp.arange(width)
    return (i[:, None] // HEAD_DIM == i[None, :] // HEAD_DIM).astype(BF)


def _layer_weights(l, norm1, norm2, w_in, conv_w, conv_b, q_norm, k_norm, a_log, dt_bias, d_skip,
                   ssm_norm, w_out, w_router):
    w = w_in[l].astype(BF)
    o = 0
    parts = {}
    for name, width in (("wq", Q_DIM), ("wk", KV_DIM), ("wv", KV_DIM), ("wx", CONV_DIM),
                        ("wz", SSM_WIDTH), ("wdt", 2 * SSM_HEADS)):
        parts[name] = w[:, o:o + width]
        o += width
    wo = w_out[l].astype(BF)
    wr = w_router[l].T
    wrh = wr.astype(BF)
    nh2 = 2 * SSM_HEADS
    parts.update(
        wdtt=parts["wdt"].T,
        n1=norm1[l][None, :], n2=norm2[l][None, :],
        gq=jnp.tile(q_norm[l], N_HEADS)[None, :], gk=jnp.tile(k_norm[l], N_KV_HEADS)[None, :],
        conv_w=conv_w[l], conv_b=conv_b[l][None, :],
        alog_r=a_log[l].reshape(1, nh2), alog_c=a_log[l].reshape(nh2, 1),
        dtb_r=dt_bias[l].reshape(1, nh2), dtb_c=dt_bias[l].reshape(nh2, 1),
        dskip=jnp.repeat(d_skip[l], SSM_HEAD_DIM)[None, :], gn=ssm_norm[l][None, :],
        woa=wo[:Q_DIM], wos=wo[Q_DIM:],
        wrh=wrh, wrl=(wr - wrh.astype(F32)).astype(BF),
    )
    return parts


def _moe(h2p, aff_t, x1, mod3, w_gate, w_up, w_down, layer, n_sets, geom):
    t = x1.shape[0]
    n = t // n_sets
    cap = EC_FACTOR * n // N_EXPERTS
    pairs = N_EXPERTS * cap
    idx, gates, dslot, start, cnt = _plan(aff_t, n_sets, cap)
    set_ix = jnp.arange(n_sets, dtype=I32)[:, None, None]
    expert_major = lambda a: a.transpose(1, 0, 2).reshape(N_EXPERTS, n_sets * cap)
    gidx = expert_major(idx + set_ix * n).reshape(-1)
    dest = expert_major(dslot + set_ix * pairs).reshape(-1)
    z = _expert_ffn(gidx, dest, h2p, expert_major(gates)[..., None], w_gate, w_up, w_down, layer)
    start_g = (start.reshape(n_sets, n) + set_ix[:, :, 0] * pairs).reshape(t, 1)
    end_g = start_g + cnt.reshape(t, 1)
    tile_lo = start_g[::TOKEN_TILE, 0] // COMBINE_ROWS
    tile_hi = (end_g[TOKEN_TILE - 1::TOKEN_TILE, 0] + COMBINE_ROWS - 1) // COMBINE_ROWS
    return _combine(z, n_sets * pairs, start_g, end_g, tile_lo, tile_hi, x1, mod3, geom)


def kernel(x_prompt, x_sample, cache_k, cache_v, state_ssm, c, c_ctx, norm1, norm2, ada_w, ada_b,
           w_in, conv_w, conv_b, q_norm, k_norm, A_log, dt_bias, d_skip, ssm_norm, w_out,
           w_router, w_gate, w_up, w_down):
    batch, seq, d = x_prompt.shape
    dec_batch, dec_seq, _ = x_sample.shape
    depth = w_in.shape[0]
    past = cache_k.shape[2]
    geom = _Geometry(batch, seq, dec_batch, dec_seq)
    assert geom.n_ctx == geom.n_lat, "routed sets are stacked as two equal halves"

    rows = -(-(1 + dec_batch) // SUBLANES) * SUBLANES
    cv = jnp.zeros((rows, d), F32).at[0].set(c_ctx).at[1:1 + dec_batch].set(c)
    mod = _ada_modulation(cv, ada_w, ada_b)
    cos, sin = _rope_tables(dec_seq)
    consts = dict(cos=cos, sin=sin, bdq=_block_diag_ones(Q_DIM), bdk=_block_diag_ones(KV_DIM))
    cache_k4 = cache_k.reshape(dec_batch, depth, past, KV_DIM)
    cache_v4 = cache_v.reshape(dec_batch, depth, past, KV_DIM)

    x = jnp.concatenate([x_prompt.reshape(geom.n_ctx, d), x_sample.reshape(geom.n_lat, d)], axis=0)
    ks_new, vs_new, hs_new = [], [], []
    for l in range(depth):
        lw = _layer_weights(l, norm1, norm2, w_in, conv_w, conv_b, q_norm, k_norm, A_log, dt_bias,
                            d_skip, ssm_norm, w_out, w_router)
        mod3 = mod[l][:, None, :]
        q, k, v, xbc, z, dt, dtt = _in_projection(x, mod3, lw, consts, geom)
        attn_c, attn_l = _attention(q, k, v, cache_k4, cache_v4, l, geom)
        ssm_c, ssm_l, h_fin = _ssd_mixer(xbc, z, dt, dtt, state_ssm, l, lw, geom)
        x1, h2p, aff_t = _out_projection(attn_c, attn_l, ssm_c, ssm_l, x, mod3, lw, geom)
        x = _moe(h2p, aff_t, x1, mod3, w_gate, w_up, w_down, l, 2, geom)
        ks_new.append(k[:geom.n_ctx].reshape(batch, seq, N_KV_HEADS, HEAD_DIM))
        vs_new.append(v[:geom.n_ctx].reshape(batch, seq, N_KV_HEADS, HEAD_DIM))
        hs_new.append(h_fin)
    y_prompt = x[:geom.n_ctx].reshape(batch, seq, d)
    y_sample = x[geom.n_ctx:].reshape(dec_batch, dec_seq, d)
    return (y_prompt, y_sample, jnp.stack(ks_new, axis=1), jnp.stack(vs_new, axis=1),
            jnp.stack(hs_new, axis=1))
```

```python
import functools

import jax
import jax.numpy as jnp
from jax import lax
from jax.experimental import pallas as pl
from jax.experimental.pallas import tpu as pltpu

F32 = jnp.float32
BF = jnp.bfloat16
I32 = jnp.int32

GRID_W = 64
N_HEADS = 8
N_KV_HEADS = 2
HEAD_DIM = 64
GQA = N_HEADS // N_KV_HEADS
ROPE_THETA = 10000.0
ROT_PAIRS = HEAD_DIM // 4
Q_DIM = N_HEADS * HEAD_DIM
KV_DIM = N_KV_HEADS * HEAD_DIM
SSM_WIDTH = 512
SSM_HEADS = 8
SSM_HEAD_DIM = 64
SSM_GROUPS = 2
SSM_STATE = 64
CONV_W = 5
CONV_PAD = CONV_W // 2
CHUNK = 128
BC_DIM = SSM_GROUPS * SSM_STATE
CONV_DIM = SSM_WIDTH + 2 * BC_DIM
N_EXPERTS = 16
EC_FACTOR = 2
EPS = 1e-6
SCALE = HEAD_DIM ** -0.5
LOG2_E = 1.4426950408889634

LANES = 128
SUBLANES = 8
TOKEN_TILE = 512
QUERY_TILE = 256
HALO = SUBLANES
VMEM_LIMIT = 56 * 1024 * 1024

NT_DIMS = (((1,), (1,)), ((), ()))


def _dot(a, b):
    return jnp.dot(a, b, preferred_element_type=F32)


def _dot_nt(a, b):
    return lax.dot_general(a, b, NT_DIMS, preferred_element_type=F32)


def _dot_exact(a, b):
    return jnp.dot(a, b, preferred_element_type=F32, precision=lax.Precision.HIGHEST)


def _silu(x):
    return x / (1.0 + jnp.exp(-x))


def _softplus(x):
    return jnp.maximum(x, 0.0) + jnp.log(1.0 + jnp.exp(-jnp.abs(x)))


def _params(sem):
    return pltpu.CompilerParams(dimension_semantics=sem, vmem_limit_bytes=VMEM_LIMIT)


def _ada_kernel(cv_ref, w_ref, b_ref, o_ref):
    s = _silu(cv_ref[...]).astype(BF)
    o_ref[...] = _dot(s, w_ref[...].astype(BF)) + b_ref[...]


def _ada_modulation(cv, ada_w, ada_b):
    depth, d, n6 = ada_w.shape
    rows = cv.shape[0]
    tn = n6 // 4
    return pl.pallas_call(
        _ada_kernel,
        grid=(depth, n6 // tn),
        in_specs=[
            pl.BlockSpec((rows, d), lambda l, j: (0, 0)),
            pl.BlockSpec((None, d, tn), lambda l, j: (l, 0, j)),
            pl.BlockSpec((None, 1, tn), lambda l, j: (l, 0, j)),
        ],
        out_specs=pl.BlockSpec((None, rows, tn), lambda l, j: (l, 0, j)),
        out_shape=jax.ShapeDtypeStruct((depth, rows, n6), F32),
        compiler_params=_params(("parallel", "parallel")),
        name="ada_modulation",
    )(cv, ada_w, ada_b.reshape(depth, 1, n6))


def _inproj_kernel(xc_ref, xl_ref, mod_ref, n1_ref, wq_ref, wk_ref, wv_ref, wx_ref, wz_ref, wdt_ref, wdtt_ref,
                   bdq_ref, bdk_ref, gq_ref, gk_ref, cos_ref, sin_ref,
                   q_out, k_out, v_out, xbc_out, z_out, dt_out, dtt_out, *, ctx_tiles):
    d = xc_ref.shape[-1]
    x = jnp.where(pl.program_id(0) < ctx_tiles, xc_ref[...], xl_ref[...])
    mod = mod_ref[...]
    sh1 = mod[:, 0:d]
    sc1 = mod[:, d:2 * d]
    ms = jnp.mean(x * x, axis=-1, keepdims=True)
    h = x * lax.rsqrt(ms + EPS) * n1_ref[...]
    hb = (h * (1.0 + sc1) + sh1).astype(BF)
    cos = cos_ref[...]
    sin = sin_ref[...]

    def norm_rope(t, bd_ref, g_ref, cos_t, sin_t):
        ss = _dot((t * t).astype(BF), bd_ref[...])
        tn = t * lax.rsqrt(ss * (1.0 / HEAD_DIM) + EPS) * g_ref[...]
        width = t.shape[-1]
        lane = lax.broadcasted_iota(I32, tn.shape, 1)
        first_half = (lane % HEAD_DIM) < (HEAD_DIM // 2)
        partner = jnp.where(first_half,
                            pltpu.roll(tn, width - HEAD_DIM // 2, 1),
                            pltpu.roll(tn, HEAD_DIM // 2, 1))
        return tn * cos_t + partner * sin_t

    q = norm_rope(_dot(hb, wq_ref[...]), bdq_ref, gq_ref, cos, sin)
    q_out[...] = (q * (SCALE * LOG2_E)).astype(BF)
    k_out[...] = norm_rope(_dot(hb, wk_ref[...]), bdk_ref, gk_ref, cos[:, :KV_DIM], sin[:, :KV_DIM])
    v_out[...] = _dot(hb, wv_ref[...])
    xbc_out[...] = _dot(hb, wx_ref[...])
    z_out[...] = _dot(hb, wz_ref[...])
    dt_out[...] = _dot(hb, wdt_ref[...])
    dtt_out[...] = _dot_nt(wdtt_ref[...], hb)


def _group_views(x, geom):
    tm = TOKEN_TILE
    nc = geom.ctx_tiles
    if isinstance(x, tuple):
        xc, xl = x
        lat_index = lambda i: (jnp.maximum(i - nc, 0), 0)
    else:
        xc = xl = x
        lat_index = lambda i: (jnp.maximum(i, nc), 0)
    d = xc.shape[-1]
    return xc, xl, pl.BlockSpec((tm, d), lambda i: (jnp.minimum(i, nc - 1), 0)), pl.BlockSpec((tm, d), lat_index)


def _in_projection(x, mod3, lw, consts, geom):
    xc, xl, xc_spec, xl_spec = _group_views(x, geom)
    t, d = geom.n_ctx + geom.n_lat, xc.shape[-1]
    tm = TOKEN_TILE
    full = lambda a: pl.BlockSpec(a.shape, lambda i: (0,) * a.ndim)
    row = lambda w: pl.BlockSpec((tm, w), lambda i: (i, 0))
    weights = [lw["wq"], lw["wk"], lw["wv"], lw["wx"], lw["wz"], lw["wdt"], lw["wdtt"],
               consts["bdq"], consts["bdk"], lw["gq"], lw["gk"]]
    out_shape = [
        jax.ShapeDtypeStruct((t, Q_DIM), BF),
        jax.ShapeDtypeStruct((t, KV_DIM), F32),
        jax.ShapeDtypeStruct((t, KV_DIM), F32),
        jax.ShapeDtypeStruct((t, CONV_DIM), F32),
        jax.ShapeDtypeStruct((t, SSM_WIDTH), F32),
        jax.ShapeDtypeStruct((t, 2 * SSM_HEADS), F32),
        jax.ShapeDtypeStruct((2 * SSM_HEADS, t), F32),
    ]
    return pl.pallas_call(
        functools.partial(_inproj_kernel, ctx_tiles=geom.ctx_tiles),
        grid=(t // tm,),
        in_specs=[xc_spec, xl_spec,
                  pl.BlockSpec((None, 1, mod3.shape[-1]), lambda i: (geom.mod_row(i), 0, 0)),
                  full(lw["n1"])]
                 + [full(w) for w in weights]
                 + [pl.BlockSpec((tm, Q_DIM), lambda i: (geom.rope_block(i), 0)),
                    pl.BlockSpec((tm, Q_DIM), lambda i: (geom.rope_block(i), 0))],
        out_specs=[row(Q_DIM), row(KV_DIM), row(KV_DIM), row(CONV_DIM), row(SSM_WIDTH),
                   row(2 * SSM_HEADS), pl.BlockSpec((2 * SSM_HEADS, tm), lambda i: (0, i))],
        out_shape=out_shape,
        compiler_params=_params(("parallel",)),
        name="in_projection",
    )(xc, xl, mod3, lw["n1"], *weights, consts["cos"], consts["sin"])


def _head_pair_operands(k, v):
    lane = lax.broadcasted_iota(I32, k.shape, 1)
    low = lane < HEAD_DIM
    k_sw = pltpu.roll(k, HEAD_DIM, 1)
    v_sw = pltpu.roll(v, HEAD_DIM, 1)
    ones_last = jnp.where(lane == LANES - 1, 1.0, 0.0)
    ones_first = jnp.where(lane == 0, 1.0, 0.0)
    kk = jnp.stack([jnp.where(low, k, k_sw), jnp.where(low, k_sw, k)]).astype(BF)
    v_lo = jnp.stack([jnp.where(low, v, ones_last), jnp.where(low, v_sw, ones_last)]).astype(BF)
    v_hi = jnp.stack([jnp.where(low, ones_first, v_sw), jnp.where(low, ones_first, v)]).astype(BF)
    return kk, v_lo, v_hi


def _attend(q_ref, o_ref, key_sets):
    tq = q_ref.shape[0]
    lane = lax.broadcasted_iota(I32, (tq, LANES), 1)
    low = lane < HEAD_DIM
    for pair in range(N_HEADS // 2):
        g = (2 * pair) // GQA
        q2 = q_ref[:, pair * LANES:(pair + 1) * LANES]
        zero = jnp.zeros_like(q2)
        accs = []
        for half, qh in enumerate((jnp.where(low, q2, zero), jnp.where(low, zero, q2))):
            scores = [_dot_nt(qh, ks[0][g]) for ks in key_sets]
            m = scores[0].max(axis=-1, keepdims=True)
            for s in scores[1:]:
                m = jnp.maximum(m, s.max(axis=-1, keepdims=True))
            acc = jnp.zeros((tq, LANES), F32)
            for s, ks in zip(scores, key_sets):
                acc = acc + _dot(jnp.exp2(s - m).astype(BF), ks[1 + half][g])
            accs.append(acc)
        out = jnp.where(low, accs[0] * (1.0 / accs[0][:, LANES - 1:LANES]), accs[1] * (1.0 / accs[1][:, 0:1]))
        o_ref[:, pair * LANES:(pair + 1) * LANES] = out.astype(BF)


def _attn_ctx_kernel(q_ref, k_ref, v_ref, o_ref):
    _attend(q_ref, o_ref, [_head_pair_operands(k_ref[...], v_ref[...])])


def _attn_lat_kernel(q_ref, kc_ref, vc_ref, kl_ref, vl_ref, o_ref, *ops):
    @pl.when(pl.program_id(1) == 0)
    def _():
        for dst, src in zip(ops, _head_pair_operands(kc_ref[...], vc_ref[...])
                            + _head_pair_operands(kl_ref[...], vl_ref[...])):
            dst[...] = src

    _attend(q_ref, o_ref, [ops[0:3], ops[3:6]])


def _attention(q, k, v, cache_k4, cache_v4, layer, geom):
    seq = geom.seq
    o_ctx = pl.pallas_call(
        _attn_ctx_kernel,
        grid=(geom.batch,),
        in_specs=[pl.BlockSpec((seq, Q_DIM), lambda b: (b, 0)),
                  pl.BlockSpec((seq, KV_DIM), lambda b: (b, 0)),
                  pl.BlockSpec((seq, KV_DIM), lambda b: (b, 0))],
        out_specs=pl.BlockSpec((seq, Q_DIM), lambda b: (b, 0)),
        out_shape=jax.ShapeDtypeStruct((geom.n_ctx, Q_DIM), BF),
        compiler_params=_params(("parallel",)),
        name="attention_context",
    )(q, k, v)

    tq = QUERY_TILE
    dseq = geom.dec_seq
    past = cache_k4.shape[2]
    q_tiles = dseq // tq
    ctx_tiles = geom.n_ctx // tq
    ctx_seqs = geom.n_ctx // dseq
    o_lat = pl.pallas_call(
        _attn_lat_kernel,
        grid=(geom.dec_batch, q_tiles),
        in_specs=[pl.BlockSpec((tq, Q_DIM), lambda b, i: (ctx_tiles + b * q_tiles + i, 0)),
                  pl.BlockSpec((None, None, past, KV_DIM), lambda b, i: (b, layer, 0, 0)),
                  pl.BlockSpec((None, None, past, KV_DIM), lambda b, i: (b, layer, 0, 0)),
                  pl.BlockSpec((dseq, KV_DIM), lambda b, i: (ctx_seqs + b, 0)),
                  pl.BlockSpec((dseq, KV_DIM), lambda b, i: (ctx_seqs + b, 0))],
        out_specs=pl.BlockSpec((tq, Q_DIM), lambda b, i: (b * q_tiles + i, 0)),
        out_shape=jax.ShapeDtypeStruct((geom.n_lat, Q_DIM), BF),
        scratch_shapes=[pltpu.VMEM((N_KV_HEADS, s, KV_DIM), BF) for s in (past, past, past, dseq, dseq, dseq)],
        compiler_params=_params(("parallel", "arbitrary")),
        name="attention_latent",
    )(q, cache_k4, cache_v4, k, v)
    return o_ctx, o_lat


def _ssd_kernel(*refs, n_chunks, has_h0, emit_state):
    it = iter(refs)
    xbc_ref, z_ref, dt_ref, dtt_ref = next(it), next(it), next(it), next(it)
    h0_ref = next(it) if has_h0 else None
    (cw_ref, cb_ref, alog_r_ref, alog_c_ref, dtb_r_ref, dtb_c_ref,
     dskip_ref, gn_ref) = (next(it) for _ in range(8))
    y_out = next(it)
    hfin_out = next(it) if emit_state else None
    pad_ref, u_ref, csc_ref, csr_ref, dtr_ref, tot_ref, st_ref, hcur_ref, y_ref = (next(it) for _ in range(9))

    seq = xbc_ref.shape[0]
    nh = SSM_HEADS
    q = CHUNK

    pad_ref[0:HALO, :] = jnp.zeros((HALO, CONV_DIM), F32)
    pad_ref[HALO:HALO + seq, :] = xbc_ref[...]
    pad_ref[HALO + seq:2 * HALO + seq, :] = jnp.zeros((HALO, CONV_DIM), F32)

    a_row = -jnp.exp(alog_r_ref[...])
    a_col = -jnp.exp(alog_c_ref[...])
    ii = lax.broadcasted_iota(I32, (q, q), 0)
    jj = lax.broadcasted_iota(I32, (q, q), 1)
    lower = jj <= ii
    upper = jj >= ii
    low_half = jj < SSM_STATE
    tri_lower = lower.astype(F32)
    tri_upper = upper.astype(F32)
    fwd_lane = lax.broadcasted_iota(I32, (q, 2 * nh), 1) < nh
    fwd_row = lax.broadcasted_iota(I32, (2 * nh, q), 0) < nh

    def chunk_stats(c, carry):
        base = pl.multiple_of(c * q, q)
        win = pad_ref[pl.ds(base, q + 2 * HALO), :]
        acc = jnp.zeros((q, CONV_DIM), F32) + cb_ref[...]
        for tap in range(CONV_W):
            off = HALO - CONV_PAD + tap
            acc = acc + win[off:off + q, :] * cw_ref[tap:tap + 1, :]
        u = _silu(acc)
        u_ref[pl.ds(base, q), :] = u

        dtc = _softplus(dt_ref[pl.ds(base, q), :] + dtb_r_ref[...])
        dtr = _softplus(dtt_ref[:, pl.ds(base, q)] + dtb_c_ref[...])
        ac = dtc * a_row
        ar = dtr * a_col
        cs_col = jnp.where(fwd_lane, _dot_exact(tri_lower, ac), _dot_exact(tri_upper, ac))
        cs_row = jnp.where(fwd_row, _dot_exact(ar, tri_upper), _dot_exact(ar, tri_lower))
        csc_ref[pl.ds(base, q), :] = cs_col
        csr_ref[:, pl.ds(base, q)] = cs_row
        dtr_ref[:, pl.ds(base, q)] = dtr
        total = jnp.where(fwd_lane[0:1, :], cs_col[q - 1:q, :], cs_col[0:1, :])
        tot_ref[c] = total
        total_c = jnp.where(fwd_row[:, 0:1], cs_row[:, q - 1:q], cs_row[:, 0:1])
        w_end = jnp.exp(total_c - cs_row) * dtr

        xs_b = u[:, 0:SSM_WIDTH].astype(BF)
        b_t = u[:, SSM_WIDTH:SSM_WIDTH + BC_DIM].T
        for dh in range(2 * nh):
            head = dh % nh
            grp = head // (nh // SSM_GROUPS)
            bw = (b_t[grp * SSM_STATE:(grp + 1) * SSM_STATE, :] * w_end[dh:dh + 1, :]).astype(BF)
            st_ref[c, dh] = _dot(bw, xs_b[:, head * SSM_HEAD_DIM:(head + 1) * SSM_HEAD_DIM])
        return carry

    lax.fori_loop(0, n_chunks, chunk_stats, 0)

    eye = (lax.broadcasted_iota(I32, (SSM_STATE, SSM_STATE), 0)
           == lax.broadcasted_iota(I32, (SSM_STATE, SSM_STATE), 1)).astype(F32)

    def transposed(h):
        return lax.dot_general(eye, h, NT_DIMS, preferred_element_type=F32, precision=lax.Precision.HIGHEST)

    for dh in range(2 * nh):
        hcur_ref[dh] = (transposed(h0_ref[dh // nh, dh % nh]) if has_h0
                        else jnp.zeros((SSM_STATE, SSM_HEAD_DIM), F32))

    def recur(step, carry):
        for direction in range(2):
            c = step if direction == 0 else n_chunks - 1 - step
            decay = jnp.exp(tot_ref[c])
            for head in range(nh):
                dh = direction * nh + head
                new = st_ref[c, dh]
                cur = hcur_ref[dh]
                st_ref[c, dh] = cur
                hcur_ref[dh] = cur * decay[:, dh:dh + 1] + new
        return carry

    lax.fori_loop(0, n_chunks, recur, 0)
    if emit_state:
        for dh in range(2 * nh):
            hfin_out[dh // nh, dh % nh] = transposed(hcur_ref[dh])

    def chunk_out(c, carry):
        base = pl.multiple_of(c * q, q)
        u = u_ref[pl.ds(base, q), :]
        xs = u[:, 0:SSM_WIDTH]
        xs_b = xs.astype(BF)
        bmat = u[:, SSM_WIDTH:SSM_WIDTH + BC_DIM].astype(BF)
        cmat = u[:, SSM_WIDTH + BC_DIM:CONV_DIM]
        cmat_b = cmat.astype(BF)
        cs_col = csc_ref[pl.ds(base, q), :]
        cs_row = csr_ref[:, pl.ds(base, q)]
        dtr = dtr_ref[:, pl.ds(base, q)]
        c_sw = pltpu.roll(cmat, SSM_STATE, 1)
        for grp in range(SSM_GROUPS):
            gs = slice(grp * SSM_STATE, (grp + 1) * SSM_STATE)
            cb = _dot_nt(cmat_b[:, gs], bmat[:, gs])
            c_dup = jnp.where(low_half, cmat, c_sw) if grp == 0 else jnp.where(low_half, c_sw, cmat)
            for head in range(grp * (nh // SSM_GROUPS), (grp + 1) * (nh // SSM_GROUPS)):
                hb = nh + head
                col_f = jnp.broadcast_to(cs_col[:, head:head + 1], (q, q))
                col_b = jnp.broadcast_to(cs_col[:, hb:hb + 1], (q, q))
                seg_f = jnp.where(lower, col_f - cs_row[head:head + 1, :], -1e30)
                seg_b = jnp.where(upper, col_b - cs_row[hb:hb + 1, :], -1e30)
                mix = cb * (jnp.exp(seg_f) * dtr[head:head + 1, :] + jnp.exp(seg_b) * dtr[hb:hb + 1, :])
                carry_in = c_dup * jnp.exp(jnp.where(low_half, col_f, col_b))
                hs = slice(head * SSM_HEAD_DIM, (head + 1) * SSM_HEAD_DIM)
                lhs = jnp.concatenate([mix.astype(BF), carry_in.astype(BF)], axis=1)
                rhs = jnp.concatenate([xs_b[:, hs], st_ref[c, head].astype(BF), st_ref[c, hb].astype(BF)], axis=0)
                y_ref[:, hs] = _dot(lhs, rhs)
        y = y_ref[...] + xs * dskip_ref[...]
        y = y * _silu(z_ref[pl.ds(base, q), :])
        ms = jnp.mean(y * y, axis=-1, keepdims=True)
        y_out[pl.ds(base, q), :] = (y * lax.rsqrt(ms + EPS) * gn_ref[...]).astype(BF)
        return carry

    lax.fori_loop(0, n_chunks, chunk_out, 0)


def _ssd_call(xbc, z, dt, dtt, h0, layer, lw, *, seq, n_seq, row_block0, emit_state, name):
    n_chunks = seq // CHUNK
    nh2 = 2 * SSM_HEADS
    full = lambda a: pl.BlockSpec(a.shape, lambda b: (0,) * a.ndim)
    params = [lw["conv_w"], lw["conv_b"], lw["alog_r"], lw["alog_c"], lw["dtb_r"], lw["dtb_c"],
              lw["dskip"], lw["gn"]]
    in_specs = [pl.BlockSpec((seq, CONV_DIM), lambda b: (row_block0 + b, 0)),
                pl.BlockSpec((seq, SSM_WIDTH), lambda b: (row_block0 + b, 0)),
                pl.BlockSpec((seq, nh2), lambda b: (row_block0 + b, 0)),
                pl.BlockSpec((nh2, seq), lambda b: (0, row_block0 + b))]
    args = [xbc, z, dt, dtt]
    if h0 is not None:
        in_specs.append(pl.BlockSpec((None, None, 2, SSM_HEADS, SSM_HEAD_DIM, SSM_STATE),
                                     lambda b: (b, layer, 0, 0, 0, 0)))
        args.append(h0)
    in_specs += [full(p) for p in params]
    args += params
    out_specs = [pl.BlockSpec((seq, SSM_WIDTH), lambda b: (b, 0))]
    out_shape = [jax.ShapeDtypeStruct((n_seq * seq, SSM_WIDTH), BF)]
    if emit_state:
        out_specs.append(pl.BlockSpec((None, 2, SSM_HEADS, SSM_HEAD_DIM, SSM_STATE),
                                      lambda b: (b, 0, 0, 0, 0)))
        out_shape.append(jax.ShapeDtypeStruct((n_seq, 2, SSM_HEADS, SSM_HEAD_DIM, SSM_STATE), F32))
    scratch = [
        pltpu.VMEM((seq + 2 * HALO, CONV_DIM), F32),
        pltpu.VMEM((seq, CONV_DIM), F32),
        pltpu.VMEM((seq, nh2), F32),
        pltpu.VMEM((nh2, seq), F32),
        pltpu.VMEM((nh2, seq), F32),
        pltpu.VMEM((n_chunks, 1, nh2), F32),
        pltpu.VMEM((n_chunks, nh2, SSM_HEAD_DIM, SSM_STATE), F32),
        pltpu.VMEM((nh2, SSM_HEAD_DIM, SSM_STATE), F32),
        pltpu.VMEM((CHUNK, SSM_WIDTH), F32),
    ]
    return pl.pallas_call(
        functools.partial(_ssd_kernel, n_chunks=n_chunks, has_h0=h0 is not None, emit_state=emit_state),
        grid=(n_seq,),
        in_specs=in_specs,
        out_specs=out_specs,
        out_shape=out_shape,
        scratch_shapes=scratch,
        compiler_params=_params(("parallel",)),
        name=name,
    )(*args)


def _ssd_mixer(xbc, z, dt, dtt, state_ssm, layer, lw, geom):
    y_ctx, h_fin = _ssd_call(xbc, z, dt, dtt, None, layer, lw, seq=geom.seq, n_seq=geom.batch,
                             row_block0=0, emit_state=True, name="ssd_context")
    (y_lat,) = _ssd_call(xbc, z, dt, dtt, state_ssm, layer, lw, seq=geom.dec_seq,
                         n_seq=geom.dec_batch, row_block0=geom.n_ctx // geom.dec_seq,
                         emit_state=False, name="ssd_latent")
    return y_ctx, y_lat, h_fin


def _outproj_kernel(ac_ref, al_ref, sc_ref, sl_ref, xc_ref, xl_ref, mod_ref, n2_ref, woa_ref, wos_ref,
                    wrh_ref, wrl_ref, x1_out, h2_out, afft_out, *, ctx_tiles):
    d = xc_ref.shape[-1]
    mod = mod_ref[...]
    g1 = mod[:, 2 * d:3 * d]
    sh2 = mod[:, 3 * d:4 * d]
    sc2 = mod[:, 4 * d:5 * d]
    is_ctx = pl.program_id(0) < ctx_tiles
    attn = jnp.where(is_ctx, ac_ref[...], al_ref[...])
    ssm = jnp.where(is_ctx, sc_ref[...], sl_ref[...])
    y = _dot(attn, woa_ref[...]) + _dot(ssm, wos_ref[...])
    x1 = jnp.where(is_ctx, xc_ref[...], xl_ref[...]) + g1 * y
    x1_out[...] = x1
    ms = jnp.mean(x1 * x1, axis=-1, keepdims=True)
    h2 = x1 * lax.rsqrt(ms + EPS) * n2_ref[...]
    h2 = h2 * (1.0 + sc2) + sh2
    h2_out[...] = h2
    hh = h2.astype(BF)
    hl = (h2 - hh.astype(F32)).astype(BF)
    wh = wrh_ref[...]
    logits = _dot_nt(wh, hh) + _dot_nt(wh, hl) + _dot_nt(wrl_ref[...], hh)
    logits = logits - logits.max(axis=0, keepdims=True)
    e = jnp.exp(logits)
    afft_out[...] = e / e.sum(axis=0, keepdims=True)


def _out_projection(attn_c, attn_l, ssm_c, ssm_l, x, mod3, lw, geom):
    xc, xl, xc_spec, xl_spec = _group_views(x, geom)
    t, d = geom.n_ctx + geom.n_lat, xc.shape[-1]
    tm = TOKEN_TILE
    nc = geom.ctx_tiles
    full = lambda a: pl.BlockSpec(a.shape, lambda i: (0,) * a.ndim)
    row = lambda w: pl.BlockSpec((tm, w), lambda i: (i, 0))
    ctx_row = lambda w: pl.BlockSpec((tm, w), lambda i: (jnp.minimum(i, nc - 1), 0))
    lat_row = lambda w: pl.BlockSpec((tm, w), lambda i: (jnp.maximum(i - nc, 0), 0))
    weights = [lw["n2"], lw["woa"], lw["wos"], lw["wrh"], lw["wrl"]]
    return pl.pallas_call(
        functools.partial(_outproj_kernel, ctx_tiles=nc),
        grid=(t // tm,),
        in_specs=[ctx_row(Q_DIM), lat_row(Q_DIM), ctx_row(SSM_WIDTH), lat_row(SSM_WIDTH), xc_spec, xl_spec,
                  pl.BlockSpec((None, 1, mod3.shape[-1]), lambda i: (geom.mod_row(i), 0, 0))]
                 + [full(w) for w in weights],
        out_specs=[row(d), row(d), pl.BlockSpec((N_EXPERTS, tm), lambda i: (0, i))],
        out_shape=[jax.ShapeDtypeStruct((t, d), F32), jax.ShapeDtypeStruct((t, d), F32),
                   jax.ShapeDtypeStruct((N_EXPERTS, t), F32)],
        compiler_params=_params(("parallel",)),
        name="out_projection",
    )(attn_c, attn_l, ssm_c, ssm_l, xc, xl, mod3, *weights)


def _flat_cumsum(m2, rows):
    gr = m2.shape[0]
    li = lax.broadcasted_iota(I32, (LANES, LANES), 0)
    lj = lax.broadcasted_iota(I32, (LANES, LANES), 1)
    mb = m2.astype(BF)
    within = _dot(mb, (li <= lj).astype(BF))
    row_tot = _dot(mb, jnp.ones((LANES, LANES), BF))
    ri = lax.broadcasted_iota(I32, (gr, gr), 0)
    rj = lax.broadcasted_iota(I32, (gr, gr), 1)
    before = ((ri // rows == rj // rows) & (rj < ri)).astype(BF)
    hi, lo = _split_base128(row_tot)
    return within + (LANES * _dot(before, hi) + _dot(before, lo))


def _split_base128(v):
    hi = jnp.floor(v * (1.0 / LANES))
    return hi.astype(BF), (v - hi * LANES).astype(BF)


def _plan_kernel(aff_ref, idx_out, gate_out, dslot_out, start_out, cnt_out, *, cap):
    a = aff_ref[...]
    n_e, rows, _ = a.shape

    def count(hit):
        return hit.astype(I32).sum(axis=2, keepdims=True).sum(axis=1, keepdims=True)

    def search(step, lo):
        cand = lo | (1 << (30 - step))
        return jnp.where(count(a >= lax.bitcast_convert_type(cand, F32)) >= cap, cand, lo)

    thr_bits = lax.fori_loop(0, 31, search, jnp.zeros((n_e, 1, 1), I32))
    thr = lax.bitcast_convert_type(thr_bits, F32)
    gt = a > thr
    eq = a == thr
    need = cap - count(gt)
    eqf = eq.astype(F32).reshape(n_e * rows, LANES)
    rank = (_flat_cumsum(eqf, rows) - eqf).reshape(n_e, rows, LANES)
    mask = gt | (eq & (rank < need.astype(F32)))
    mf = mask.astype(F32)

    c3 = _flat_cumsum(mf.reshape(n_e * rows, LANES), rows).reshape(n_e, rows, LANES)
    ahead = []
    run = jnp.zeros((rows, LANES), F32)
    for e in range(n_e):
        ahead.append(run)
        run = run + mf[e]
    cnt = run
    start = _flat_cumsum(cnt, rows) - cnt
    start_out[...] = start.astype(I32)
    cnt_out[...] = cnt.astype(I32)

    s_lane = lax.broadcasted_iota(I32, (1, cap), 1).astype(F32)
    r_col = lax.broadcasted_iota(I32, (LANES, cap), 0).astype(F32)
    pad = jnp.zeros((LANES - rows, LANES), F32)

    def lanes_first(xm):
        return (jnp.concatenate([xm, pad], axis=0) if rows < LANES else xm).T

    for e in range(n_e):
        c_e = c3[e]
        row_of = (c_e[:, LANES - 1:LANES] <= s_lane).astype(F32).sum(axis=0, keepdims=True)
        onehot = (r_col == row_of).astype(BF)
        a_t = lanes_first(a[e])
        a1 = a_t.astype(BF)
        r1 = a_t - a1.astype(F32)
        a2 = r1.astype(BF)
        a3 = (r1 - a2.astype(F32)).astype(BF)
        pieces = _split_base128(lanes_first(c_e)) + (a1, a2, a3) + _split_base128(lanes_first(start + ahead[e]))
        sel = _dot(jnp.concatenate(pieces, axis=0), onehot)
        part = [sel[j * LANES:(j + 1) * LANES] for j in range(len(pieces))]
        sel_c = LANES * part[0] + part[1]
        sel_a = part[2] + (part[3] + part[4])
        sel_d = LANES * part[5] + part[6]
        lane_of = (sel_c <= s_lane).astype(F32).sum(axis=0, keepdims=True)
        pick = r_col == lane_of
        idx_out[e:e + 1, :] = (row_of * LANES + lane_of).astype(I32)
        gate_out[e:e + 1, :] = jnp.where(pick, sel_a, 0.0).sum(axis=0, keepdims=True)
        dslot_out[e:e + 1, :] = jnp.where(pick, sel_d, 0.0).sum(axis=0, keepdims=True).astype(I32)


def _plan(aff_t, n_sets, cap):
    n_e, t = aff_t.shape
    n = t // n_sets
    rows = n // LANES
    assert rows <= LANES
    aff4 = aff_t.reshape(n_e, n_sets, rows, LANES)
    slot_spec = pl.BlockSpec((None, n_e, cap), lambda s: (s, 0, 0))
    tok_spec = pl.BlockSpec((None, rows, LANES), lambda s: (s, 0, 0))
    slot_shape = lambda dt: jax.ShapeDtypeStruct((n_sets, n_e, cap), dt)
    tok_shape = jax.ShapeDtypeStruct((n_sets, rows, LANES), I32)
    return pl.pallas_call(
        functools.partial(_plan_kernel, cap=cap),
        grid=(n_sets,),
        in_specs=[pl.BlockSpec((n_e, None, rows, LANES), lambda s: (0, s, 0, 0))],
        out_specs=[slot_spec, slot_spec, slot_spec, tok_spec, tok_spec],
        out_shape=[slot_shape(I32), slot_shape(F32), slot_shape(I32), tok_shape, tok_shape],
        compiler_params=_params(("parallel",)),
        name="expert_choice_plan",
    )(aff4)


DMA_UNROLL = 8
ROW_BLOCK = 512
EXPERT_FF_TILE = 512
COMBINE_ROWS = 256
COMBINE_BUFFERS = 4


def _expert_kernel(gidx_ref, dest_ref, h2p_hbm, gate_ref, wg_ref, wu_ref, wd_ref, z_hbm,
                   xw_a, xw_b, xb_ref, acc_a, acc_b, wgb_ref, wub_ref, wdb_ref, gcol_ref, sem_g, sem_s,
                   *, n_f, row_block):
    e = pl.program_id(0)
    f = pl.program_id(1)
    n_e = pl.num_programs(0)
    m_rows = xw_a.shape[0]
    n_blocks = m_rows // row_block
    per_block = m_rows // (n_f * n_blocks)

    def vmem_row(buf, m8, k):
        tile = pl.multiple_of(m8 + (k // SUBLANES) * SUBLANES, SUBLANES)
        return buf.at[pl.ds(tile, SUBLANES), :].at[pl.ds(k % SUBLANES, 1), :]

    def gather_copy(entry, m8, k, xw):
        return pltpu.make_async_copy(h2p_hbm.at[pl.ds(gidx_ref[entry], 1), :], vmem_row(xw, m8, k), sem_g)

    def scatter_copy(entry, m8, k, acc):
        return pltpu.make_async_copy(vmem_row(acc, m8, k), z_hbm.at[pl.ds(dest_ref[entry], 1), :], sem_s)

    def wait_gathers(xw):
        pltpu.make_async_copy(h2p_hbm.at[pl.ds(0, m_rows), :], xw, sem_g).wait()

    def wait_scatters(acc):
        pltpu.make_async_copy(acc, z_hbm.at[pl.ds(0, m_rows), :], sem_s).wait()

    def issue_all(make):
        def body(j, carry):
            for k in range(DMA_UNROLL):
                make(j * DMA_UNROLL, k).start()
            return carry
        lax.fori_loop(0, m_rows // DMA_UNROLL, body, 0)

    def run(xw_cur, xw_next, acc_cur, acc_prev):
        @pl.when((e == 0) & (f == 0))
        def _():
            acc_prev[...] = jnp.zeros_like(acc_prev)
            issue_all(lambda m8, k: gather_copy(m8 + k, m8, k, xw_cur))

        @pl.when(f == 0)
        def _():
            wait_gathers(xw_cur)
            xb_ref[...] = xw_cur[...].astype(BF)
            acc_cur[...] = jnp.zeros_like(acc_cur)
            diag = (lax.broadcasted_iota(I32, (LANES, LANES), 0) == lax.broadcasted_iota(I32, (LANES, LANES), 1))
            for j in range(gate_ref.shape[0]):
                spread = jnp.where(diag, gate_ref[j:j + 1, :], 0.0)
                gcol_ref[j * LANES:(j + 1) * LANES, :] = spread.sum(axis=1, keepdims=True)

        wgb_ref[...] = wg_ref[...].astype(BF)
        wub_ref[...] = wu_ref[...].astype(BF)
        wdb_ref[...] = wd_ref[...].astype(BF)
        last = f == n_f - 1

        def block(rb, carry):
            r0 = pl.multiple_of(rb * row_block, row_block)
            m0 = (f * n_blocks + rb) * per_block

            for k in range(per_block):
                gather_copy((e + 1) * m_rows + m0 + k, m0, k, xw_next).start()
                scatter_copy(e * m_rows + m0 + k, m0, k, acc_prev).start()
            x = xb_ref[pl.ds(r0, row_block), :]
            g = _dot(x, wgb_ref[...])
            u = _dot(x, wub_ref[...])
            hid = (_silu(g) * u).astype(BF)
            total = acc_cur[pl.ds(r0, row_block), :] + _dot(hid, wdb_ref[...])
            gate = jnp.where(last, gcol_ref[pl.ds(r0, row_block), :], 1.0)
            acc_cur[pl.ds(r0, row_block), :] = total * gate
            return carry

        lax.fori_loop(0, n_blocks, block, 0)

        @pl.when(last)
        def _():
            wait_scatters(acc_prev)

        @pl.when(last & (e == n_e - 1))
        def _():
            wait_gathers(xw_next)
            issue_all(lambda m8, k: scatter_copy((e + 1) * m_rows + m8 + k, m8, k, acc_cur))
            wait_scatters(acc_cur)

    @pl.when(e % 2 == 0)
    def _():
        run(xw_a, xw_b, acc_a, acc_b)

    @pl.when(e % 2 == 1)
    def _():
        run(xw_b, xw_a, acc_b, acc_a)


def _expert_ffn(gidx, dest, h2p, gates, w_gate, w_up, w_down, layer):
    n_e, m = gates.shape
    gates = gates.reshape(n_e, m // LANES, LANES)
    d = h2p.shape[1]
    ff = w_gate.shape[-1]
    tf = min(EXPERT_FF_TILE, ff)
    n_f = ff // tf
    row_block = min(ROW_BLOCK, m)
    assert m % DMA_UNROLL == 0 and (n_e * m) % COMBINE_ROWS == 0 and m % (n_f * (m // row_block)) == 0
    gidx_ext = jnp.concatenate([gidx, jnp.arange(m, dtype=I32)])
    dest_ext = jnp.concatenate([n_e * m + jnp.arange(m, dtype=I32), dest])
    return pl.pallas_call(
        functools.partial(_expert_kernel, n_f=n_f, row_block=row_block),
        grid_spec=pltpu.PrefetchScalarGridSpec(
            num_scalar_prefetch=2,
            grid=(n_e, n_f),
            in_specs=[pl.BlockSpec(memory_space=pl.ANY),
                      pl.BlockSpec((None, m // LANES, LANES), lambda e, f, gi, de: (e, 0, 0)),
                      pl.BlockSpec((None, None, d, tf), lambda e, f, gi, de: (layer, e, 0, f)),
                      pl.BlockSpec((None, None, d, tf), lambda e, f, gi, de: (layer, e, 0, f)),
                      pl.BlockSpec((None, None, tf, d), lambda e, f, gi, de: (layer, e, f, 0))],
            out_specs=pl.BlockSpec(memory_space=pl.ANY),
            scratch_shapes=[pltpu.VMEM((m, d), F32), pltpu.VMEM((m, d), F32),
                            pltpu.VMEM((m, d), BF), pltpu.VMEM((m, d), F32), pltpu.VMEM((m, d), F32),
                            pltpu.VMEM((d, tf), BF), pltpu.VMEM((d, tf), BF), pltpu.VMEM((tf, d), BF),
                            pltpu.VMEM((m, 1), F32),
                            pltpu.SemaphoreType.DMA, pltpu.SemaphoreType.DMA]),
        out_shape=jax.ShapeDtypeStruct(((n_e + 1) * m, d), F32),
        compiler_params=_params(("arbitrary", "arbitrary")),
        name="expert_swiglu",
    )(gidx_ext, dest_ext, h2p, gates, w_gate, w_up, w_down)


def _combine_kernel(lo_ref, hi_ref, z_hbm, start_ref, end_ref, x1_ref, mod_ref, o_ref, zbuf, acc_ref, sem,
                    state, *, n_chunks):
    i = pl.program_id(0)
    d = x1_ref.shape[-1]
    n_buf, ch, _ = zbuf.shape
    lo = lo_ref[i]
    hi = hi_ref[i]

    def chunk_copy(kc):
        slot = kc % n_buf
        return pltpu.make_async_copy(z_hbm.at[pl.ds(pl.multiple_of(kc * ch, ch), ch), :], zbuf.at[slot],
                                     sem.at[slot])

    @pl.when(i == 0)
    def _():
        state[0] = 0
        state[1] = 0

    acc_ref[...] = jnp.zeros_like(acc_ref)
    start = start_ref[...]
    end = end_ref[...]

    def body(kc, carry):
        upto = jnp.minimum(kc + n_buf, n_chunks)

        def start_one(j, c):
            chunk_copy(j).start()
            return c

        lax.fori_loop(state[0], upto, start_one, 0)
        state[0] = jnp.maximum(state[0], upto)

        @pl.when(kc >= state[1])
        def _():
            chunk_copy(kc).wait()
            state[1] = kc + 1

        slot = kc % n_buf
        k = kc * ch + lax.broadcasted_iota(I32, (1, ch), 1)
        own = jnp.where((k >= start) & (k < end), 1.0, 0.0).astype(BF)
        zc = zbuf[slot]
        zh = zc.astype(BF)
        zl = (zc - zh.astype(F32)).astype(BF)
        acc_ref[...] += _dot(own, zh) + _dot(own, zl)
        return carry

    lax.fori_loop(lo, hi, body, 0)
    g2 = mod_ref[...][:, 5 * d:6 * d]
    o_ref[...] = x1_ref[...] + g2 * acc_ref[...]


def _combine(z, n_rows, start, end, tile_lo, tile_hi, x1, mod3, geom):
    t, d = x1.shape
    tm = TOKEN_TILE
    return pl.pallas_call(
        functools.partial(_combine_kernel, n_chunks=n_rows // COMBINE_ROWS),
        grid_spec=pltpu.PrefetchScalarGridSpec(
            num_scalar_prefetch=2,
            grid=(t // tm,),
            in_specs=[pl.BlockSpec(memory_space=pl.ANY),
                      pl.BlockSpec((tm, 1), lambda i, lo, hi: (i, 0)),
                      pl.BlockSpec((tm, 1), lambda i, lo, hi: (i, 0)),
                      pl.BlockSpec((tm, d), lambda i, lo, hi: (i, 0)),
                      pl.BlockSpec((None, 1, mod3.shape[-1]), lambda i, lo, hi: (geom.mod_row(i), 0, 0))],
            out_specs=pl.BlockSpec((tm, d), lambda i, lo, hi: (i, 0)),
            scratch_shapes=[pltpu.VMEM((COMBINE_BUFFERS, COMBINE_ROWS, d), F32), pltpu.VMEM((tm, d), F32),
                            pltpu.SemaphoreType.DMA((COMBINE_BUFFERS,)), pltpu.SMEM((2,), I32)]),
        out_shape=jax.ShapeDtypeStruct((t, d), F32),
        compiler_params=_params(("arbitrary",)),
        name="gated_combine",
    )(tile_lo, tile_hi, z, start, end, x1, mod3)


class _Geometry:
    def __init__(self, batch, seq, dec_batch, dec_seq):
        self.batch, self.seq, self.dec_batch, self.dec_seq = batch, seq, dec_batch, dec_seq
        self.n_ctx = batch * seq
        self.n_lat = dec_batch * dec_seq
        tm = TOKEN_TILE
        assert self.n_ctx % tm == 0 and dec_seq % tm == 0 and self.n_ctx % dec_seq == 0
        assert seq % CHUNK == 0 and dec_seq % CHUNK == 0 and dec_seq % QUERY_TILE == 0
        self.ctx_tiles = self.n_ctx // tm
        self.lat_tiles_per_seq = dec_seq // tm

    def mod_row(self, i):
        return jnp.where(i < self.ctx_tiles, 0, 1 + (i - self.ctx_tiles) // self.lat_tiles_per_seq)

    def rope_block(self, i):
        return jnp.where(i < self.ctx_tiles, 0, 1 + (i - self.ctx_tiles) % self.lat_tiles_per_seq)


def _rope_tables(dec_seq):
    t = jnp.arange(dec_seq)
    row = (t // GRID_W).astype(F32)
    col = (t % GRID_W).astype(F32)
    inv = ROPE_THETA ** (-jnp.arange(ROT_PAIRS, dtype=F32) / ROT_PAIRS)
    ang = jnp.concatenate([row[:, None] * inv, col[:, None] * inv], axis=-1)
    cos = jnp.tile(jnp.concatenate([jnp.cos(ang), jnp.cos(ang)], axis=-1), (1, N_HEADS))
    sin = jnp.tile(jnp.concatenate([-jnp.sin(ang), jnp.sin(ang)], axis=-1), (1, N_HEADS))
    ident = jnp.ones((TOKEN_TILE, Q_DIM), F32)
    return (jnp.concatenate([ident, cos], axis=0),
            jnp.concatenate([jnp.zeros_like(ident), sin], axis=0))


def _block_diag_ones(width):
    i = jnp.arange(width)
    return (i[:, None] // HEAD_DIM == i[None, :] // HEAD_DIM).astype(BF)


def _layer_weights(l, norm1, norm2, w_in, conv_w, conv_b, q_norm, k_norm, a_log, dt_bias, d_skip,
                   ssm_norm, w_out, w_router):
    w = w_in[l].astype(BF)
    o = 0
    parts = {}
    for name, width in (("wq", Q_DIM), ("wk", KV_DIM), ("wv", KV_DIM), ("wx", CONV_DIM),
                        ("wz", SSM_WIDTH), ("wdt", 2 * SSM_HEADS)):
        parts[name] = w[:, o:o + width]
        o += width
    wo = w_out[l].astype(BF)
    wr = w_router[l].T
    wrh = wr.astype(BF)
    nh2 = 2 * SSM_HEADS
    parts.update(
        wdtt=parts["wdt"].T,
        n1=norm1[l][None, :], n2=norm2[l][None, :],
        gq=jnp.tile(q_norm[l], N_HEADS)[None, :], gk=jnp.tile(k_norm[l], N_KV_HEADS)[None, :],
        conv_w=conv_w[l], conv_b=conv_b[l][None, :],
        alog_r=a_log[l].reshape(1, nh2), alog_c=a_log[l].reshape(nh2, 1),
        dtb_r=dt_bias[l].reshape(1, nh2), dtb_c=dt_bias[l].reshape(nh2, 1),
        dskip=jnp.repeat(d_skip[l], SSM_HEAD_DIM)[None, :], gn=ssm_norm[l][None, :],
        woa=wo[:Q_DIM], wos=wo[Q_DIM:],
        wrh=wrh, wrl=(wr - wrh.astype(F32)).astype(BF),
    )
    return parts


def _moe(h2p, aff_t, x1, mod3, w_gate, w_up, w_down, layer, n_sets, geom):
    t = x1.shape[0]
    n = t // n_sets
    cap = EC_FACTOR * n // N_EXPERTS
    pairs = N_EXPERTS * cap
    idx, gates, dslot, start, cnt = _plan(aff_t, n_sets, cap)
    set_ix = jnp.arange(n_sets, dtype=I32)[:, None, None]
    expert_major = lambda a: a.transpose(1, 0, 2).reshape(N_EXPERTS, n_sets * cap)
    gidx = expert_major(idx + set_ix * n).reshape(-1)
    dest = expert_major(dslot + set_ix * pairs).reshape(-1)
    z = _expert_ffn(gidx, dest, h2p, expert_major(gates), w_gate, w_up, w_down, layer)
    start_g = (start.reshape(n_sets, n) + set_ix[:, :, 0] * pairs).reshape(t, 1)
    end_g = start_g + cnt.reshape(t, 1)
    tile_lo = start_g[::TOKEN_TILE, 0] // COMBINE_ROWS
    tile_hi = (end_g[TOKEN_TILE - 1::TOKEN_TILE, 0] + COMBINE_ROWS - 1) // COMBINE_ROWS
    return _combine(z, n_sets * pairs, start_g, end_g, tile_lo, tile_hi, x1, mod3, geom)


def kernel(x_prompt, x_sample, cache_k, cache_v, state_ssm, c, c_ctx, norm1, norm2, ada_w, ada_b,
           w_in, conv_w, conv_b, q_norm, k_norm, A_log, dt_bias, d_skip, ssm_norm, w_out,
           w_router, w_gate, w_up, w_down):
    batch, seq, d = x_prompt.shape
    dec_batch, dec_seq, _ = x_sample.shape
    depth = w_in.shape[0]
    past = cache_k.shape[2]
    geom = _Geometry(batch, seq, dec_batch, dec_seq)
    assert geom.n_ctx == geom.n_lat, "routed sets are stacked as two equal halves"

    rows = -(-(1 + dec_batch) // SUBLANES) * SUBLANES
    cv = jnp.zeros((rows, d), F32).at[0].set(c_ctx).at[1:1 + dec_batch].set(c)
    mod = _ada_modulation(cv, ada_w, ada_b)
    cos, sin = _rope_tables(dec_seq)
    consts = dict(cos=cos, sin=sin, bdq=_block_diag_ones(Q_DIM), bdk=_block_diag_ones(KV_DIM))
    cache_k4 = cache_k.reshape(dec_batch, depth, past, KV_DIM)
    cache_v4 = cache_v.reshape(dec_batch, depth, past, KV_DIM)

    x = (x_prompt.reshape(geom.n_ctx, d), x_sample.reshape(geom.n_lat, d))
    ks_new, vs_new, hs_new = [], [], []
    for l in range(depth):
        lw = _layer_weights(l, norm1, norm2, w_in, conv_w, conv_b, q_norm, k_norm, A_log, dt_bias,
                            d_skip, ssm_norm, w_out, w_router)
        mod3 = mod[l][:, None, :]
        q, k, v, xbc, z, dt, dtt = _in_projection(x, mod3, lw, consts, geom)
        attn_c, attn_l = _attention(q, k, v, cache_k4, cache_v4, l, geom)
        ssm_c, ssm_l, h_fin = _ssd_mixer(xbc, z, dt, dtt, state_ssm, l, lw, geom)
        x1, h2p, aff_t = _out_projection(attn_c, attn_l, ssm_c, ssm_l, x, mod3, lw, geom)
        x = _moe(h2p, aff_t, x1, mod3, w_gate, w_up, w_down, l, 2, geom)
        ks_new.append(k[:geom.n_ctx].reshape(batch, seq, N_KV_HEADS, HEAD_DIM))
        vs_new.append(v[:geom.n_ctx].reshape(batch, seq, N_KV_HEADS, HEAD_DIM))
        hs_new.append(h_fin)
    y_prompt = x[:geom.n_ctx].reshape(batch, seq, d)
    y_sample = x[geom.n_ctx:].reshape(dec_batch, dec_seq, d)
    return (y_prompt, y_sample, jnp.stack(ks_new, axis=1), jnp.stack(vs_new, axis=1),
            jnp.stack(hs_new, axis=1))
```

```python
import functools

import jax
import jax.numpy as jnp
from jax import lax
from jax.experimental import pallas as pl
from jax.experimental.pallas import tpu as pltpu

F32 = jnp.float32
BF = jnp.bfloat16
I32 = jnp.int32

GRID_W = 64
N_HEADS = 8
N_KV_HEADS = 2
HEAD_DIM = 64
GQA = N_HEADS // N_KV_HEADS
ROPE_THETA = 10000.0
ROT_PAIRS = HEAD_DIM // 4
Q_DIM = N_HEADS * HEAD_DIM
KV_DIM = N_KV_HEADS * HEAD_DIM
SSM_WIDTH = 512
SSM_HEADS = 8
SSM_HEAD_DIM = 64
SSM_GROUPS = 2
SSM_STATE = 64
CONV_W = 5
CONV_PAD = CONV_W // 2
CHUNK = 128
BC_DIM = SSM_GROUPS * SSM_STATE
CONV_DIM = SSM_WIDTH + 2 * BC_DIM
N_EXPERTS = 16
EC_FACTOR = 2
EPS = 1e-6
SCALE = HEAD_DIM ** -0.5
LOG2_E = 1.4426950408889634

LANES = 128
SUBLANES = 8
TOKEN_TILE = 512
QUERY_TILE = 256
HALO = SUBLANES
VMEM_LIMIT = 56 * 1024 * 1024

NT_DIMS = (((1,), (1,)), ((), ()))


def _dot(a, b):
    return jnp.dot(a, b, preferred_element_type=F32)


def _dot_nt(a, b):
    return lax.dot_general(a, b, NT_DIMS, preferred_element_type=F32)


def _dot_exact(a, b):
    return jnp.dot(a, b, preferred_element_type=F32, precision=lax.Precision.HIGHEST)


def _silu(x):
    return x / (1.0 + jnp.exp(-x))


def _softplus(x):
    return jnp.maximum(x, 0.0) + jnp.log(1.0 + jnp.exp(-jnp.abs(x)))


def _params(sem):
    return pltpu.CompilerParams(dimension_semantics=sem, vmem_limit_bytes=VMEM_LIMIT)


def _ada_kernel(cv_ref, w_ref, b_ref, o_ref):
    s = _silu(cv_ref[...]).astype(BF)
    o_ref[...] = _dot(s, w_ref[...].astype(BF)) + b_ref[...]


def _ada_modulation(cv, ada_w, ada_b):
    depth, d, n6 = ada_w.shape
    rows = cv.shape[0]
    tn = n6 // 4
    return pl.pallas_call(
        _ada_kernel,
        grid=(depth, n6 // tn),
        in_specs=[
            pl.BlockSpec((rows, d), lambda l, j: (0, 0)),
            pl.BlockSpec((None, d, tn), lambda l, j: (l, 0, j)),
            pl.BlockSpec((None, 1, tn), lambda l, j: (l, 0, j)),
        ],
        out_specs=pl.BlockSpec((None, rows, tn), lambda l, j: (l, 0, j)),
        out_shape=jax.ShapeDtypeStruct((depth, rows, n6), F32),
        compiler_params=_params(("parallel", "parallel")),
        name="ada_modulation",
    )(cv, ada_w, ada_b.reshape(depth, 1, n6))


def _inproj_kernel(xc_ref, xl_ref, mod_ref, n1_ref, wq_ref, wk_ref, wv_ref, wx_ref, wz_ref, wdt_ref, wdtt_ref,
                   bdq_ref, bdk_ref, gq_ref, gk_ref, cos_ref, sin_ref,
                   q_out, k_out, v_out, xbc_out, z_out, dt_out, dtt_out, *, ctx_tiles):
    d = xc_ref.shape[-1]
    x = jnp.where(pl.program_id(0) < ctx_tiles, xc_ref[...], xl_ref[...])
    mod = mod_ref[...]
    sh1 = mod[:, 0:d]
    sc1 = mod[:, d:2 * d]
    ms = jnp.mean(x * x, axis=-1, keepdims=True)
    h = x * lax.rsqrt(ms + EPS) * n1_ref[...]
    hb = (h * (1.0 + sc1) + sh1).astype(BF)
    cos = cos_ref[...]
    sin = sin_ref[...]

    def norm_rope(t, bd_ref, g_ref, cos_t, sin_t):
        ss = _dot((t * t).astype(BF), bd_ref[...])
        tn = t * lax.rsqrt(ss * (1.0 / HEAD_DIM) + EPS) * g_ref[...]
        width = t.shape[-1]
        lane = lax.broadcasted_iota(I32, tn.shape, 1)
        first_half = (lane % HEAD_DIM) < (HEAD_DIM // 2)
        partner = jnp.where(first_half,
                            pltpu.roll(tn, width - HEAD_DIM // 2, 1),
                            pltpu.roll(tn, HEAD_DIM // 2, 1))
        return tn * cos_t + partner * sin_t

    q = norm_rope(_dot(hb, wq_ref[...]), bdq_ref, gq_ref, cos, sin)
    q_out[...] = (q * (SCALE * LOG2_E)).astype(BF)
    k_out[...] = norm_rope(_dot(hb, wk_ref[...]), bdk_ref, gk_ref, cos[:, :KV_DIM], sin[:, :KV_DIM])
    v_out[...] = _dot(hb, wv_ref[...])
    xbc_out[...] = _dot(hb, wx_ref[...])
    z_out[...] = _dot(hb, wz_ref[...])
    dt_out[...] = _dot(hb, wdt_ref[...])
    dtt_out[...] = _dot_nt(wdtt_ref[...], hb)


def _group_views(x, geom):
    tm = TOKEN_TILE
    nc = geom.ctx_tiles
    if isinstance(x, tuple):
        xc, xl = x
        lat_index = lambda i: (jnp.maximum(i - nc, 0), 0)
    else:
        xc = xl = x
        lat_index = lambda i: (jnp.maximum(i, nc), 0)
    d = xc.shape[-1]
    return xc, xl, pl.BlockSpec((tm, d), lambda i: (jnp.minimum(i, nc - 1), 0)), pl.BlockSpec((tm, d), lat_index)


def _in_projection(x, mod3, lw, consts, geom):
    xc, xl, xc_spec, xl_spec = _group_views(x, geom)
    t, d = geom.n_ctx + geom.n_lat, xc.shape[-1]
    tm = TOKEN_TILE
    full = lambda a: pl.BlockSpec(a.shape, lambda i: (0,) * a.ndim)
    row = lambda w: pl.BlockSpec((tm, w), lambda i: (i, 0))
    weights = [lw["wq"], lw["wk"], lw["wv"], lw["wx"], lw["wz"], lw["wdt"], lw["wdtt"],
               consts["bdq"], consts["bdk"], lw["gq"], lw["gk"]]
    out_shape = [
        jax.ShapeDtypeStruct((t, Q_DIM), BF),
        jax.ShapeDtypeStruct((t, KV_DIM), F32),
        jax.ShapeDtypeStruct((t, KV_DIM), F32),
        jax.ShapeDtypeStruct((t, CONV_DIM), F32),
        jax.ShapeDtypeStruct((t, SSM_WIDTH), F32),
        jax.ShapeDtypeStruct((t, 2 * SSM_HEADS), F32),
        jax.ShapeDtypeStruct((2 * SSM_HEADS, t), F32),
    ]
    return pl.pallas_call(
        functools.partial(_inproj_kernel, ctx_tiles=geom.ctx_tiles),
        grid=(t // tm,),
        in_specs=[xc_spec, xl_spec,
                  pl.BlockSpec((None, 1, mod3.shape[-1]), lambda i: (geom.mod_row(i), 0, 0)),
                  full(lw["n1"])]
                 + [full(w) for w in weights]
                 + [pl.BlockSpec((tm, Q_DIM), lambda i: (geom.rope_block(i), 0)),
                    pl.BlockSpec((tm, Q_DIM), lambda i: (geom.rope_block(i), 0))],
        out_specs=[row(Q_DIM), row(KV_DIM), row(KV_DIM), row(CONV_DIM), row(SSM_WIDTH),
                   row(2 * SSM_HEADS), pl.BlockSpec((2 * SSM_HEADS, tm), lambda i: (0, i))],
        out_shape=out_shape,
        compiler_params=_params(("parallel",)),
        name="in_projection",
    )(xc, xl, mod3, lw["n1"], *weights, consts["cos"], consts["sin"])


def _head_pair_operands(k, v):
    lane = lax.broadcasted_iota(I32, k.shape, 1)
    low = lane < HEAD_DIM
    k_sw = pltpu.roll(k, HEAD_DIM, 1)
    v_sw = pltpu.roll(v, HEAD_DIM, 1)
    ones_last = jnp.where(lane == LANES - 1, 1.0, 0.0)
    ones_first = jnp.where(lane == 0, 1.0, 0.0)
    kk = jnp.stack([jnp.where(low, k, k_sw), jnp.where(low, k_sw, k)]).astype(BF)
    v_lo = jnp.stack([jnp.where(low, v, ones_last), jnp.where(low, v_sw, ones_last)]).astype(BF)
    v_hi = jnp.stack([jnp.where(low, ones_first, v_sw), jnp.where(low, ones_first, v)]).astype(BF)
    return kk, v_lo, v_hi


def _attend(q_ref, o_ref, key_sets):
    tq = q_ref.shape[0]
    lane = lax.broadcasted_iota(I32, (tq, LANES), 1)
    low = lane < HEAD_DIM
    for pair in range(N_HEADS // 2):
        g = (2 * pair) // GQA
        q2 = q_ref[:, pair * LANES:(pair + 1) * LANES]
        zero = jnp.zeros_like(q2)
        accs = []
        for half, qh in enumerate((jnp.where(low, q2, zero), jnp.where(low, zero, q2))):
            scores = [_dot_nt(qh, ks[0][g]) for ks in key_sets]
            m = scores[0].max(axis=-1, keepdims=True)
            for s in scores[1:]:
                m = jnp.maximum(m, s.max(axis=-1, keepdims=True))
            acc = jnp.zeros((tq, LANES), F32)
            for s, ks in zip(scores, key_sets):
                acc = acc + _dot(jnp.exp2(s - m).astype(BF), ks[1 + half][g])
            accs.append(acc)
        out = jnp.where(low, accs[0] * (1.0 / accs[0][:, LANES - 1:LANES]), accs[1] * (1.0 / accs[1][:, 0:1]))
        o_ref[:, pair * LANES:(pair + 1) * LANES] = out.astype(BF)


def _attn_ctx_kernel(q_ref, k_ref, v_ref, o_ref):
    _attend(q_ref, o_ref, [_head_pair_operands(k_ref[...], v_ref[...])])


def _attn_lat_kernel(q_ref, kc_ref, vc_ref, kl_ref, vl_ref, o_ref, *ops):
    @pl.when(pl.program_id(1) == 0)
    def _():
        for dst, src in zip(ops, _head_pair_operands(kc_ref[...], vc_ref[...])
                            + _head_pair_operands(kl_ref[...], vl_ref[...])):
            dst[...] = src

    _attend(q_ref, o_ref, [ops[0:3], ops[3:6]])


def _attention(q, k, v, cache_k4, cache_v4, layer, geom):
    seq = geom.seq
    o_ctx = pl.pallas_call(
        _attn_ctx_kernel,
        grid=(geom.batch,),
        in_specs=[pl.BlockSpec((seq, Q_DIM), lambda b: (b, 0)),
                  pl.BlockSpec((seq, KV_DIM), lambda b: (b, 0)),
                  pl.BlockSpec((seq, KV_DIM), lambda b: (b, 0))],
        out_specs=pl.BlockSpec((seq, Q_DIM), lambda b: (b, 0)),
        out_shape=jax.ShapeDtypeStruct((geom.n_ctx, Q_DIM), BF),
        compiler_params=_params(("parallel",)),
        name="attention_context",
    )(q, k, v)

    tq = QUERY_TILE
    dseq = geom.dec_seq
    past = cache_k4.shape[2]
    q_tiles = dseq // tq
    ctx_tiles = geom.n_ctx // tq
    ctx_seqs = geom.n_ctx // dseq
    o_lat = pl.pallas_call(
        _attn_lat_kernel,
        grid=(geom.dec_batch, q_tiles),
        in_specs=[pl.BlockSpec((tq, Q_DIM), lambda b, i: (ctx_tiles + b * q_tiles + i, 0)),
                  pl.BlockSpec((None, None, past, KV_DIM), lambda b, i: (b, layer, 0, 0)),
                  pl.BlockSpec((None, None, past, KV_DIM), lambda b, i: (b, layer, 0, 0)),
                  pl.BlockSpec((dseq, KV_DIM), lambda b, i: (ctx_seqs + b, 0)),
                  pl.BlockSpec((dseq, KV_DIM), lambda b, i: (ctx_seqs + b, 0))],
        out_specs=pl.BlockSpec((tq, Q_DIM), lambda b, i: (b * q_tiles + i, 0)),
        out_shape=jax.ShapeDtypeStruct((geom.n_lat, Q_DIM), BF),
        scratch_shapes=[pltpu.VMEM((N_KV_HEADS, s, KV_DIM), BF) for s in (past, past, past, dseq, dseq, dseq)],
        compiler_params=_params(("parallel", "arbitrary")),
        name="attention_latent",
    )(q, cache_k4, cache_v4, k, v)
    return o_ctx, o_lat


def _ssd_kernel(*refs, n_chunks, has_h0, emit_state):
    it = iter(refs)
    xbc_ref, z_ref, dt_ref, dtt_ref = next(it), next(it), next(it), next(it)
    h0_ref = next(it) if has_h0 else None
    (cw_ref, cb_ref, alog_r_ref, alog_c_ref, dtb_r_ref, dtb_c_ref,
     dskip_ref, gn_ref) = (next(it) for _ in range(8))
    y_out = next(it)
    hfin_out = next(it) if emit_state else None
    pad_ref, u_ref, csc_ref, csr_ref, dtr_ref, tot_ref, st_ref, hcur_ref, y_ref = (next(it) for _ in range(9))

    seq = xbc_ref.shape[0]
    nh = SSM_HEADS
    q = CHUNK

    pad_ref[0:HALO, :] = jnp.zeros((HALO, CONV_DIM), F32)
    pad_ref[HALO:HALO + seq, :] = xbc_ref[...]
    pad_ref[HALO + seq:2 * HALO + seq, :] = jnp.zeros((HALO, CONV_DIM), F32)

    a_row = -jnp.exp(alog_r_ref[...])
    a_col = -jnp.exp(alog_c_ref[...])
    ii = lax.broadcasted_iota(I32, (q, q), 0)
    jj = lax.broadcasted_iota(I32, (q, q), 1)
    lower = jj <= ii
    upper = jj >= ii
    low_half = jj < SSM_STATE
    tri_lower = lower.astype(F32)
    tri_upper = upper.astype(F32)
    fwd_lane = lax.broadcasted_iota(I32, (q, 2 * nh), 1) < nh
    fwd_row = lax.broadcasted_iota(I32, (2 * nh, q), 0) < nh

    def chunk_stats(c, carry):
        base = pl.multiple_of(c * q, q)
        win = pad_ref[pl.ds(base, q + 2 * HALO), :]
        acc = jnp.zeros((q, CONV_DIM), F32) + cb_ref[...]
        for tap in range(CONV_W):
            off = HALO - CONV_PAD + tap
            acc = acc + win[off:off + q, :] * cw_ref[tap:tap + 1, :]
        u = _silu(acc)
        u_ref[pl.ds(base, q), :] = u

        dtc = _softplus(dt_ref[pl.ds(base, q), :] + dtb_r_ref[...])
        dtr = _softplus(dtt_ref[:, pl.ds(base, q)] + dtb_c_ref[...])
        ac = dtc * a_row
        ar = dtr * a_col
        cs_col = jnp.where(fwd_lane, _dot_exact(tri_lower, ac), _dot_exact(tri_upper, ac))
        cs_row = jnp.where(fwd_row, _dot_exact(ar, tri_upper), _dot_exact(ar, tri_lower))
        csc_ref[pl.ds(base, q), :] = cs_col
        csr_ref[:, pl.ds(base, q)] = cs_row
        dtr_ref[:, pl.ds(base, q)] = dtr
        total = jnp.where(fwd_lane[0:1, :], cs_col[q - 1:q, :], cs_col[0:1, :])
        tot_ref[c] = total
        total_c = jnp.where(fwd_row[:, 0:1], cs_row[:, q - 1:q], cs_row[:, 0:1])
        w_end = jnp.exp(total_c - cs_row) * dtr

        xs_b = u[:, 0:SSM_WIDTH].astype(BF)
        b_t = u[:, SSM_WIDTH:SSM_WIDTH + BC_DIM].T
        for dh in range(2 * nh):
            head = dh % nh
            grp = head // (nh // SSM_GROUPS)
            bw = (b_t[grp * SSM_STATE:(grp + 1) * SSM_STATE, :] * w_end[dh:dh + 1, :]).astype(BF)
            st_ref[c, dh] = _dot(bw, xs_b[:, head * SSM_HEAD_DIM:(head + 1) * SSM_HEAD_DIM])
        return carry

    lax.fori_loop(0, n_chunks, chunk_stats, 0)

    eye = (lax.broadcasted_iota(I32, (SSM_STATE, SSM_STATE), 0)
           == lax.broadcasted_iota(I32, (SSM_STATE, SSM_STATE), 1)).astype(BF)

    def transposed_blocks(stack):
        p1 = stack.astype(BF)
        r1 = stack - p1.astype(F32)
        p2 = r1.astype(BF)
        p3 = (r1 - p2.astype(F32)).astype(BF)
        return _dot_nt(eye, p1) + (_dot_nt(eye, p2) + _dot_nt(eye, p3))

    if has_h0:
        h0_t = transposed_blocks(h0_ref[...].reshape(2 * nh * SSM_HEAD_DIM, SSM_STATE))
    for dh in range(2 * nh):
        hcur_ref[dh] = (h0_t[:, dh * SSM_HEAD_DIM:(dh + 1) * SSM_HEAD_DIM] if has_h0
                        else jnp.zeros((SSM_STATE, SSM_HEAD_DIM), F32))

    def recur(step, carry):
        for direction in range(2):
            c = step if direction == 0 else n_chunks - 1 - step
            decay = jnp.exp(tot_ref[c])
            for head in range(nh):
                dh = direction * nh + head
                new = st_ref[c, dh]
                cur = hcur_ref[dh]
                st_ref[c, dh] = cur
                hcur_ref[dh] = cur * decay[:, dh:dh + 1] + new
        return carry

    lax.fori_loop(0, n_chunks, recur, 0)
    if emit_state:
        h_t = transposed_blocks(hcur_ref[...].reshape(2 * nh * SSM_STATE, SSM_HEAD_DIM))
        for dh in range(2 * nh):
            hfin_out[dh // nh, dh % nh] = h_t[:, dh * SSM_STATE:(dh + 1) * SSM_STATE]

    def chunk_out(c, carry):
        base = pl.multiple_of(c * q, q)
        u = u_ref[pl.ds(base, q), :]
        xs = u[:, 0:SSM_WIDTH]
        xs_b = xs.astype(BF)
        bmat = u[:, SSM_WIDTH:SSM_WIDTH + BC_DIM].astype(BF)
        cmat = u[:, SSM_WIDTH + BC_DIM:CONV_DIM]
        cmat_b = cmat.astype(BF)
        cs_col = csc_ref[pl.ds(base, q), :]
        cs_row = csr_ref[:, pl.ds(base, q)]
        dtr = dtr_ref[:, pl.ds(base, q)]
        c_sw = pltpu.roll(cmat, SSM_STATE, 1)
        for grp in range(SSM_GROUPS):
            gs = slice(grp * SSM_STATE, (grp + 1) * SSM_STATE)
            cb = _dot_nt(cmat_b[:, gs], bmat[:, gs])
            c_dup = jnp.where(low_half, cmat, c_sw) if grp == 0 else jnp.where(low_half, c_sw, cmat)
            for head in range(grp * (nh // SSM_GROUPS), (grp + 1) * (nh // SSM_GROUPS)):
                hb = nh + head
                col_f = jnp.broadcast_to(cs_col[:, head:head + 1], (q, q))
                col_b = jnp.broadcast_to(cs_col[:, hb:hb + 1], (q, q))
                seg_f = jnp.where(lower, col_f - cs_row[head:head + 1, :], -1e30)
                seg_b = jnp.where(upper, col_b - cs_row[hb:hb + 1, :], -1e30)
                mix = cb * (jnp.exp(seg_f) * dtr[head:head + 1, :] + jnp.exp(seg_b) * dtr[hb:hb + 1, :])
                carry_in = c_dup * jnp.exp(jnp.where(low_half, col_f, col_b))
                hs = slice(head * SSM_HEAD_DIM, (head + 1) * SSM_HEAD_DIM)
                lhs = jnp.concatenate([mix.astype(BF), carry_in.astype(BF)], axis=1)
                rhs = jnp.concatenate([xs_b[:, hs], st_ref[c, head].astype(BF), st_ref[c, hb].astype(BF)], axis=0)
                y_ref[:, hs] = _dot(lhs, rhs)
        y = y_ref[...] + xs * dskip_ref[...]
        y = y * _silu(z_ref[pl.ds(base, q), :])
        ms = jnp.mean(y * y, axis=-1, keepdims=True)
        y_out[pl.ds(base, q), :] = (y * lax.rsqrt(ms + EPS) * gn_ref[...]).astype(BF)
        return carry

    lax.fori_loop(0, n_chunks, chunk_out, 0)


def _ssd_call(xbc, z, dt, dtt, h0, layer, lw, *, seq, n_seq, row_block0, emit_state, name):
    n_chunks = seq // CHUNK
    nh2 = 2 * SSM_HEADS
    full = lambda a: pl.BlockSpec(a.shape, lambda b: (0,) * a.ndim)
    params = [lw["conv_w"], lw["conv_b"], lw["alog_r"], lw["alog_c"], lw["dtb_r"], lw["dtb_c"],
              lw["dskip"], lw["gn"]]
    in_specs = [pl.BlockSpec((seq, CONV_DIM), lambda b: (row_block0 + b, 0)),
                pl.BlockSpec((seq, SSM_WIDTH), lambda b: (row_block0 + b, 0)),
                pl.BlockSpec((seq, nh2), lambda b: (row_block0 + b, 0)),
                pl.BlockSpec((nh2, seq), lambda b: (0, row_block0 + b))]
    args = [xbc, z, dt, dtt]
    if h0 is not None:
        in_specs.append(pl.BlockSpec((None, None, 2, SSM_HEADS, SSM_HEAD_DIM, SSM_STATE),
                                     lambda b: (b, layer, 0, 0, 0, 0)))
        args.append(h0)
    in_specs += [full(p) for p in params]
    args += params
    out_specs = [pl.BlockSpec((seq, SSM_WIDTH), lambda b: (b, 0))]
    out_shape = [jax.ShapeDtypeStruct((n_seq * seq, SSM_WIDTH), BF)]
    if emit_state:
        out_specs.append(pl.BlockSpec((None, 2, SSM_HEADS, SSM_HEAD_DIM, SSM_STATE),
                                      lambda b: (b, 0, 0, 0, 0)))
        out_shape.append(jax.ShapeDtypeStruct((n_seq, 2, SSM_HEADS, SSM_HEAD_DIM, SSM_STATE), F32))
    scratch = [
        pltpu.VMEM((seq + 2 * HALO, CONV_DIM), F32),
        pltpu.VMEM((seq, CONV_DIM), F32),
        pltpu.VMEM((seq, nh2), F32),
        pltpu.VMEM((nh2, seq), F32),
        pltpu.VMEM((nh2, seq), F32),
        pltpu.VMEM((n_chunks, 1, nh2), F32),
        pltpu.VMEM((n_chunks, nh2, SSM_HEAD_DIM, SSM_STATE), F32),
        pltpu.VMEM((nh2, SSM_HEAD_DIM, SSM_STATE), F32),
        pltpu.VMEM((CHUNK, SSM_WIDTH), F32),
    ]
    return pl.pallas_call(
        functools.partial(_ssd_kernel, n_chunks=n_chunks, has_h0=h0 is not None, emit_state=emit_state),
        grid=(n_seq,),
        in_specs=in_specs,
        out_specs=out_specs,
        out_shape=out_shape,
        scratch_shapes=scratch,
        compiler_params=_params(("parallel",)),
        name=name,
    )(*args)


def _ssd_mixer(xbc, z, dt, dtt, state_ssm, layer, lw, geom):
    y_ctx, h_fin = _ssd_call(xbc, z, dt, dtt, None, layer, lw, seq=geom.seq, n_seq=geom.batch,
                             row_block0=0, emit_state=True, name="ssd_context")
    (y_lat,) = _ssd_call(xbc, z, dt, dtt, state_ssm, layer, lw, seq=geom.dec_seq,
                         n_seq=geom.dec_batch, row_block0=geom.n_ctx // geom.dec_seq,
                         emit_state=False, name="ssd_latent")
    return y_ctx, y_lat, h_fin


def _outproj_kernel(ac_ref, al_ref, sc_ref, sl_ref, xc_ref, xl_ref, mod_ref, n2_ref, woa_ref, wos_ref,
                    wrh_ref, wrl_ref, x1_out, h2_out, afft_out, *, ctx_tiles):
    d = xc_ref.shape[-1]
    mod = mod_ref[...]
    g1 = mod[:, 2 * d:3 * d]
    sh2 = mod[:, 3 * d:4 * d]
    sc2 = mod[:, 4 * d:5 * d]
    is_ctx = pl.program_id(0) < ctx_tiles
    attn = jnp.where(is_ctx, ac_ref[...], al_ref[...])
    ssm = jnp.where(is_ctx, sc_ref[...], sl_ref[...])
    y = _dot(attn, woa_ref[...]) + _dot(ssm, wos_ref[...])
    x1 = jnp.where(is_ctx, xc_ref[...], xl_ref[...]) + g1 * y
    x1_out[...] = x1
    ms = jnp.mean(x1 * x1, axis=-1, keepdims=True)
    h2 = x1 * lax.rsqrt(ms + EPS) * n2_ref[...]
    h2 = h2 * (1.0 + sc2) + sh2
    h2_out[...] = h2
    hh = h2.astype(BF)
    hl = (h2 - hh.astype(F32)).astype(BF)
    wh = wrh_ref[...]
    logits = _dot_nt(wh, hh) + _dot_nt(wh, hl) + _dot_nt(wrl_ref[...], hh)
    logits = logits - logits.max(axis=0, keepdims=True)
    e = jnp.exp(logits)
    afft_out[...] = e / e.sum(axis=0, keepdims=True)


def _out_projection(attn_c, attn_l, ssm_c, ssm_l, x, mod3, lw, geom):
    xc, xl, xc_spec, xl_spec = _group_views(x, geom)
    t, d = geom.n_ctx + geom.n_lat, xc.shape[-1]
    tm = TOKEN_TILE
    nc = geom.ctx_tiles
    full = lambda a: pl.BlockSpec(a.shape, lambda i: (0,) * a.ndim)
    row = lambda w: pl.BlockSpec((tm, w), lambda i: (i, 0))
    ctx_row = lambda w: pl.BlockSpec((tm, w), lambda i: (jnp.minimum(i, nc - 1), 0))
    lat_row = lambda w: pl.BlockSpec((tm, w), lambda i: (jnp.maximum(i - nc, 0), 0))
    weights = [lw["n2"], lw["woa"], lw["wos"], lw["wrh"], lw["wrl"]]
    return pl.pallas_call(
        functools.partial(_outproj_kernel, ctx_tiles=nc),
        grid=(t // tm,),
        in_specs=[ctx_row(Q_DIM), lat_row(Q_DIM), ctx_row(SSM_WIDTH), lat_row(SSM_WIDTH), xc_spec, xl_spec,
                  pl.BlockSpec((None, 1, mod3.shape[-1]), lambda i: (geom.mod_row(i), 0, 0))]
                 + [full(w) for w in weights],
        out_specs=[row(d), row(d), pl.BlockSpec((N_EXPERTS, tm), lambda i: (0, i))],
        out_shape=[jax.ShapeDtypeStruct((t, d), F32), jax.ShapeDtypeStruct((t, d), F32),
                   jax.ShapeDtypeStruct((N_EXPERTS, t), F32)],
        compiler_params=_params(("parallel",)),
        name="out_projection",
    )(attn_c, attn_l, ssm_c, ssm_l, xc, xl, mod3, *weights)


def _flat_cumsum(m2, rows):
    gr = m2.shape[0]
    li = lax.broadcasted_iota(I32, (LANES, LANES), 0)
    lj = lax.broadcasted_iota(I32, (LANES, LANES), 1)
    mb = m2.astype(BF)
    within = _dot(mb, (li <= lj).astype(BF))
    row_tot = _dot(mb, jnp.ones((LANES, LANES), BF))
    ri = lax.broadcasted_iota(I32, (gr, gr), 0)
    rj = lax.broadcasted_iota(I32, (gr, gr), 1)
    before = ((ri // rows == rj // rows) & (rj < ri)).astype(BF)
    hi, lo = _split_base128(row_tot)
    return within + (LANES * _dot(before, hi) + _dot(before, lo))


def _split_base128(v):
    hi = jnp.floor(v * (1.0 / LANES))
    return hi.astype(BF), (v - hi * LANES).astype(BF)


def _plan_kernel(aff_ref, idx_out, gate_out, dslot_out, start_out, end_out, *, cap):
    a = aff_ref[...]
    n_e, rows, _ = a.shape

    def count(hit):
        return hit.astype(I32).sum(axis=1, keepdims=True).sum(axis=2, keepdims=True)

    def kth_largest_bits(values, member, k):
        def search(step, lo):
            cand = lo | (1 << (30 - step))
            return jnp.where(count(member & (values >= lax.bitcast_convert_type(cand, F32))) >= k, cand, lo)
        return lax.fori_loop(0, 31, search, jnp.zeros((n_e, 1, 1), I32))

    everyone = jnp.full(a.shape, True)
    thr_bits = kth_largest_bits(a, everyone, cap)
    thr = lax.bitcast_convert_type(thr_bits, F32)
    above = a >= lax.bitcast_convert_type(thr_bits + 1, F32)
    cell = (a >= thr) & jnp.logical_not(above)
    need = cap - count(above)
    resid = jnp.where(cell, a - thr, 0.0)
    thr2 = lax.bitcast_convert_type(kth_largest_bits(resid, cell, need), F32)
    gt = above | (cell & (resid > thr2))
    eq = cell & (resid == thr2)
    need = cap - count(gt)
    eqf = eq.astype(F32).reshape(n_e * rows, LANES)
    rank = (_flat_cumsum(eqf, rows) - eqf).reshape(n_e, rows, LANES)
    mask = gt | (eq & (rank < need.astype(F32)))
    mf = mask.astype(F32)

    c3 = _flat_cumsum(mf.reshape(n_e * rows, LANES), rows).reshape(n_e, rows, LANES)
    ahead = []
    run = jnp.zeros((rows, LANES), F32)
    for e in range(n_e):
        ahead.append(run)
        run = run + mf[e]
    cnt = run
    start = _flat_cumsum(cnt, rows) - cnt
    set_id = pl.program_id(0)
    row0 = (set_id * (n_e * cap)).astype(F32)
    start_out[...] = start + row0
    end_out[...] = start + cnt + row0

    s_lane = lax.broadcasted_iota(I32, (1, cap), 1).astype(F32)
    r_col = lax.broadcasted_iota(I32, (LANES, cap), 0).astype(F32)
    pad = jnp.zeros((LANES - rows, LANES), F32)

    def lanes_first(xm):
        return (jnp.concatenate([xm, pad], axis=0) if rows < LANES else xm).T

    for e in range(n_e):
        c_e = c3[e]
        row_of = (c_e[:, LANES - 1:LANES] <= s_lane).astype(F32).sum(axis=0, keepdims=True)
        onehot = (r_col == row_of).astype(BF)
        a_t = lanes_first(a[e])
        a1 = a_t.astype(BF)
        r1 = a_t - a1.astype(F32)
        a2 = r1.astype(BF)
        a3 = (r1 - a2.astype(F32)).astype(BF)
        pieces = _split_base128(lanes_first(c_e)) + (a1, a2, a3) + _split_base128(lanes_first(start + ahead[e]))
        sel = _dot(jnp.concatenate(pieces, axis=0), onehot)
        part = [sel[j * LANES:(j + 1) * LANES] for j in range(len(pieces))]
        sel_c = LANES * part[0] + part[1]
        sel_a = part[2] + (part[3] + part[4])
        sel_d = LANES * part[5] + part[6]
        lane_of = (sel_c <= s_lane).astype(F32).sum(axis=0, keepdims=True)
        pick = r_col == lane_of
        idx_out[e:e + 1, :] = (row_of * LANES + lane_of).astype(I32) + set_id * (rows * LANES)
        gate_out[e:e + 1, :] = jnp.where(pick, sel_a, 0.0).sum(axis=0, keepdims=True)
        dslot_out[e:e + 1, :] = (jnp.where(pick, sel_d, 0.0).sum(axis=0, keepdims=True).astype(I32)
                                 + set_id * (n_e * cap))


def _plan(aff_t, n_sets, cap):
    n_e, t = aff_t.shape
    n = t // n_sets
    rows = n // LANES
    assert rows <= LANES
    aff4 = aff_t.reshape(n_e, n_sets, rows, LANES)
    slot_spec = pl.BlockSpec((n_e, cap), lambda s: (0, s))
    tok_spec = pl.BlockSpec((None, rows, LANES), lambda s: (s, 0, 0))
    slot_shape = lambda dt: jax.ShapeDtypeStruct((n_e, n_sets * cap), dt)
    tok_shape = jax.ShapeDtypeStruct((n_sets, rows, LANES), F32)
    return pl.pallas_call(
        functools.partial(_plan_kernel, cap=cap),
        grid=(n_sets,),
        in_specs=[pl.BlockSpec((n_e, None, rows, LANES), lambda s: (0, s, 0, 0))],
        out_specs=[slot_spec, slot_spec, slot_spec, tok_spec, tok_spec],
        out_shape=[slot_shape(I32), slot_shape(F32), slot_shape(I32), tok_shape, tok_shape],
        compiler_params=_params(("parallel",)),
        name="expert_choice_plan",
    )(aff4)


DMA_UNROLL = 8
ROW_BLOCK = 512
EXPERT_FF_TILE = 512
COMBINE_ROWS = 256
COMBINE_BUFFERS = 4


def _expert_kernel(gidx_ref, dest_ref, h2p_hbm, gate_ref, wg_ref, wu_ref, wd_ref, z_hbm,
                   xw_a, xw_b, xb_ref, acc_a, acc_b, wgb_ref, wub_ref, wdb_ref, gcol_ref, sem_g, sem_s,
                   *, n_f, row_block):
    e = pl.program_id(0)
    f = pl.program_id(1)
    n_e = pl.num_programs(0)
    m_rows = xw_a.shape[0]
    n_blocks = m_rows // row_block
    per_block = m_rows // (n_f * n_blocks)

    def vmem_row(buf, m8, k):
        tile = pl.multiple_of(m8 + (k // SUBLANES) * SUBLANES, SUBLANES)
        return buf.at[pl.ds(tile, SUBLANES), :].at[pl.ds(k % SUBLANES, 1), :]

    def gather_copy(entry, m8, k, xw):
        return pltpu.make_async_copy(h2p_hbm.at[pl.ds(gidx_ref[entry], 1), :], vmem_row(xw, m8, k), sem_g)

    def scatter_copy(entry, m8, k, acc):
        return pltpu.make_async_copy(vmem_row(acc, m8, k), z_hbm.at[pl.ds(dest_ref[entry], 1), :], sem_s)

    def wait_gathers(xw):
        pltpu.make_async_copy(h2p_hbm.at[pl.ds(0, m_rows), :], xw, sem_g).wait()

    def wait_scatters(acc):
        pltpu.make_async_copy(acc, z_hbm.at[pl.ds(0, m_rows), :], sem_s).wait()

    def issue_all(make):
        def body(j, carry):
            for k in range(DMA_UNROLL):
                make(j * DMA_UNROLL, k).start()
            return carry
        lax.fori_loop(0, m_rows // DMA_UNROLL, body, 0)

    def run(xw_cur, xw_next, acc_cur, acc_prev):
        @pl.when((e == 0) & (f == 0))
        def _():
            acc_prev[...] = jnp.zeros_like(acc_prev)
            issue_all(lambda m8, k: gather_copy(m8 + k, m8, k, xw_cur))

        @pl.when(f == 0)
        def _():
            wait_gathers(xw_cur)
            xb_ref[...] = xw_cur[...].astype(BF)
            acc_cur[...] = jnp.zeros_like(acc_cur)
            eye = (lax.broadcasted_iota(I32, (LANES, LANES), 0)
                   == lax.broadcasted_iota(I32, (LANES, LANES), 1)).astype(BF)
            gr = gate_ref[...]
            g1 = gr.astype(BF)
            r1 = gr - g1.astype(F32)
            g2 = r1.astype(BF)
            g3 = (r1 - g2.astype(F32)).astype(BF)
            cols = _dot_nt(eye, g1) + (_dot_nt(eye, g2) + _dot_nt(eye, g3))
            for j in range(gr.shape[0]):
                gcol_ref[j * LANES:(j + 1) * LANES, :] = cols[:, j:j + 1]

        wgb_ref[...] = wg_ref[...].astype(BF)
        wub_ref[...] = wu_ref[...].astype(BF)
        wdb_ref[...] = wd_ref[...].astype(BF)
        last = f == n_f - 1

        def block(rb, carry):
            r0 = pl.multiple_of(rb * row_block, row_block)
            m0 = (f * n_blocks + rb) * per_block

            for k in range(per_block):
                gather_copy((e + 1) * m_rows + m0 + k, m0, k, xw_next).start()
                scatter_copy(e * m_rows + m0 + k, m0, k, acc_prev).start()
            x = xb_ref[pl.ds(r0, row_block), :]
            g = _dot(x, wgb_ref[...])
            u = _dot(x, wub_ref[...])
            hid = (_silu(g) * u).astype(BF)
            total = acc_cur[pl.ds(r0, row_block), :] + _dot(hid, wdb_ref[...])
            gate = jnp.where(last, gcol_ref[pl.ds(r0, row_block), :], 1.0)
            acc_cur[pl.ds(r0, row_block), :] = total * gate
            return carry

        lax.fori_loop(0, n_blocks, block, 0)

        @pl.when(last)
        def _():
            wait_scatters(acc_prev)

        @pl.when(last & (e == n_e - 1))
        def _():
            wait_gathers(xw_next)
            issue_all(lambda m8, k: scatter_copy((e + 1) * m_rows + m8 + k, m8, k, acc_cur))
            wait_scatters(acc_cur)

    @pl.when(e % 2 == 0)
    def _():
        run(xw_a, xw_b, acc_a, acc_b)

    @pl.when(e % 2 == 1)
    def _():
        run(xw_b, xw_a, acc_b, acc_a)


def _expert_ffn(gidx, dest, h2p, gates, w_gate, w_up, w_down, layer):
    n_e, m = gates.shape
    gates = gates.reshape(n_e, m // LANES, LANES)
    d = h2p.shape[1]
    ff = w_gate.shape[-1]
    tf = min(EXPERT_FF_TILE, ff)
    n_f = ff // tf
    row_block = min(ROW_BLOCK, m)
    assert m % DMA_UNROLL == 0 and (n_e * m) % COMBINE_ROWS == 0 and m % (n_f * (m // row_block)) == 0
    gidx_ext = jnp.concatenate([gidx, jnp.arange(m, dtype=I32)])
    dest_ext = jnp.concatenate([n_e * m + jnp.arange(m, dtype=I32), dest])
    return pl.pallas_call(
        functools.partial(_expert_kernel, n_f=n_f, row_block=row_block),
        grid_spec=pltpu.PrefetchScalarGridSpec(
            num_scalar_prefetch=2,
            grid=(n_e, n_f),
            in_specs=[pl.BlockSpec(memory_space=pl.ANY),
                      pl.BlockSpec((None, m // LANES, LANES), lambda e, f, gi, de: (e, 0, 0)),
                      pl.BlockSpec((None, None, d, tf), lambda e, f, gi, de: (layer, e, 0, f)),
                      pl.BlockSpec((None, None, d, tf), lambda e, f, gi, de: (layer, e, 0, f)),
                      pl.BlockSpec((None, None, tf, d), lambda e, f, gi, de: (layer, e, f, 0))],
            out_specs=pl.BlockSpec(memory_space=pl.ANY),
            scratch_shapes=[pltpu.VMEM((m, d), F32), pltpu.VMEM((m, d), F32),
                            pltpu.VMEM((m, d), BF), pltpu.VMEM((m, d), F32), pltpu.VMEM((m, d), F32),
                            pltpu.VMEM((d, tf), BF), pltpu.VMEM((d, tf), BF), pltpu.VMEM((tf, d), BF),
                            pltpu.VMEM((m, 1), F32),
                            pltpu.SemaphoreType.DMA, pltpu.SemaphoreType.DMA]),
        out_shape=jax.ShapeDtypeStruct(((n_e + 1) * m, d), F32),
        compiler_params=_params(("arbitrary", "arbitrary")),
        name="expert_swiglu",
    )(gidx_ext, dest_ext, h2p, gates, w_gate, w_up, w_down)


def _combine_kernel(lo_ref, hi_ref, z_hbm, start_ref, end_ref, x1_ref, mod_ref, oc_ref, ol_ref, zbuf, acc_ref,
                    col_ref, sem, state, *, n_chunks, ctx_tiles):
    i = pl.program_id(0)
    d = x1_ref.shape[-1]
    n_buf, ch, _ = zbuf.shape
    lo = lo_ref[i]
    hi = hi_ref[i]

    def chunk_copy(kc):
        slot = kc % n_buf
        return pltpu.make_async_copy(z_hbm.at[pl.ds(pl.multiple_of(kc * ch, ch), ch), :], zbuf.at[slot],
                                     sem.at[slot])

    @pl.when(i == 0)
    def _():
        state[0] = 0
        state[1] = 0

    acc_ref[...] = jnp.zeros_like(acc_ref)
    tm = acc_ref.shape[0]
    blocks = tm // LANES
    rows = jnp.concatenate([start_ref[i * blocks + j] for j in range(blocks)]
                           + [end_ref[i * blocks + j] for j in range(blocks)], axis=0)
    eye = (lax.broadcasted_iota(I32, (LANES, LANES), 0) == lax.broadcasted_iota(I32, (LANES, LANES), 1)).astype(BF)
    upper = jnp.floor(rows * (1.0 / 256.0))
    cols = 256.0 * _dot_nt(eye, upper.astype(BF)) + _dot_nt(eye, (rows - 256.0 * upper).astype(BF))
    for j in range(blocks):
        col_ref[j * LANES:(j + 1) * LANES, 0:1] = cols[:, j:j + 1]
        col_ref[j * LANES:(j + 1) * LANES, 1:2] = cols[:, blocks + j:blocks + j + 1]
    start = col_ref[:, 0:1]
    end = col_ref[:, 1:2]

    def body(kc, carry):
        upto = jnp.minimum(kc + n_buf, n_chunks)

        def start_one(j, c):
            chunk_copy(j).start()
            return c

        lax.fori_loop(state[0], upto, start_one, 0)
        state[0] = jnp.maximum(state[0], upto)

        @pl.when(kc >= state[1])
        def _():
            chunk_copy(kc).wait()
            state[1] = kc + 1

        slot = kc % n_buf
        k = (kc * ch + lax.broadcasted_iota(I32, (1, ch), 1)).astype(F32)
        own = jnp.where((k >= start) & (k < end), 1.0, 0.0).astype(BF)
        acc_ref[...] += _dot(own, zbuf[slot].astype(BF))
        return carry

    lax.fori_loop(lo, hi, body, 0)
    g2 = mod_ref[...][:, 5 * d:6 * d]
    out = x1_ref[...] + g2 * acc_ref[...]

    @pl.when(i < ctx_tiles)
    def _():
        oc_ref[...] = out

    @pl.when(i >= ctx_tiles)
    def _():
        ol_ref[...] = out


def _combine(z, n_rows, start, end, tile_lo, tile_hi, x1, mod3, geom):
    t, d = x1.shape
    tm = TOKEN_TILE
    nc = geom.ctx_tiles
    whole = lambda a: pl.BlockSpec(a.shape, lambda i, lo, hi: (0,) * a.ndim)
    return pl.pallas_call(
        functools.partial(_combine_kernel, n_chunks=n_rows // COMBINE_ROWS, ctx_tiles=nc),
        grid_spec=pltpu.PrefetchScalarGridSpec(
            num_scalar_prefetch=2,
            grid=(t // tm,),
            in_specs=[pl.BlockSpec(memory_space=pl.ANY), whole(start), whole(end),
                      pl.BlockSpec((tm, d), lambda i, lo, hi: (i, 0)),
                      pl.BlockSpec((None, 1, mod3.shape[-1]), lambda i, lo, hi: (geom.mod_row(i), 0, 0))],
            out_specs=[pl.BlockSpec((tm, d), lambda i, lo, hi: (jnp.minimum(i, nc - 1), 0)),
                       pl.BlockSpec((tm, d), lambda i, lo, hi: (jnp.maximum(i - nc, 0), 0))],
            scratch_shapes=[pltpu.VMEM((COMBINE_BUFFERS, COMBINE_ROWS, d), F32), pltpu.VMEM((tm, d), F32),
                            pltpu.VMEM((tm, 2), F32),
                            pltpu.SemaphoreType.DMA((COMBINE_BUFFERS,)), pltpu.SMEM((2,), I32)]),
        out_shape=[jax.ShapeDtypeStruct((geom.n_ctx, d), F32), jax.ShapeDtypeStruct((geom.n_lat, d), F32)],
        compiler_params=_params(("arbitrary",)),
        name="gated_combine",
    )(tile_lo, tile_hi, z, start, end, x1, mod3)


class _Geometry:
    def __init__(self, batch, seq, dec_batch, dec_seq):
        self.batch, self.seq, self.dec_batch, self.dec_seq = batch, seq, dec_batch, dec_seq
        self.n_ctx = batch * seq
        self.n_lat = dec_batch * dec_seq
        tm = TOKEN_TILE
        assert self.n_ctx % tm == 0 and dec_seq % tm == 0 and self.n_ctx % dec_seq == 0
        assert seq % CHUNK == 0 and dec_seq % CHUNK == 0 and dec_seq % QUERY_TILE == 0
        self.ctx_tiles = self.n_ctx // tm
        self.lat_tiles_per_seq = dec_seq // tm

    def mod_row(self, i):
        return jnp.where(i < self.ctx_tiles, 0, 1 + (i - self.ctx_tiles) // self.lat_tiles_per_seq)

    def rope_block(self, i):
        return jnp.where(i < self.ctx_tiles, 0, 1 + (i - self.ctx_tiles) % self.lat_tiles_per_seq)


def _rope_tables(dec_seq):
    t = jnp.arange(dec_seq)
    row = (t // GRID_W).astype(F32)
    col = (t % GRID_W).astype(F32)
    inv = ROPE_THETA ** (-jnp.arange(ROT_PAIRS, dtype=F32) / ROT_PAIRS)
    ang = jnp.concatenate([row[:, None] * inv, col[:, None] * inv], axis=-1)
    cos = jnp.tile(jnp.concatenate([jnp.cos(ang), jnp.cos(ang)], axis=-1), (1, N_HEADS))
    sin = jnp.tile(jnp.concatenate([-jnp.sin(ang), jnp.sin(ang)], axis=-1), (1, N_HEADS))
    ident = jnp.ones((TOKEN_TILE, Q_DIM), F32)
    return (jnp.concatenate([ident, cos], axis=0),
            jnp.concatenate([jnp.zeros_like(ident), sin], axis=0))


def _block_diag_ones(width):
    i = jnp.arange(width)
    return (i[:, None] // HEAD_DIM == i[None, :] // HEAD_DIM).astype(BF)


def _layer_weights(l, norm1, norm2, w_in, conv_w, conv_b, q_norm, k_norm, a_log, dt_bias, d_skip,
                   ssm_norm, w_out, w_router):
    w = w_in[l].astype(BF)
    o = 0
    parts = {}
    for name, width in (("wq", Q_DIM), ("wk", KV_DIM), ("wv", KV_DIM), ("wx", CONV_DIM),
                        ("wz", SSM_WIDTH), ("wdt", 2 * SSM_HEADS)):
        parts[name] = w[:, o:o + width]
        o += width
    wo = w_out[l].astype(BF)
    wr = w_router[l].T
    wrh = wr.astype(BF)
    nh2 = 2 * SSM_HEADS
    parts.update(
        wdtt=parts["wdt"].T,
        n1=norm1[l][None, :], n2=norm2[l][None, :],
        gq=jnp.tile(q_norm[l], N_HEADS)[None, :], gk=jnp.tile(k_norm[l], N_KV_HEADS)[None, :],
        conv_w=conv_w[l], conv_b=conv_b[l][None, :],
        alog_r=a_log[l].reshape(1, nh2), alog_c=a_log[l].reshape(nh2, 1),
        dtb_r=dt_bias[l].reshape(1, nh2), dtb_c=dt_bias[l].reshape(nh2, 1),
        dskip=jnp.repeat(d_skip[l], SSM_HEAD_DIM)[None, :], gn=ssm_norm[l][None, :],
        woa=wo[:Q_DIM], wos=wo[Q_DIM:],
        wrh=wrh, wrl=(wr - wrh.astype(F32)).astype(BF),
    )
    return parts


def _moe(h2p, aff_t, x1, mod3, w_gate, w_up, w_down, layer, n_sets, geom):
    t = x1.shape[0]
    n = t // n_sets
    cap = EC_FACTOR * n // N_EXPERTS
    pairs = N_EXPERTS * cap
    gidx, gates, dest, start, end = _plan(aff_t, n_sets, cap)
    z = _expert_ffn(gidx.reshape(-1), dest.reshape(-1), h2p, gates, w_gate, w_up, w_down, layer)
    tile_lo = start.reshape(t)[::TOKEN_TILE].astype(I32) // COMBINE_ROWS
    tile_hi = (end.reshape(t)[TOKEN_TILE - 1::TOKEN_TILE].astype(I32) + COMBINE_ROWS - 1) // COMBINE_ROWS
    lanes_major = lambda a: a.reshape(t // LANES, 1, LANES)
    return _combine(z, n_sets * pairs, lanes_major(start), lanes_major(end), tile_lo, tile_hi, x1, mod3, geom)


def kernel(x_prompt, x_sample, cache_k, cache_v, state_ssm, c, c_ctx, norm1, norm2, ada_w, ada_b,
           w_in, conv_w, conv_b, q_norm, k_norm, A_log, dt_bias, d_skip, ssm_norm, w_out,
           w_router, w_gate, w_up, w_down):
    batch, seq, d = x_prompt.shape
    dec_batch, dec_seq, _ = x_sample.shape
    depth = w_in.shape[0]
    past = cache_k.shape[2]
    geom = _Geometry(batch, seq, dec_batch, dec_seq)
    assert geom.n_ctx == geom.n_lat, "routed sets are stacked as two equal halves"

    rows = -(-(1 + dec_batch) // SUBLANES) * SUBLANES
    cv = jnp.zeros((rows, d), F32).at[0].set(c_ctx).at[1:1 + dec_batch].set(c)
    mod = _ada_modulation(cv, ada_w, ada_b)
    cos, sin = _rope_tables(dec_seq)
    consts = dict(cos=cos, sin=sin, bdq=_block_diag_ones(Q_DIM), bdk=_block_diag_ones(KV_DIM))
    cache_k4 = cache_k.reshape(dec_batch, depth, past, KV_DIM)
    cache_v4 = cache_v.reshape(dec_batch, depth, past, KV_DIM)

    x = (x_prompt.reshape(geom.n_ctx, d), x_sample.reshape(geom.n_lat, d))
    ks_new, vs_new, hs_new = [], [], []
    for l in range(depth):
        lw = _layer_weights(l, norm1, norm2, w_in, conv_w, conv_b, q_norm, k_norm, A_log, dt_bias,
                            d_skip, ssm_norm, w_out, w_router)
        mod3 = mod[l][:, None, :]
        q, k, v, xbc, z, dt, dtt = _in_projection(x, mod3, lw, consts, geom)
        attn_c, attn_l = _attention(q, k, v, cache_k4, cache_v4, l, geom)
        ssm_c, ssm_l, h_fin = _ssd_mixer(xbc, z, dt, dtt, state_ssm, l, lw, geom)
        x1, h2p, aff_t = _out_projection(attn_c, attn_l, ssm_c, ssm_l, x, mod3, lw, geom)
        x = _moe(h2p, aff_t, x1, mod3, w_gate, w_up, w_down, l, 2, geom)
        ks_new.append(k[:geom.n_ctx].reshape(batch, seq, N_KV_HEADS, HEAD_DIM))
        vs_new.append(v[:geom.n_ctx].reshape(batch, seq, N_KV_HEADS, HEAD_DIM))
        hs_new.append(h_fin)
    y_prompt = x[0].reshape(batch, seq, d)
    y_sample = x[1].reshape(dec_batch, dec_seq, d)
    return (y_prompt, y_sample, jnp.stack(ks_new, axis=1), jnp.stack(vs_new, axis=1),
            jnp.stack(hs_new, axis=1))
```

```python
import functools

import jax
import jax.numpy as jnp
from jax import lax
from jax.experimental import pallas as pl
from jax.experimental.pallas import tpu as pltpu

F32 = jnp.float32
BF = jnp.bfloat16
I32 = jnp.int32

GRID_W = 64
N_HEADS = 8
N_KV_HEADS = 2
HEAD_DIM = 64
GQA = N_HEADS // N_KV_HEADS
ROPE_THETA = 10000.0
ROT_PAIRS = HEAD_DIM // 4
Q_DIM = N_HEADS * HEAD_DIM
KV_DIM = N_KV_HEADS * HEAD_DIM
SSM_WIDTH = 512
SSM_HEADS = 8
SSM_HEAD_DIM = 64
SSM_GROUPS = 2
SSM_STATE = 64
CONV_W = 5
CONV_PAD = CONV_W // 2
CHUNK = 128
BC_DIM = SSM_GROUPS * SSM_STATE
CONV_DIM = SSM_WIDTH + 2 * BC_DIM
N_EXPERTS = 16
EC_FACTOR = 2
EPS = 1e-6
SCALE = HEAD_DIM ** -0.5
LOG2_E = 1.4426950408889634

LANES = 128
SUBLANES = 8
TOKEN_TILE = 512
QUERY_TILE = 256
HALO = SUBLANES
VMEM_LIMIT = 56 * 1024 * 1024

NT_DIMS = (((1,), (1,)), ((), ()))


def _dot(a, b):
    return jnp.dot(a, b, preferred_element_type=F32)


def _dot_nt(a, b):
    return lax.dot_general(a, b, NT_DIMS, preferred_element_type=F32)


def _dot_exact(a, b):
    return jnp.dot(a, b, preferred_element_type=F32, precision=lax.Precision.HIGHEST)


def _silu(x):
    return x / (1.0 + jnp.exp(-x))


def _softplus(x):
    return jnp.maximum(x, 0.0) + jnp.log(1.0 + jnp.exp(-jnp.abs(x)))


def _params(sem):
    return pltpu.CompilerParams(dimension_semantics=sem, vmem_limit_bytes=VMEM_LIMIT)


def _ada_kernel(cv_ref, w_ref, b_ref, o_ref):
    s = _silu(cv_ref[...]).astype(BF)
    o_ref[...] = _dot(s, w_ref[...].astype(BF)) + b_ref[...]


def _ada_modulation(cv, ada_w, ada_b):
    depth, d, n6 = ada_w.shape
    rows = cv.shape[0]
    tn = n6 // 4
    return pl.pallas_call(
        _ada_kernel,
        grid=(depth, n6 // tn),
        in_specs=[
            pl.BlockSpec((rows, d), lambda l, j: (0, 0)),
            pl.BlockSpec((None, d, tn), lambda l, j: (l, 0, j)),
            pl.BlockSpec((None, 1, tn), lambda l, j: (l, 0, j)),
        ],
        out_specs=pl.BlockSpec((None, rows, tn), lambda l, j: (l, 0, j)),
        out_shape=jax.ShapeDtypeStruct((depth, rows, n6), F32),
        compiler_params=_params(("parallel", "parallel")),
        name="ada_modulation",
    )(cv, ada_w, ada_b.reshape(depth, 1, n6))


def _inproj_kernel(xc_ref, xl_ref, mod_ref, n1_ref, wq_ref, wk_ref, wv_ref, wx_ref, wz_ref, wdt_ref, wdtt_ref,
                   bdq_ref, bdk_ref, gq_ref, gk_ref, cos_ref, sin_ref,
                   q_out, k_out, v_out, xbc_out, z_out, dt_out, dtt_out, *, ctx_tiles):
    d = xc_ref.shape[-1]
    x = jnp.where(pl.program_id(0) < ctx_tiles, xc_ref[...], xl_ref[...])
    mod = mod_ref[...]
    sh1 = mod[:, 0:d]
    sc1 = mod[:, d:2 * d]
    ms = jnp.mean(x * x, axis=-1, keepdims=True)
    h = x * lax.rsqrt(ms + EPS) * n1_ref[...]
    hb = (h * (1.0 + sc1) + sh1).astype(BF)
    cos = cos_ref[...]
    sin = sin_ref[...]

    def norm_rope(t, bd_ref, g_ref, cos_t, sin_t):
        ss = _dot((t * t).astype(BF), bd_ref[...])
        tn = t * lax.rsqrt(ss * (1.0 / HEAD_DIM) + EPS) * g_ref[...]
        width = t.shape[-1]
        lane = lax.broadcasted_iota(I32, tn.shape, 1)
        first_half = (lane % HEAD_DIM) < (HEAD_DIM // 2)
        partner = jnp.where(first_half,
                            pltpu.roll(tn, width - HEAD_DIM // 2, 1),
                            pltpu.roll(tn, HEAD_DIM // 2, 1))
        return tn * cos_t + partner * sin_t

    q = norm_rope(_dot(hb, wq_ref[...]), bdq_ref, gq_ref, cos, sin)
    q_out[...] = (q * (SCALE * LOG2_E)).astype(BF)
    k_out[...] = norm_rope(_dot(hb, wk_ref[...]), bdk_ref, gk_ref, cos[:, :KV_DIM], sin[:, :KV_DIM])
    v_out[...] = _dot(hb, wv_ref[...])
    xbc_out[...] = _dot(hb, wx_ref[...])
    z_out[...] = _dot(hb, wz_ref[...])
    dt_out[...] = _dot(hb, wdt_ref[...])
    dtt_out[...] = _dot_nt(wdtt_ref[...], hb)


def _group_views(x, geom):
    tm = TOKEN_TILE
    nc = geom.ctx_tiles
    if isinstance(x, tuple):
        xc, xl = x
        lat_index = lambda i: (jnp.maximum(i - nc, 0), 0)
    else:
        xc = xl = x
        lat_index = lambda i: (jnp.maximum(i, nc), 0)
    d = xc.shape[-1]
    return xc, xl, pl.BlockSpec((tm, d), lambda i: (jnp.minimum(i, nc - 1), 0)), pl.BlockSpec((tm, d), lat_index)


def _in_projection(x, mod3, lw, consts, geom):
    xc, xl, xc_spec, xl_spec = _group_views(x, geom)
    t, d = geom.n_ctx + geom.n_lat, xc.shape[-1]
    tm = TOKEN_TILE
    full = lambda a: pl.BlockSpec(a.shape, lambda i: (0,) * a.ndim)
    row = lambda w: pl.BlockSpec((tm, w), lambda i: (i, 0))
    weights = [lw["wq"], lw["wk"], lw["wv"], lw["wx"], lw["wz"], lw["wdt"], lw["wdtt"],
               consts["bdq"], consts["bdk"], lw["gq"], lw["gk"]]
    out_shape = [
        jax.ShapeDtypeStruct((t, Q_DIM), BF),
        jax.ShapeDtypeStruct((t, KV_DIM), F32),
        jax.ShapeDtypeStruct((t, KV_DIM), F32),
        jax.ShapeDtypeStruct((t, CONV_DIM), F32),
        jax.ShapeDtypeStruct((t, SSM_WIDTH), F32),
        jax.ShapeDtypeStruct((t, 2 * SSM_HEADS), F32),
        jax.ShapeDtypeStruct((2 * SSM_HEADS, t), F32),
    ]
    return pl.pallas_call(
        functools.partial(_inproj_kernel, ctx_tiles=geom.ctx_tiles),
        grid=(t // tm,),
        in_specs=[xc_spec, xl_spec,
                  pl.BlockSpec((None, 1, mod3.shape[-1]), lambda i: (geom.mod_row(i), 0, 0)),
                  full(lw["n1"])]
                 + [full(w) for w in weights]
                 + [pl.BlockSpec((tm, Q_DIM), lambda i: (geom.rope_block(i), 0)),
                    pl.BlockSpec((tm, Q_DIM), lambda i: (geom.rope_block(i), 0))],
        out_specs=[row(Q_DIM), row(KV_DIM), row(KV_DIM), row(CONV_DIM), row(SSM_WIDTH),
                   row(2 * SSM_HEADS), pl.BlockSpec((2 * SSM_HEADS, tm), lambda i: (0, i))],
        out_shape=out_shape,
        compiler_params=_params(("parallel",)),
        name="in_projection",
    )(xc, xl, mod3, lw["n1"], *weights, consts["cos"], consts["sin"])


def _head_pair_operands(k, v):
    lane = lax.broadcasted_iota(I32, k.shape, 1)
    low = lane < HEAD_DIM
    k_sw = pltpu.roll(k, HEAD_DIM, 1)
    v_sw = pltpu.roll(v, HEAD_DIM, 1)
    ones_last = jnp.where(lane == LANES - 1, 1.0, 0.0)
    ones_first = jnp.where(lane == 0, 1.0, 0.0)
    kk = jnp.stack([jnp.where(low, k, k_sw), jnp.where(low, k_sw, k)]).astype(BF)
    v_lo = jnp.stack([jnp.where(low, v, ones_last), jnp.where(low, v_sw, ones_last)]).astype(BF)
    v_hi = jnp.stack([jnp.where(low, ones_first, v_sw), jnp.where(low, ones_first, v)]).astype(BF)
    return kk, v_lo, v_hi


def _attend(q_ref, o_ref, key_sets):
    tq = q_ref.shape[0]
    lane = lax.broadcasted_iota(I32, (tq, LANES), 1)
    low = lane < HEAD_DIM
    for pair in range(N_HEADS // 2):
        g = (2 * pair) // GQA
        q2 = q_ref[:, pair * LANES:(pair + 1) * LANES]
        zero = jnp.zeros_like(q2)
        accs = []
        for half, qh in enumerate((jnp.where(low, q2, zero), jnp.where(low, zero, q2))):
            scores = [_dot_nt(qh, ks[0][g]) for ks in key_sets]
            m = scores[0].max(axis=-1, keepdims=True)
            for s in scores[1:]:
                m = jnp.maximum(m, s.max(axis=-1, keepdims=True))
            acc = jnp.zeros((tq, LANES), F32)
            for s, ks in zip(scores, key_sets):
                acc = acc + _dot(jnp.exp2(s - m).astype(BF), ks[1 + half][g])
            accs.append(acc)
        out = jnp.where(low, accs[0] * (1.0 / accs[0][:, LANES - 1:LANES]), accs[1] * (1.0 / accs[1][:, 0:1]))
        o_ref[:, pair * LANES:(pair + 1) * LANES] = out.astype(BF)


def _attn_ctx_kernel(q_ref, k_ref, v_ref, o_ref):
    _attend(q_ref, o_ref, [_head_pair_operands(k_ref[...], v_ref[...])])


def _attn_lat_kernel(q_ref, kc_ref, vc_ref, kl_ref, vl_ref, o_ref, *ops):
    @pl.when(pl.program_id(1) == 0)
    def _():
        for dst, src in zip(ops, _head_pair_operands(kc_ref[...], vc_ref[...])
                            + _head_pair_operands(kl_ref[...], vl_ref[...])):
            dst[...] = src

    _attend(q_ref, o_ref, [ops[0:3], ops[3:6]])


def _attention(q, k, v, cache_k4, cache_v4, layer, geom):
    seq = geom.seq
    o_ctx = pl.pallas_call(
        _attn_ctx_kernel,
        grid=(geom.batch,),
        in_specs=[pl.BlockSpec((seq, Q_DIM), lambda b: (b, 0)),
                  pl.BlockSpec((seq, KV_DIM), lambda b: (b, 0)),
                  pl.BlockSpec((seq, KV_DIM), lambda b: (b, 0))],
        out_specs=pl.BlockSpec((seq, Q_DIM), lambda b: (b, 0)),
        out_shape=jax.ShapeDtypeStruct((geom.n_ctx, Q_DIM), BF),
        compiler_params=_params(("parallel",)),
        name="attention_context",
    )(q, k, v)

    tq = QUERY_TILE
    dseq = geom.dec_seq
    past = cache_k4.shape[2]
    q_tiles = dseq // tq
    ctx_tiles = geom.n_ctx // tq
    ctx_seqs = geom.n_ctx // dseq
    o_lat = pl.pallas_call(
        _attn_lat_kernel,
        grid=(geom.dec_batch, q_tiles),
        in_specs=[pl.BlockSpec((tq, Q_DIM), lambda b, i: (ctx_tiles + b * q_tiles + i, 0)),
                  pl.BlockSpec((None, None, past, KV_DIM), lambda b, i: (b, layer, 0, 0)),
                  pl.BlockSpec((None, None, past, KV_DIM), lambda b, i: (b, layer, 0, 0)),
                  pl.BlockSpec((dseq, KV_DIM), lambda b, i: (ctx_seqs + b, 0)),
                  pl.BlockSpec((dseq, KV_DIM), lambda b, i: (ctx_seqs + b, 0))],
        out_specs=pl.BlockSpec((tq, Q_DIM), lambda b, i: (b * q_tiles + i, 0)),
        out_shape=jax.ShapeDtypeStruct((geom.n_lat, Q_DIM), BF),
        scratch_shapes=[pltpu.VMEM((N_KV_HEADS, s, KV_DIM), BF) for s in (past, past, past, dseq, dseq, dseq)],
        compiler_params=_params(("parallel", "arbitrary")),
        name="attention_latent",
    )(q, cache_k4, cache_v4, k, v)
    return o_ctx, o_lat


def _ssd_kernel(*refs, n_chunks, has_h0, emit_state):
    it = iter(refs)
    xbc_ref, z_ref, dt_ref, dtt_ref = next(it), next(it), next(it), next(it)
    h0_ref = next(it) if has_h0 else None
    (cw_ref, cb_ref, alog_r_ref, alog_c_ref, dtb_r_ref, dtb_c_ref,
     dskip_ref, gn_ref) = (next(it) for _ in range(8))
    y_out = next(it)
    hfin_out = next(it) if emit_state else None
    pad_ref, u_ref, csc_ref, csr_ref, dtr_ref, tot_ref, st_ref, hcur_ref, y_ref = (next(it) for _ in range(9))

    seq = xbc_ref.shape[0]
    nh = SSM_HEADS
    q = CHUNK

    pad_ref[0:HALO, :] = jnp.zeros((HALO, CONV_DIM), F32)
    pad_ref[HALO:HALO + seq, :] = xbc_ref[...]
    pad_ref[HALO + seq:2 * HALO + seq, :] = jnp.zeros((HALO, CONV_DIM), F32)

    a_row = -jnp.exp(alog_r_ref[...])
    a_col = -jnp.exp(alog_c_ref[...])
    ii = lax.broadcasted_iota(I32, (q, q), 0)
    jj = lax.broadcasted_iota(I32, (q, q), 1)
    lower = jj <= ii
    upper = jj >= ii
    low_half = jj < SSM_STATE
    tri_lower = lower.astype(F32)
    tri_upper = upper.astype(F32)
    fwd_lane = lax.broadcasted_iota(I32, (q, 2 * nh), 1) < nh
    fwd_row = lax.broadcasted_iota(I32, (2 * nh, q), 0) < nh

    def chunk_stats(c, carry):
        base = pl.multiple_of(c * q, q)
        win = pad_ref[pl.ds(base, q + 2 * HALO), :]
        acc = jnp.zeros((q, CONV_DIM), F32) + cb_ref[...]
        for tap in range(CONV_W):
            off = HALO - CONV_PAD + tap
            acc = acc + win[off:off + q, :] * cw_ref[tap:tap + 1, :]
        u = _silu(acc)
        u_ref[pl.ds(base, q), :] = u

        dtc = _softplus(dt_ref[pl.ds(base, q), :] + dtb_r_ref[...])
        dtr = _softplus(dtt_ref[:, pl.ds(base, q)] + dtb_c_ref[...])
        ac = dtc * a_row
        ar = dtr * a_col
        cs_col = jnp.where(fwd_lane, _dot_exact(tri_lower, ac), _dot_exact(tri_upper, ac))
        cs_row = jnp.where(fwd_row, _dot_exact(ar, tri_upper), _dot_exact(ar, tri_lower))
        csc_ref[pl.ds(base, q), :] = cs_col
        csr_ref[:, pl.ds(base, q)] = cs_row
        dtr_ref[:, pl.ds(base, q)] = dtr
        total = jnp.where(fwd_lane[0:1, :], cs_col[q - 1:q, :], cs_col[0:1, :])
        tot_ref[c] = total
        total_c = jnp.where(fwd_row[:, 0:1], cs_row[:, q - 1:q], cs_row[:, 0:1])
        w_end = jnp.exp(total_c - cs_row) * dtr

        xs_b = u[:, 0:SSM_WIDTH].astype(BF)
        b_t = u[:, SSM_WIDTH:SSM_WIDTH + BC_DIM].T
        for dh in range(2 * nh):
            head = dh % nh
            grp = head // (nh // SSM_GROUPS)
            bw = (b_t[grp * SSM_STATE:(grp + 1) * SSM_STATE, :] * w_end[dh:dh + 1, :]).astype(BF)
            st_ref[c, dh] = _dot(bw, xs_b[:, head * SSM_HEAD_DIM:(head + 1) * SSM_HEAD_DIM])
        return carry

    lax.fori_loop(0, n_chunks, chunk_stats, 0)

    eye = (lax.broadcasted_iota(I32, (SSM_STATE, SSM_STATE), 0)
           == lax.broadcasted_iota(I32, (SSM_STATE, SSM_STATE), 1)).astype(BF)

    def transposed_blocks(stack):
        p1 = stack.astype(BF)
        r1 = stack - p1.astype(F32)
        p2 = r1.astype(BF)
        p3 = (r1 - p2.astype(F32)).astype(BF)
        return _dot_nt(eye, p1) + (_dot_nt(eye, p2) + _dot_nt(eye, p3))

    if has_h0:
        h0_t = transposed_blocks(h0_ref[...].reshape(2 * nh * SSM_HEAD_DIM, SSM_STATE))
    for dh in range(2 * nh):
        hcur_ref[dh] = (h0_t[:, dh * SSM_HEAD_DIM:(dh + 1) * SSM_HEAD_DIM] if has_h0
                        else jnp.zeros((SSM_STATE, SSM_HEAD_DIM), F32))

    def recur(step, carry):
        for direction in range(2):
            c = step if direction == 0 else n_chunks - 1 - step
            decay = jnp.exp(tot_ref[c])
            for head in range(nh):
                dh = direction * nh + head
                new = st_ref[c, dh]
                cur = hcur_ref[dh]
                st_ref[c, dh] = cur
                hcur_ref[dh] = cur * decay[:, dh:dh + 1] + new
        return carry

    lax.fori_loop(0, n_chunks, recur, 0)
    if emit_state:
        h_t = transposed_blocks(hcur_ref[...].reshape(2 * nh * SSM_STATE, SSM_HEAD_DIM))
        for dh in range(2 * nh):
            hfin_out[dh // nh, dh % nh] = h_t[:, dh * SSM_STATE:(dh + 1) * SSM_STATE]

    def chunk_out(c, carry):
        base = pl.multiple_of(c * q, q)
        u = u_ref[pl.ds(base, q), :]
        xs = u[:, 0:SSM_WIDTH]
        xs_b = xs.astype(BF)
        bmat = u[:, SSM_WIDTH:SSM_WIDTH + BC_DIM].astype(BF)
        cmat = u[:, SSM_WIDTH + BC_DIM:CONV_DIM]
        cmat_b = cmat.astype(BF)
        cs_col = csc_ref[pl.ds(base, q), :]
        cs_row = csr_ref[:, pl.ds(base, q)]
        dtr = dtr_ref[:, pl.ds(base, q)]
        c_sw = pltpu.roll(cmat, SSM_STATE, 1)
        for grp in range(SSM_GROUPS):
            gs = slice(grp * SSM_STATE, (grp + 1) * SSM_STATE)
            cb = _dot_nt(cmat_b[:, gs], bmat[:, gs])
            c_dup = jnp.where(low_half, cmat, c_sw) if grp == 0 else jnp.where(low_half, c_sw, cmat)
            for head in range(grp * (nh // SSM_GROUPS), (grp + 1) * (nh // SSM_GROUPS)):
                hb = nh + head
                col_f = jnp.broadcast_to(cs_col[:, head:head + 1], (q, q))
                col_b = jnp.broadcast_to(cs_col[:, hb:hb + 1], (q, q))
                seg_f = jnp.where(lower, col_f - cs_row[head:head + 1, :], -1e30)
                seg_b = jnp.where(upper, col_b - cs_row[hb:hb + 1, :], -1e30)
                mix = cb * (jnp.exp(seg_f) * dtr[head:head + 1, :] + jnp.exp(seg_b) * dtr[hb:hb + 1, :])
                carry_in = c_dup * jnp.exp(jnp.where(low_half, col_f, col_b))
                hs = slice(head * SSM_HEAD_DIM, (head + 1) * SSM_HEAD_DIM)
                lhs = jnp.concatenate([mix.astype(BF), carry_in.astype(BF)], axis=1)
                rhs = jnp.concatenate([xs_b[:, hs], st_ref[c, head].astype(BF), st_ref[c, hb].astype(BF)], axis=0)
                y_ref[:, hs] = _dot(lhs, rhs)
        y = y_ref[...] + xs * dskip_ref[...]
        y = y * _silu(z_ref[pl.ds(base, q), :])
        ms = jnp.mean(y * y, axis=-1, keepdims=True)
        y_out[pl.ds(base, q), :] = (y * lax.rsqrt(ms + EPS) * gn_ref[...]).astype(BF)
        return carry

    lax.fori_loop(0, n_chunks, chunk_out, 0)


def _ssd_call(xbc, z, dt, dtt, h0, layer, lw, *, seq, n_seq, row_block0, emit_state, name):
    n_chunks = seq // CHUNK
    nh2 = 2 * SSM_HEADS
    full = lambda a: pl.BlockSpec(a.shape, lambda b: (0,) * a.ndim)
    params = [lw["conv_w"], lw["conv_b"], lw["alog_r"], lw["alog_c"], lw["dtb_r"], lw["dtb_c"],
              lw["dskip"], lw["gn"]]
    in_specs = [pl.BlockSpec((seq, CONV_DIM), lambda b: (row_block0 + b, 0)),
                pl.BlockSpec((seq, SSM_WIDTH), lambda b: (row_block0 + b, 0)),
                pl.BlockSpec((seq, nh2), lambda b: (row_block0 + b, 0)),
                pl.BlockSpec((nh2, seq), lambda b: (0, row_block0 + b))]
    args = [xbc, z, dt, dtt]
    if h0 is not None:
        in_specs.append(pl.BlockSpec((None, None, 2, SSM_HEADS, SSM_HEAD_DIM, SSM_STATE),
                                     lambda b: (b, layer, 0, 0, 0, 0)))
        args.append(h0)
    in_specs += [full(p) for p in params]
    args += params
    out_specs = [pl.BlockSpec((seq, SSM_WIDTH), lambda b: (b, 0))]
    out_shape = [jax.ShapeDtypeStruct((n_seq * seq, SSM_WIDTH), BF)]
    if emit_state:
        out_specs.append(pl.BlockSpec((None, 2, SSM_HEADS, SSM_HEAD_DIM, SSM_STATE),
                                      lambda b: (b, 0, 0, 0, 0)))
        out_shape.append(jax.ShapeDtypeStruct((n_seq, 2, SSM_HEADS, SSM_HEAD_DIM, SSM_STATE), F32))
    scratch = [
        pltpu.VMEM((seq + 2 * HALO, CONV_DIM), F32),
        pltpu.VMEM((seq, CONV_DIM), F32),
        pltpu.VMEM((seq, nh2), F32),
        pltpu.VMEM((nh2, seq), F32),
        pltpu.VMEM((nh2, seq), F32),
        pltpu.VMEM((n_chunks, 1, nh2), F32),
        pltpu.VMEM((n_chunks, nh2, SSM_HEAD_DIM, SSM_STATE), F32),
        pltpu.VMEM((nh2, SSM_HEAD_DIM, SSM_STATE), F32),
        pltpu.VMEM((CHUNK, SSM_WIDTH), F32),
    ]
    return pl.pallas_call(
        functools.partial(_ssd_kernel, n_chunks=n_chunks, has_h0=h0 is not None, emit_state=emit_state),
        grid=(n_seq,),
        in_specs=in_specs,
        out_specs=out_specs,
        out_shape=out_shape,
        scratch_shapes=scratch,
        compiler_params=_params(("parallel",)),
        name=name,
    )(*args)


def _ssd_mixer(xbc, z, dt, dtt, state_ssm, layer, lw, geom):
    y_ctx, h_fin = _ssd_call(xbc, z, dt, dtt, None, layer, lw, seq=geom.seq, n_seq=geom.batch,
                             row_block0=0, emit_state=True, name="ssd_context")
    (y_lat,) = _ssd_call(xbc, z, dt, dtt, state_ssm, layer, lw, seq=geom.dec_seq,
                         n_seq=geom.dec_batch, row_block0=geom.n_ctx // geom.dec_seq,
                         emit_state=False, name="ssd_latent")
    return y_ctx, y_lat, h_fin


def _outproj_kernel(ac_ref, al_ref, sc_ref, sl_ref, xc_ref, xl_ref, mod_ref, n2_ref, woa_ref, wos_ref,
                    wrh_ref, wrl_ref, x1_out, h2_out, afft_out, *, ctx_tiles):
    d = xc_ref.shape[-1]
    mod = mod_ref[...]
    g1 = mod[:, 2 * d:3 * d]
    sh2 = mod[:, 3 * d:4 * d]
    sc2 = mod[:, 4 * d:5 * d]
    is_ctx = pl.program_id(0) < ctx_tiles
    attn = jnp.where(is_ctx, ac_ref[...], al_ref[...])
    ssm = jnp.where(is_ctx, sc_ref[...], sl_ref[...])
    y = _dot(attn, woa_ref[...]) + _dot(ssm, wos_ref[...])
    x1 = jnp.where(is_ctx, xc_ref[...], xl_ref[...]) + g1 * y
    x1_out[...] = x1
    ms = jnp.mean(x1 * x1, axis=-1, keepdims=True)
    h2 = x1 * lax.rsqrt(ms + EPS) * n2_ref[...]
    h2 = h2 * (1.0 + sc2) + sh2
    h2_out[...] = h2
    hh = h2.astype(BF)
    hl = (h2 - hh.astype(F32)).astype(BF)
    wh = wrh_ref[...]
    logits = _dot_nt(wh, hh) + _dot_nt(wh, hl) + _dot_nt(wrl_ref[...], hh)
    logits = logits - logits.max(axis=0, keepdims=True)
    e = jnp.exp(logits)
    afft_out[...] = e / e.sum(axis=0, keepdims=True)


def _out_projection(attn_c, attn_l, ssm_c, ssm_l, x, mod3, lw, geom):
    xc, xl, xc_spec, xl_spec = _group_views(x, geom)
    t, d = geom.n_ctx + geom.n_lat, xc.shape[-1]
    tm = TOKEN_TILE
    nc = geom.ctx_tiles
    full = lambda a: pl.BlockSpec(a.shape, lambda i: (0,) * a.ndim)
    row = lambda w: pl.BlockSpec((tm, w), lambda i: (i, 0))
    ctx_row = lambda w: pl.BlockSpec((tm, w), lambda i: (jnp.minimum(i, nc - 1), 0))
    lat_row = lambda w: pl.BlockSpec((tm, w), lambda i: (jnp.maximum(i - nc, 0), 0))
    weights = [lw["n2"], lw["woa"], lw["wos"], lw["wrh"], lw["wrl"]]
    return pl.pallas_call(
        functools.partial(_outproj_kernel, ctx_tiles=nc),
        grid=(t // tm,),
        in_specs=[ctx_row(Q_DIM), lat_row(Q_DIM), ctx_row(SSM_WIDTH), lat_row(SSM_WIDTH), xc_spec, xl_spec,
                  pl.BlockSpec((None, 1, mod3.shape[-1]), lambda i: (geom.mod_row(i), 0, 0))]
                 + [full(w) for w in weights],
        out_specs=[row(d), row(d), pl.BlockSpec((N_EXPERTS, tm), lambda i: (0, i))],
        out_shape=[jax.ShapeDtypeStruct((t, d), F32), jax.ShapeDtypeStruct((t, d), F32),
                   jax.ShapeDtypeStruct((N_EXPERTS, t), F32)],
        compiler_params=_params(("parallel",)),
        name="out_projection",
    )(attn_c, attn_l, ssm_c, ssm_l, xc, xl, mod3, *weights)


def _flat_cumsum(m2, rows):
    gr = m2.shape[0]
    li = lax.broadcasted_iota(I32, (LANES, LANES), 0)
    lj = lax.broadcasted_iota(I32, (LANES, LANES), 1)
    mb = m2.astype(BF)
    within = _dot(mb, (li <= lj).astype(BF))
    row_tot = _dot(mb, jnp.ones((LANES, LANES), BF))
    ri = lax.broadcasted_iota(I32, (gr, gr), 0)
    rj = lax.broadcasted_iota(I32, (gr, gr), 1)
    before = ((ri // rows == rj // rows) & (rj < ri)).astype(BF)
    hi, lo = _split_base128(row_tot)
    return within + (LANES * _dot(before, hi) + _dot(before, lo))


def _split_base128(v):
    hi = jnp.floor(v * (1.0 / LANES))
    return hi.astype(BF), (v - hi * LANES).astype(BF)


def _plan_kernel(aff_ref, idx_out, gate_out, dslot_out, start_out, end_out, *, cap):
    a = aff_ref[...]
    n_e, rows, _ = a.shape

    def count(hit):
        return hit.astype(I32).sum(axis=1, keepdims=True).sum(axis=2, keepdims=True)

    def kth_largest_bits(values, member, k):
        def search(step, lo):
            cand = lo | (1 << (30 - step))
            return jnp.where(count(member & (values >= lax.bitcast_convert_type(cand, F32))) >= k, cand, lo)
        return lax.fori_loop(0, 31, search, jnp.zeros((n_e, 1, 1), I32))

    everyone = jnp.full(a.shape, True)
    thr_bits = kth_largest_bits(a, everyone, cap)
    thr = lax.bitcast_convert_type(thr_bits, F32)
    above = a >= lax.bitcast_convert_type(thr_bits + 1, F32)
    cell = (a >= thr) & jnp.logical_not(above)
    need = cap - count(above)
    resid = jnp.where(cell, a - thr, 0.0)
    thr2 = lax.bitcast_convert_type(kth_largest_bits(resid, cell, need), F32)
    gt = above | (cell & (resid > thr2))
    eq = cell & (resid == thr2)
    need = cap - count(gt)
    eqf = eq.astype(F32).reshape(n_e * rows, LANES)
    rank = (_flat_cumsum(eqf, rows) - eqf).reshape(n_e, rows, LANES)
    mask = gt | (eq & (rank < need.astype(F32)))
    mf = mask.astype(F32)

    c3 = _flat_cumsum(mf.reshape(n_e * rows, LANES), rows).reshape(n_e, rows, LANES)
    ahead = []
    run = jnp.zeros((rows, LANES), F32)
    for e in range(n_e):
        ahead.append(run)
        run = run + mf[e]
    cnt = run
    start = _flat_cumsum(cnt, rows) - cnt
    set_id = pl.program_id(0)
    row0 = (set_id * (n_e * cap)).astype(F32)
    start_out[...] = start + row0
    end_out[...] = start + cnt + row0

    s_lane = lax.broadcasted_iota(I32, (1, cap), 1).astype(F32)
    r_col = lax.broadcasted_iota(I32, (LANES, cap), 0).astype(F32)
    pad = jnp.zeros((LANES - rows, LANES), F32)

    def lanes_first(xm):
        return (jnp.concatenate([xm, pad], axis=0) if rows < LANES else xm).T

    for e in range(n_e):
        c_e = c3[e]
        row_of = (c_e[:, LANES - 1:LANES] <= s_lane).astype(F32).sum(axis=0, keepdims=True)
        onehot = (r_col == row_of).astype(BF)
        a_t = lanes_first(a[e])
        a1 = a_t.astype(BF)
        r1 = a_t - a1.astype(F32)
        a2 = r1.astype(BF)
        a3 = (r1 - a2.astype(F32)).astype(BF)
        pieces = _split_base128(lanes_first(c_e)) + (a1, a2, a3) + _split_base128(lanes_first(start + ahead[e]))
        sel = _dot(jnp.concatenate(pieces, axis=0), onehot)
        part = [sel[j * LANES:(j + 1) * LANES] for j in range(len(pieces))]
        sel_c = LANES * part[0] + part[1]
        sel_a = part[2] + (part[3] + part[4])
        sel_d = LANES * part[5] + part[6]
        lane_of = (sel_c <= s_lane).astype(F32).sum(axis=0, keepdims=True)
        pick = r_col == lane_of
        idx_out[e:e + 1, :] = (row_of * LANES + lane_of).astype(I32) + set_id * (rows * LANES)
        gate_out[e:e + 1, :] = jnp.where(pick, sel_a, 0.0).sum(axis=0, keepdims=True)
        dslot_out[e:e + 1, :] = (jnp.where(pick, sel_d, 0.0).sum(axis=0, keepdims=True).astype(I32)
                                 + set_id * (n_e * cap))


def _plan(aff_t, n_sets, cap):
    n_e, t = aff_t.shape
    n = t // n_sets
    rows = n // LANES
    assert rows <= LANES
    aff4 = aff_t.reshape(n_e, n_sets, rows, LANES)
    slot_spec = pl.BlockSpec((n_e, cap), lambda s: (0, s))
    tok_spec = pl.BlockSpec((None, rows, LANES), lambda s: (s, 0, 0))
    slot_shape = lambda dt: jax.ShapeDtypeStruct((n_e, n_sets * cap), dt)
    tok_shape = jax.ShapeDtypeStruct((n_sets, rows, LANES), F32)
    return pl.pallas_call(
        functools.partial(_plan_kernel, cap=cap),
        grid=(n_sets,),
        in_specs=[pl.BlockSpec((n_e, None, rows, LANES), lambda s: (0, s, 0, 0))],
        out_specs=[slot_spec, slot_spec, slot_spec, tok_spec, tok_spec],
        out_shape=[slot_shape(I32), slot_shape(F32), slot_shape(I32), tok_shape, tok_shape],
        compiler_params=_params(("parallel",)),
        name="expert_choice_plan",
    )(aff4)


DMA_UNROLL = 8
ROW_BLOCK = 512
EXPERT_FF_TILE = 512
COMBINE_ROWS = 256
COMBINE_BUFFERS = 4


def _expert_kernel(gidx_ref, dest_ref, h2p_hbm, gate_ref, wg_ref, wu_ref, wd_ref, z_hbm,
                   xw_a, xw_b, xb_ref, acc_a, acc_b, wgb_ref, wub_ref, wdb_ref, gcol_ref, sem_g, sem_s,
                   *, n_f, row_block):
    e = pl.program_id(0)
    f = pl.program_id(1)
    n_e = pl.num_programs(0)
    m_rows = xw_a.shape[0]
    n_blocks = m_rows // row_block
    per_block = m_rows // (n_f * n_blocks)

    def vmem_row(buf, m8, k):
        tile = pl.multiple_of(m8 + (k // SUBLANES) * SUBLANES, SUBLANES)
        return buf.at[pl.ds(tile, SUBLANES), :].at[pl.ds(k % SUBLANES, 1), :]

    def gather_copy(entry, m8, k, xw):
        return pltpu.make_async_copy(h2p_hbm.at[pl.ds(gidx_ref[entry], 1), :], vmem_row(xw, m8, k), sem_g)

    def scatter_copy(entry, m8, k, acc):
        return pltpu.make_async_copy(vmem_row(acc, m8, k), z_hbm.at[pl.ds(dest_ref[entry], 1), :], sem_s)

    def wait_gathers(xw):
        pltpu.make_async_copy(h2p_hbm.at[pl.ds(0, m_rows), :], xw, sem_g).wait()

    def wait_scatters(acc):
        pltpu.make_async_copy(acc, z_hbm.at[pl.ds(0, m_rows), :], sem_s).wait()

    def issue_all(make):
        def body(j, carry):
            for k in range(DMA_UNROLL):
                make(j * DMA_UNROLL, k).start()
            return carry
        lax.fori_loop(0, m_rows // DMA_UNROLL, body, 0)

    def run(xw_cur, xw_next, acc_cur, acc_prev):
        @pl.when((e == 0) & (f == 0))
        def _():
            acc_prev[...] = jnp.zeros_like(acc_prev)
            issue_all(lambda m8, k: gather_copy(m8 + k, m8, k, xw_cur))

        @pl.when(f == 0)
        def _():
            wait_gathers(xw_cur)
            xb_ref[...] = xw_cur[...].astype(BF)
            eye = (lax.broadcasted_iota(I32, (LANES, LANES), 0)
                   == lax.broadcasted_iota(I32, (LANES, LANES), 1)).astype(BF)
            gr = gate_ref[...]
            g1 = gr.astype(BF)
            r1 = gr - g1.astype(F32)
            g2 = r1.astype(BF)
            g3 = (r1 - g2.astype(F32)).astype(BF)
            cols = _dot_nt(eye, g1) + (_dot_nt(eye, g2) + _dot_nt(eye, g3))
            for j in range(gr.shape[0]):
                gcol_ref[j * LANES:(j + 1) * LANES, :] = cols[:, j:j + 1]

        wgb_ref[...] = wg_ref[...].astype(BF)
        wub_ref[...] = wu_ref[...].astype(BF)
        wdb_ref[...] = wd_ref[...].astype(BF)
        last = f == n_f - 1

        def block(rb, carry, *, is_first, is_last):
            r0 = pl.multiple_of(rb * row_block, row_block)
            m0 = (f * n_blocks + rb) * per_block

            for k in range(per_block):
                gather_copy((e + 1) * m_rows + m0 + k, m0, k, xw_next).start()
                scatter_copy(e * m_rows + m0 + k, m0, k, acc_prev).start()
            x = xb_ref[pl.ds(r0, row_block), :]
            g = _dot(x, wgb_ref[...])
            u = _dot(x, wub_ref[...])
            hid = (_silu(g) * u).astype(BF)
            total = _dot(hid, wdb_ref[...])
            if not is_first:
                total = acc_cur[pl.ds(r0, row_block), :] + total
            if is_last:
                total = total * gcol_ref[pl.ds(r0, row_block), :]
            acc_cur[pl.ds(r0, row_block), :] = total
            return carry

        for is_first, is_last in ((True, n_f == 1), (False, False), (False, True)):
            if (n_f == 1 and not is_first) or (n_f == 2 and not (is_first or is_last)):
                continue
            step = (f == 0) if is_first else (last if is_last else (f > 0) & jnp.logical_not(last))

            @pl.when(step)
            def _(is_first=is_first, is_last=is_last):
                lax.fori_loop(0, n_blocks, functools.partial(block, is_first=is_first, is_last=is_last), 0)

        @pl.when(last)
        def _():
            wait_scatters(acc_prev)

        @pl.when(last & (e == n_e - 1))
        def _():
            wait_gathers(xw_next)
            issue_all(lambda m8, k: scatter_copy((e + 1) * m_rows + m8 + k, m8, k, acc_cur))
            wait_scatters(acc_cur)

    @pl.when(e % 2 == 0)
    def _():
        run(xw_a, xw_b, acc_a, acc_b)

    @pl.when(e % 2 == 1)
    def _():
        run(xw_b, xw_a, acc_b, acc_a)


def _expert_ffn(gidx, dest, h2p, gates, w_gate, w_up, w_down, layer):
    n_e, m = gates.shape
    gates = gates.reshape(n_e, m // LANES, LANES)
    d = h2p.shape[1]
    ff = w_gate.shape[-1]
    tf = min(EXPERT_FF_TILE, ff)
    n_f = ff // tf
    row_block = min(ROW_BLOCK, m)
    assert m % DMA_UNROLL == 0 and (n_e * m) % COMBINE_ROWS == 0 and m % (n_f * (m // row_block)) == 0
    gidx_ext = jnp.concatenate([gidx, jnp.arange(m, dtype=I32)])
    dest_ext = jnp.concatenate([n_e * m + jnp.arange(m, dtype=I32), dest])
    return pl.pallas_call(
        functools.partial(_expert_kernel, n_f=n_f, row_block=row_block),
        grid_spec=pltpu.PrefetchScalarGridSpec(
            num_scalar_prefetch=2,
            grid=(n_e, n_f),
            in_specs=[pl.BlockSpec(memory_space=pl.ANY),
                      pl.BlockSpec((None, m // LANES, LANES), lambda e, f, gi, de: (e, 0, 0)),
                      pl.BlockSpec((None, None, d, tf), lambda e, f, gi, de: (layer, e, 0, f)),
                      pl.BlockSpec((None, None, d, tf), lambda e, f, gi, de: (layer, e, 0, f)),
                      pl.BlockSpec((None, None, tf, d), lambda e, f, gi, de: (layer, e, f, 0))],
            out_specs=pl.BlockSpec(memory_space=pl.ANY),
            scratch_shapes=[pltpu.VMEM((m, d), F32), pltpu.VMEM((m, d), F32),
                            pltpu.VMEM((m, d), BF), pltpu.VMEM((m, d), F32), pltpu.VMEM((m, d), F32),
                            pltpu.VMEM((d, tf), BF), pltpu.VMEM((d, tf), BF), pltpu.VMEM((tf, d), BF),
                            pltpu.VMEM((m, 1), F32),
                            pltpu.SemaphoreType.DMA, pltpu.SemaphoreType.DMA]),
        out_shape=jax.ShapeDtypeStruct(((n_e + 1) * m, d), F32),
        compiler_params=_params(("arbitrary", "arbitrary")),
        name="expert_swiglu",
    )(gidx_ext, dest_ext, h2p, gates, w_gate, w_up, w_down)


def _combine_kernel(lo_ref, hi_ref, z_hbm, start_ref, end_ref, x1_ref, mod_ref, oc_ref, ol_ref, zbuf, acc_ref,
                    col_ref, sem, state, *, n_chunks, ctx_tiles):
    i = pl.program_id(0)
    d = x1_ref.shape[-1]
    n_buf, ch, _ = zbuf.shape
    lo = lo_ref[i]
    hi = hi_ref[i]

    def chunk_copy(kc):
        slot = kc % n_buf
        return pltpu.make_async_copy(z_hbm.at[pl.ds(pl.multiple_of(kc * ch, ch), ch), :], zbuf.at[slot],
                                     sem.at[slot])

    @pl.when(i == 0)
    def _():
        state[0] = 0
        state[1] = 0

    acc_ref[...] = jnp.zeros_like(acc_ref)
    tm = acc_ref.shape[0]
    blocks = tm // LANES
    rows = jnp.concatenate([start_ref[i * blocks + j] for j in range(blocks)]
                           + [end_ref[i * blocks + j] for j in range(blocks)], axis=0)
    eye = (lax.broadcasted_iota(I32, (LANES, LANES), 0) == lax.broadcasted_iota(I32, (LANES, LANES), 1)).astype(BF)
    upper = jnp.floor(rows * (1.0 / 256.0))
    cols = 256.0 * _dot_nt(eye, upper.astype(BF)) + _dot_nt(eye, (rows - 256.0 * upper).astype(BF))
    for j in range(blocks):
        col_ref[j * LANES:(j + 1) * LANES, 0:1] = cols[:, j:j + 1]
        col_ref[j * LANES:(j + 1) * LANES, 1:2] = cols[:, blocks + j:blocks + j + 1]
    start = col_ref[:, 0:1]
    end = col_ref[:, 1:2]

    def body(kc, carry):
        upto = jnp.minimum(kc + n_buf, n_chunks)

        def start_one(j, c):
            chunk_copy(j).start()
            return c

        lax.fori_loop(state[0], upto, start_one, 0)
        state[0] = jnp.maximum(state[0], upto)

        @pl.when(kc >= state[1])
        def _():
            chunk_copy(kc).wait()
            state[1] = kc + 1

        slot = kc % n_buf
        k = (kc * ch + lax.broadcasted_iota(I32, (1, ch), 1)).astype(F32)
        own = jnp.where((k >= start) & (k < end), 1.0, 0.0).astype(BF)
        acc_ref[...] += _dot(own, zbuf[slot].astype(BF))
        return carry

    lax.fori_loop(lo, hi, body, 0)
    g2 = mod_ref[...][:, 5 * d:6 * d]
    out = x1_ref[...] + g2 * acc_ref[...]

    @pl.when(i < ctx_tiles)
    def _():
        oc_ref[...] = out

    @pl.when(i >= ctx_tiles)
    def _():
        ol_ref[...] = out


def _combine(z, n_rows, start, end, tile_lo, tile_hi, x1, mod3, geom):
    t, d = x1.shape
    tm = TOKEN_TILE
    nc = geom.ctx_tiles
    whole = lambda a: pl.BlockSpec(a.shape, lambda i, lo, hi: (0,) * a.ndim)
    return pl.pallas_call(
        functools.partial(_combine_kernel, n_chunks=n_rows // COMBINE_ROWS, ctx_tiles=nc),
        grid_spec=pltpu.PrefetchScalarGridSpec(
            num_scalar_prefetch=2,
            grid=(t // tm,),
            in_specs=[pl.BlockSpec(memory_space=pl.ANY), whole(start), whole(end),
                      pl.BlockSpec((tm, d), lambda i, lo, hi: (i, 0)),
                      pl.BlockSpec((None, 1, mod3.shape[-1]), lambda i, lo, hi: (geom.mod_row(i), 0, 0))],
            out_specs=[pl.BlockSpec((tm, d), lambda i, lo, hi: (jnp.minimum(i, nc - 1), 0)),
                       pl.BlockSpec((tm, d), lambda i, lo, hi: (jnp.maximum(i - nc, 0), 0))],
            scratch_shapes=[pltpu.VMEM((COMBINE_BUFFERS, COMBINE_ROWS, d), F32), pltpu.VMEM((tm, d), F32),
                            pltpu.VMEM((tm, 2), F32),
                            pltpu.SemaphoreType.DMA((COMBINE_BUFFERS,)), pltpu.SMEM((2,), I32)]),
        out_shape=[jax.ShapeDtypeStruct((geom.n_ctx, d), F32), jax.ShapeDtypeStruct((geom.n_lat, d), F32)],
        compiler_params=_params(("arbitrary",)),
        name="gated_combine",
    )(tile_lo, tile_hi, z, start, end, x1, mod3)


class _Geometry:
    def __init__(self, batch, seq, dec_batch, dec_seq):
        self.batch, self.seq, self.dec_batch, self.dec_seq = batch, seq, dec_batch, dec_seq
        self.n_ctx = batch * seq
        self.n_lat = dec_batch * dec_seq
        tm = TOKEN_TILE
        assert self.n_ctx % tm == 0 and dec_seq % tm == 0 and self.n_ctx % dec_seq == 0
        assert seq % CHUNK == 0 and dec_seq % CHUNK == 0 and dec_seq % QUERY_TILE == 0
        self.ctx_tiles = self.n_ctx // tm
        self.lat_tiles_per_seq = dec_seq // tm

    def mod_row(self, i):
        return jnp.where(i < self.ctx_tiles, 0, 1 + (i - self.ctx_tiles) // self.lat_tiles_per_seq)

    def rope_block(self, i):
        return jnp.where(i < self.ctx_tiles, 0, 1 + (i - self.ctx_tiles) % self.lat_tiles_per_seq)


def _rope_tables(dec_seq):
    t = jnp.arange(dec_seq)
    row = (t // GRID_W).astype(F32)
    col = (t % GRID_W).astype(F32)
    inv = ROPE_THETA ** (-jnp.arange(ROT_PAIRS, dtype=F32) / ROT_PAIRS)
    ang = jnp.concatenate([row[:, None] * inv, col[:, None] * inv], axis=-1)
    cos = jnp.tile(jnp.concatenate([jnp.cos(ang), jnp.cos(ang)], axis=-1), (1, N_HEADS))
    sin = jnp.tile(jnp.concatenate([-jnp.sin(ang), jnp.sin(ang)], axis=-1), (1, N_HEADS))
    ident = jnp.ones((TOKEN_TILE, Q_DIM), F32)
    return (jnp.concatenate([ident, cos], axis=0),
            jnp.concatenate([jnp.zeros_like(ident), sin], axis=0))


def _block_diag_ones(width):
    i = jnp.arange(width)
    return (i[:, None] // HEAD_DIM == i[None, :] // HEAD_DIM).astype(BF)


def _layer_weights(l, norm1, norm2, w_in, conv_w, conv_b, q_norm, k_norm, a_log, dt_bias, d_skip,
                   ssm_norm, w_out, w_router):
    w = w_in[l].astype(BF)
    o = 0
    parts = {}
    for name, width in (("wq", Q_DIM), ("wk", KV_DIM), ("wv", KV_DIM), ("wx", CONV_DIM),
                        ("wz", SSM_WIDTH), ("wdt", 2 * SSM_HEADS)):
        parts[name] = w[:, o:o + width]
        o += width
    wo = w_out[l].astype(BF)
    wr = w_router[l].T
    wrh = wr.astype(BF)
    nh2 = 2 * SSM_HEADS
    parts.update(
        wdtt=parts["wdt"].T,
        n1=norm1[l][None, :], n2=norm2[l][None, :],
        gq=jnp.tile(q_norm[l], N_HEADS)[None, :], gk=jnp.tile(k_norm[l], N_KV_HEADS)[None, :],
        conv_w=conv_w[l], conv_b=conv_b[l][None, :],
        alog_r=a_log[l].reshape(1, nh2), alog_c=a_log[l].reshape(nh2, 1),
        dtb_r=dt_bias[l].reshape(1, nh2), dtb_c=dt_bias[l].reshape(nh2, 1),
        dskip=jnp.repeat(d_skip[l], SSM_HEAD_DIM)[None, :], gn=ssm_norm[l][None, :],
        woa=wo[:Q_DIM], wos=wo[Q_DIM:],
        wrh=wrh, wrl=(wr - wrh.astype(F32)).astype(BF),
    )
    return parts


def _moe(h2p, aff_t, x1, mod3, w_gate, w_up, w_down, layer, n_sets, geom):
    t = x1.shape[0]
    n = t // n_sets
    cap = EC_FACTOR * n // N_EXPERTS
    pairs = N_EXPERTS * cap
    gidx, gates, dest, start, end = _plan(aff_t, n_sets, cap)
    z = _expert_ffn(gidx.reshape(-1), dest.reshape(-1), h2p, gates, w_gate, w_up, w_down, layer)
    tile_lo = start.reshape(t)[::TOKEN_TILE].astype(I32) // COMBINE_ROWS
    tile_hi = (end.reshape(t)[TOKEN_TILE - 1::TOKEN_TILE].astype(I32) + COMBINE_ROWS - 1) // COMBINE_ROWS
    lanes_major = lambda a: a.reshape(t // LANES, 1, LANES)
    return _combine(z, n_sets * pairs, lanes_major(start), lanes_major(end), tile_lo, tile_hi, x1, mod3, geom)


def kernel(x_prompt, x_sample, cache_k, cache_v, state_ssm, c, c_ctx, norm1, norm2, ada_w, ada_b,
           w_in, conv_w, conv_b, q_norm, k_norm, A_log, dt_bias, d_skip, ssm_norm, w_out,
           w_router, w_gate, w_up, w_down):
    batch, seq, d = x_prompt.shape
    dec_batch, dec_seq, _ = x_sample.shape
    depth = w_in.shape[0]
    past = cache_k.shape[2]
    geom = _Geometry(batch, seq, dec_batch, dec_seq)
    assert geom.n_ctx == geom.n_lat, "routed sets are stacked as two equal halves"

    rows = -(-(1 + dec_batch) // SUBLANES) * SUBLANES
    cv = jnp.zeros((rows, d), F32).at[0].set(c_ctx).at[1:1 + dec_batch].set(c)
    mod = _ada_modulation(cv, ada_w, ada_b)
    cos, sin = _rope_tables(dec_seq)
    consts = dict(cos=cos, sin=sin, bdq=_block_diag_ones(Q_DIM), bdk=_block_diag_ones(KV_DIM))
    cache_k4 = cache_k.reshape(dec_batch, depth, past, KV_DIM)
    cache_v4 = cache_v.reshape(dec_batch, depth, past, KV_DIM)

    x = (x_prompt.reshape(geom.n_ctx, d), x_sample.reshape(geom.n_lat, d))
    ks_new, vs_new, hs_new = [], [], []
    for l in range(depth):
        lw = _layer_weights(l, norm1, norm2, w_in, conv_w, conv_b, q_norm, k_norm, A_log, dt_bias,
                            d_skip, ssm_norm, w_out, w_router)
        mod3 = mod[l][:, None, :]
        q, k, v, xbc, z, dt, dtt = _in_projection(x, mod3, lw, consts, geom)
        attn_c, attn_l = _attention(q, k, v, cache_k4, cache_v4, l, geom)
        ssm_c, ssm_l, h_fin = _ssd_mixer(xbc, z, dt, dtt, state_ssm, l, lw, geom)
        x1, h2p, aff_t = _out_projection(attn_c, attn_l, ssm_c, ssm_l, x, mod3, lw, geom)
        x = _moe(h2p, aff_t, x1, mod3, w_gate, w_up, w_down, l, 2, geom)
        ks_new.append(k[:geom.n_ctx].reshape(batch, seq, N_KV_HEADS, HEAD_DIM))
        vs_new.append(v[:geom.n_ctx].reshape(batch, seq, N_KV_HEADS, HEAD_DIM))
        hs_new.append(h_fin)
    y_prompt = x[0].reshape(batch, seq, d)
    y_sample = x[1].reshape(dec_batch, dec_seq, d)
    return (y_prompt, y_sample, jnp.stack(ks_new, axis=1), jnp.stack(vs_new, axis=1),
            jnp.stack(hs_new, axis=1))
```

```python
import functools

import jax
import jax.numpy as jnp
from jax import lax
from jax.experimental import pallas as pl
from jax.experimental.pallas import tpu as pltpu

F32 = jnp.float32
BF = jnp.bfloat16
I32 = jnp.int32

GRID_W = 64
N_HEADS = 8
N_KV_HEADS = 2
HEAD_DIM = 64
GQA = N_HEADS // N_KV_HEADS
ROPE_THETA = 10000.0
ROT_PAIRS = HEAD_DIM // 4
Q_DIM = N_HEADS * HEAD_DIM
KV_DIM = N_KV_HEADS * HEAD_DIM
SSM_WIDTH = 512
SSM_HEADS = 8
SSM_HEAD_DIM = 64
SSM_GROUPS = 2
SSM_STATE = 64
CONV_W = 5
CONV_PAD = CONV_W // 2
CHUNK = 128
BC_DIM = SSM_GROUPS * SSM_STATE
CONV_DIM = SSM_WIDTH + 2 * BC_DIM
N_EXPERTS = 16
EC_FACTOR = 2
EPS = 1e-6
SCALE = HEAD_DIM ** -0.5
LOG2_E = 1.4426950408889634

LANES = 128
SUBLANES = 8
TOKEN_TILE = 512
QUERY_TILE = 256
HALO = SUBLANES
VMEM_LIMIT = 56 * 1024 * 1024

NT_DIMS = (((1,), (1,)), ((), ()))


def _dot(a, b):
    return jnp.dot(a, b, preferred_element_type=F32)


def _dot_nt(a, b):
    return lax.dot_general(a, b, NT_DIMS, preferred_element_type=F32)


def _bf16_pieces(v):
    p1 = v.astype(BF)
    r1 = v - p1.astype(F32)
    p2 = r1.astype(BF)
    return p1, p2, (r1 - p2.astype(F32)).astype(BF)


def _dot_f32_by_01(a, b, *, f32_side):
    if f32_side == 0:
        d1, d2, d3 = (_dot(p, b) for p in _bf16_pieces(a))
    else:
        d1, d2, d3 = (_dot(a, p) for p in _bf16_pieces(b))
    return d1 + (d2 + d3)


def _silu(x):
    return x / (1.0 + jnp.exp(-x))


def _softplus(x):
    return jnp.maximum(x, 0.0) + jnp.log(1.0 + jnp.exp(-jnp.abs(x)))


def _params(sem):
    return pltpu.CompilerParams(dimension_semantics=sem, vmem_limit_bytes=VMEM_LIMIT)


def _ada_kernel(cv_ref, w_ref, b_ref, o_ref):
    s = _silu(cv_ref[...]).astype(BF)
    o_ref[...] = _dot(s, w_ref[...].astype(BF)) + b_ref[...]


def _ada_modulation(cv, ada_w, ada_b):
    depth, d, n6 = ada_w.shape
    rows = cv.shape[0]
    tn = n6 // 4
    return pl.pallas_call(
        _ada_kernel,
        grid=(depth, n6 // tn),
        in_specs=[
            pl.BlockSpec((rows, d), lambda l, j: (0, 0)),
            pl.BlockSpec((None, d, tn), lambda l, j: (l, 0, j)),
            pl.BlockSpec((None, 1, tn), lambda l, j: (l, 0, j)),
        ],
        out_specs=pl.BlockSpec((None, rows, tn), lambda l, j: (l, 0, j)),
        out_shape=jax.ShapeDtypeStruct((depth, rows, n6), F32),
        compiler_params=_params(("parallel", "parallel")),
        name="ada_modulation",
    )(cv, ada_w, ada_b.reshape(depth, 1, n6))


def _inproj_kernel(xc_ref, xl_ref, mod_ref, n1_ref, wq_ref, wkv_ref, wx_ref, wz_ref, wdtt_ref,
                   bdq_ref, bdk_ref, gq_ref, gk_ref, cos_ref, sin_ref,
                   q_out, k_out, v_out, xbc_out, z_out, dt_out, dtt_out, *, ctx_tiles):
    d = xc_ref.shape[-1]
    x = jnp.where(pl.program_id(0) < ctx_tiles, xc_ref[...], xl_ref[...])
    mod = mod_ref[...]
    sh1 = mod[:, 0:d]
    sc1 = mod[:, d:2 * d]
    ms = jnp.mean(x * x, axis=-1, keepdims=True)
    h = x * lax.rsqrt(ms + EPS) * n1_ref[...]
    hb = (h * (1.0 + sc1) + sh1).astype(BF)
    cos = cos_ref[...]
    sin = sin_ref[...]

    def norm_rope(t, bd_ref, g_ref, cos_t, sin_t):
        ss = _dot((t * t).astype(BF), bd_ref[...])
        tn = t * lax.rsqrt(ss * (1.0 / HEAD_DIM) + EPS) * g_ref[...]
        width = t.shape[-1]
        lane = lax.broadcasted_iota(I32, tn.shape, 1)
        first_half = (lane % HEAD_DIM) < (HEAD_DIM // 2)
        partner = jnp.where(first_half,
                            pltpu.roll(tn, width - HEAD_DIM // 2, 1),
                            pltpu.roll(tn, HEAD_DIM // 2, 1))
        return tn * cos_t + partner * sin_t

    q = norm_rope(_dot(hb, wq_ref[...]), bdq_ref, gq_ref, cos, sin)
    q_out[...] = (q * (SCALE * LOG2_E)).astype(BF)
    kv = _dot(hb, wkv_ref[...])
    k_out[...] = norm_rope(kv[:, :KV_DIM], bdk_ref, gk_ref, cos[:, :KV_DIM], sin[:, :KV_DIM])
    v_out[...] = kv[:, KV_DIM:]
    xbc_out[...] = _dot(hb, wx_ref[...])
    z_out[...] = _dot(hb, wz_ref[...])
    dtt = _dot_nt(wdtt_ref[...], hb)
    dtt_out[...] = dtt
    padded = jnp.concatenate([dtt, jnp.zeros((LANES - dtt.shape[0], dtt.shape[1]), F32)], axis=0)
    dt_out[...] = padded.T[:, :dtt.shape[0]]


def _group_views(x, geom):
    tm = TOKEN_TILE
    nc = geom.ctx_tiles
    if isinstance(x, tuple):
        xc, xl = x
        lat_index = lambda i: (jnp.maximum(i - nc, 0), 0)
    else:
        xc = xl = x
        lat_index = lambda i: (jnp.maximum(i, nc), 0)
    d = xc.shape[-1]
    return xc, xl, pl.BlockSpec((tm, d), lambda i: (jnp.minimum(i, nc - 1), 0)), pl.BlockSpec((tm, d), lat_index)


def _in_projection(x, mod3, lw, consts, geom):
    xc, xl, xc_spec, xl_spec = _group_views(x, geom)
    t, d = geom.n_ctx + geom.n_lat, xc.shape[-1]
    tm = TOKEN_TILE
    full = lambda a: pl.BlockSpec(a.shape, lambda i: (0,) * a.ndim)
    row = lambda w: pl.BlockSpec((tm, w), lambda i: (i, 0))
    weights = [lw["wq"], lw["wkv"], lw["wx"], lw["wz"], lw["wdtt"],
               consts["bdq"], consts["bdk"], lw["gq"], lw["gk"]]
    out_shape = [
        jax.ShapeDtypeStruct((t, Q_DIM), BF),
        jax.ShapeDtypeStruct((t, KV_DIM), F32),
        jax.ShapeDtypeStruct((t, KV_DIM), F32),
        jax.ShapeDtypeStruct((t, CONV_DIM), F32),
        jax.ShapeDtypeStruct((t, SSM_WIDTH), F32),
        jax.ShapeDtypeStruct((t, 2 * SSM_HEADS), F32),
        jax.ShapeDtypeStruct((2 * SSM_HEADS, t), F32),
    ]
    return pl.pallas_call(
        functools.partial(_inproj_kernel, ctx_tiles=geom.ctx_tiles),
        grid=(t // tm,),
        in_specs=[xc_spec, xl_spec,
                  pl.BlockSpec((None, 1, mod3.shape[-1]), lambda i: (geom.mod_row(i), 0, 0)),
                  full(lw["n1"])]
                 + [full(w) for w in weights]
                 + [pl.BlockSpec((tm, Q_DIM), lambda i: (geom.rope_block(i), 0)),
                    pl.BlockSpec((tm, Q_DIM), lambda i: (geom.rope_block(i), 0))],
        out_specs=[row(Q_DIM), row(KV_DIM), row(KV_DIM), row(CONV_DIM), row(SSM_WIDTH),
                   row(2 * SSM_HEADS), pl.BlockSpec((2 * SSM_HEADS, tm), lambda i: (0, i))],
        out_shape=out_shape,
        compiler_params=_params(("parallel",)),
        name="in_projection",
    )(xc, xl, mod3, lw["n1"], *weights, consts["cos"], consts["sin"])


def _head_pair_operands(k, v):
    lane = lax.broadcasted_iota(I32, k.shape, 1)
    low = lane < HEAD_DIM
    k_sw = pltpu.roll(k, HEAD_DIM, 1)
    v_sw = pltpu.roll(v, HEAD_DIM, 1)
    ones_last = jnp.where(lane == LANES - 1, 1.0, 0.0)
    ones_first = jnp.where(lane == 0, 1.0, 0.0)
    kk = jnp.stack([jnp.where(low, k, k_sw), jnp.where(low, k_sw, k)]).astype(BF)
    v_lo = jnp.stack([jnp.where(low, v, ones_last), jnp.where(low, v_sw, ones_last)]).astype(BF)
    v_hi = jnp.stack([jnp.where(low, ones_first, v_sw), jnp.where(low, ones_first, v)]).astype(BF)
    return kk, v_lo, v_hi


def _attend(q_ref, o_ref, key_sets):
    tq = q_ref.shape[0]
    lane = lax.broadcasted_iota(I32, (tq, LANES), 1)
    low = lane < HEAD_DIM
    for pair in range(N_HEADS // 2):
        g = (2 * pair) // GQA
        q2 = q_ref[:, pair * LANES:(pair + 1) * LANES]
        zero = jnp.zeros_like(q2)
        accs = []
        for half, qh in enumerate((jnp.where(low, q2, zero), jnp.where(low, zero, q2))):
            scores = [_dot_nt(qh, ks[0][g]) for ks in key_sets]
            m = scores[0].max(axis=-1, keepdims=True)
            for s in scores[1:]:
                m = jnp.maximum(m, s.max(axis=-1, keepdims=True))
            acc = jnp.zeros((tq, LANES), F32)
            for s, ks in zip(scores, key_sets):
                acc = acc + _dot(jnp.exp2(s - m).astype(BF), ks[1 + half][g])
            accs.append(acc)
        out = jnp.where(low, accs[0] * (1.0 / accs[0][:, LANES - 1:LANES]), accs[1] * (1.0 / accs[1][:, 0:1]))
        o_ref[:, pair * LANES:(pair + 1) * LANES] = out.astype(BF)


def _attn_ctx_kernel(q_ref, k_ref, v_ref, o_ref):
    _attend(q_ref, o_ref, [_head_pair_operands(k_ref[...], v_ref[...])])


def _attn_lat_kernel(q_ref, kc_ref, vc_ref, kl_ref, vl_ref, o_ref, *ops):
    @pl.when(pl.program_id(1) == 0)
    def _():
        for dst, src in zip(ops, _head_pair_operands(kc_ref[...], vc_ref[...])
                            + _head_pair_operands(kl_ref[...], vl_ref[...])):
            dst[...] = src

    _attend(q_ref, o_ref, [ops[0:3], ops[3:6]])


def _attention(q, k, v, cache_k4, cache_v4, layer, geom):
    seq = geom.seq
    o_ctx = pl.pallas_call(
        _attn_ctx_kernel,
        grid=(geom.batch,),
        in_specs=[pl.BlockSpec((seq, Q_DIM), lambda b: (b, 0)),
                  pl.BlockSpec((seq, KV_DIM), lambda b: (b, 0)),
                  pl.BlockSpec((seq, KV_DIM), lambda b: (b, 0))],
        out_specs=pl.BlockSpec((seq, Q_DIM), lambda b: (b, 0)),
        out_shape=jax.ShapeDtypeStruct((geom.n_ctx, Q_DIM), BF),
        compiler_params=_params(("parallel",)),
        name="attention_context",
    )(q, k, v)

    tq = QUERY_TILE
    dseq = geom.dec_seq
    past = cache_k4.shape[2]
    q_tiles = dseq // tq
    ctx_tiles = geom.n_ctx // tq
    ctx_seqs = geom.n_ctx // dseq
    o_lat = pl.pallas_call(
        _attn_lat_kernel,
        grid=(geom.dec_batch, q_tiles),
        in_specs=[pl.BlockSpec((tq, Q_DIM), lambda b, i: (ctx_tiles + b * q_tiles + i, 0)),
                  pl.BlockSpec((None, None, past, KV_DIM), lambda b, i: (b, layer, 0, 0)),
                  pl.BlockSpec((None, None, past, KV_DIM), lambda b, i: (b, layer, 0, 0)),
                  pl.BlockSpec((dseq, KV_DIM), lambda b, i: (ctx_seqs + b, 0)),
                  pl.BlockSpec((dseq, KV_DIM), lambda b, i: (ctx_seqs + b, 0))],
        out_specs=pl.BlockSpec((tq, Q_DIM), lambda b, i: (b * q_tiles + i, 0)),
        out_shape=jax.ShapeDtypeStruct((geom.n_lat, Q_DIM), BF),
        scratch_shapes=[pltpu.VMEM((N_KV_HEADS, s, KV_DIM), BF) for s in (past, past, past, dseq, dseq, dseq)],
        compiler_params=_params(("parallel", "arbitrary")),
        name="attention_latent",
    )(q, cache_k4, cache_v4, k, v)
    return o_ctx, o_lat


def _ssd_kernel(*refs, n_chunks, has_h0, emit_state):
    it = iter(refs)
    xbc_ref, z_ref, dt_ref, dtt_ref = next(it), next(it), next(it), next(it)
    h0_ref = next(it) if has_h0 else None
    (cw_ref, cb_ref, alog_r_ref, alog_c_ref, dtb_r_ref, dtb_c_ref,
     dskip_ref, gn_ref) = (next(it) for _ in range(8))
    y_out = next(it)
    hfin_out = next(it) if emit_state else None
    pad_ref, u_ref, csc_ref, csr_ref, dtr_ref, tot_ref, st_ref, hcur_ref, y_ref = (next(it) for _ in range(9))

    seq = xbc_ref.shape[0]
    nh = SSM_HEADS
    q = CHUNK

    pad_ref[0:HALO, :] = jnp.zeros((HALO, CONV_DIM), F32)
    pad_ref[HALO:HALO + seq, :] = xbc_ref[...]
    pad_ref[HALO + seq:2 * HALO + seq, :] = jnp.zeros((HALO, CONV_DIM), F32)

    a_row = -jnp.exp(alog_r_ref[...])
    a_col = -jnp.exp(alog_c_ref[...])
    ii = lax.broadcasted_iota(I32, (q, q), 0)
    jj = lax.broadcasted_iota(I32, (q, q), 1)
    lower = jj <= ii
    upper = jj >= ii
    low_half = jj < SSM_STATE
    tri_lower = lower.astype(BF)
    tri_upper = upper.astype(BF)
    fwd_lane = lax.broadcasted_iota(I32, (q, 2 * nh), 1) < nh
    fwd_row = lax.broadcasted_iota(I32, (2 * nh, q), 0) < nh

    def chunk_stats(c, carry):
        base = pl.multiple_of(c * q, q)
        win = pad_ref[pl.ds(base, q + 2 * HALO), :]
        acc = jnp.zeros((q, CONV_DIM), F32) + cb_ref[...]
        for tap in range(CONV_W):
            off = HALO - CONV_PAD + tap
            acc = acc + win[off:off + q, :] * cw_ref[tap:tap + 1, :]
        u = _silu(acc)
        u_ref[pl.ds(base, q), :] = u

        dtc = _softplus(dt_ref[pl.ds(base, q), :] + dtb_r_ref[...])
        dtr = _softplus(dtt_ref[:, pl.ds(base, q)] + dtb_c_ref[...])
        ac = dtc * a_row
        ar = dtr * a_col
        cs_col = jnp.where(fwd_lane, _dot_f32_by_01(tri_lower, ac, f32_side=1),
                           _dot_f32_by_01(tri_upper, ac, f32_side=1))
        cs_row = jnp.where(fwd_row, _dot_f32_by_01(ar, tri_upper, f32_side=0),
                           _dot_f32_by_01(ar, tri_lower, f32_side=0))
        csc_ref[pl.ds(base, q), :] = cs_col
        csr_ref[:, pl.ds(base, q)] = cs_row
        dtr_ref[:, pl.ds(base, q)] = dtr
        total = jnp.where(fwd_lane[0:1, :], cs_col[q - 1:q, :], cs_col[0:1, :])
        tot_ref[c] = total
        total_c = jnp.where(fwd_row[:, 0:1], cs_row[:, q - 1:q], cs_row[:, 0:1])
        w_end = jnp.exp(total_c - cs_row) * dtr

        xs_b = u[:, 0:SSM_WIDTH].astype(BF)
        b_t = u[:, SSM_WIDTH:SSM_WIDTH + BC_DIM].T
        for dh in range(2 * nh):
            head = dh % nh
            grp = head // (nh // SSM_GROUPS)
            bw = (b_t[grp * SSM_STATE:(grp + 1) * SSM_STATE, :] * w_end[dh:dh + 1, :]).astype(BF)
            st_ref[c, dh] = _dot(bw, xs_b[:, head * SSM_HEAD_DIM:(head + 1) * SSM_HEAD_DIM])
        return carry

    lax.fori_loop(0, n_chunks, chunk_stats, 0)

    eye = (lax.broadcasted_iota(I32, (SSM_STATE, SSM_STATE), 0)
           == lax.broadcasted_iota(I32, (SSM_STATE, SSM_STATE), 1)).astype(BF)

    def transposed_blocks(stack):
        p1, p2, p3 = _bf16_pieces(stack)
        return _dot_nt(eye, p1) + (_dot_nt(eye, p2) + _dot_nt(eye, p3))

    if has_h0:
        h0_t = transposed_blocks(h0_ref[...].reshape(2 * nh * SSM_HEAD_DIM, SSM_STATE))
    for dh in range(2 * nh):
        hcur_ref[dh] = (h0_t[:, dh * SSM_HEAD_DIM:(dh + 1) * SSM_HEAD_DIM] if has_h0
                        else jnp.zeros((SSM_STATE, SSM_HEAD_DIM), F32))

    def recur(step, carry):
        for direction in range(2):
            c = step if direction == 0 else n_chunks - 1 - step
            decay = jnp.exp(tot_ref[c])
            for head in range(nh):
                dh = direction * nh + head
                new = st_ref[c, dh]
                cur = hcur_ref[dh]
                st_ref[c, dh] = cur
                hcur_ref[dh] = cur * decay[:, dh:dh + 1] + new
        return carry

    lax.fori_loop(0, n_chunks, recur, 0)
    if emit_state:
        h_t = transposed_blocks(hcur_ref[...].reshape(2 * nh * SSM_STATE, SSM_HEAD_DIM))
        for dh in range(2 * nh):
            hfin_out[dh // nh, dh % nh] = h_t[:, dh * SSM_STATE:(dh + 1) * SSM_STATE]

    def chunk_out(c, carry):
        base = pl.multiple_of(c * q, q)
        u = u_ref[pl.ds(base, q), :]
        xs = u[:, 0:SSM_WIDTH]
        xs_b = xs.astype(BF)
        bmat = u[:, SSM_WIDTH:SSM_WIDTH + BC_DIM].astype(BF)
        cmat = u[:, SSM_WIDTH + BC_DIM:CONV_DIM]
        cmat_b = cmat.astype(BF)
        cs_col = csc_ref[pl.ds(base, q), :]
        cs_row = csr_ref[:, pl.ds(base, q)]
        dtr = dtr_ref[:, pl.ds(base, q)]
        c_sw = pltpu.roll(cmat, SSM_STATE, 1)
        for grp in range(SSM_GROUPS):
            gs = slice(grp * SSM_STATE, (grp + 1) * SSM_STATE)
            cb = _dot_nt(cmat_b[:, gs], bmat[:, gs])
            c_dup = jnp.where(low_half, cmat, c_sw) if grp == 0 else jnp.where(low_half, c_sw, cmat)
            for head in range(grp * (nh // SSM_GROUPS), (grp + 1) * (nh // SSM_GROUPS)):
                hb = nh + head
                col_f = jnp.broadcast_to(cs_col[:, head:head + 1], (q, q))
                col_b = jnp.broadcast_to(cs_col[:, hb:hb + 1], (q, q))
                seg_f = jnp.where(lower, col_f - cs_row[head:head + 1, :], -1e30)
                seg_b = jnp.where(upper, col_b - cs_row[hb:hb + 1, :], -1e30)
                mix = cb * (jnp.exp(seg_f) * dtr[head:head + 1, :] + jnp.exp(seg_b) * dtr[hb:hb + 1, :])
                carry_in = c_dup * jnp.exp(jnp.where(low_half, col_f, col_b))
                hs = slice(head * SSM_HEAD_DIM, (head + 1) * SSM_HEAD_DIM)
                lhs = jnp.concatenate([mix.astype(BF), carry_in.astype(BF)], axis=1)
                rhs = jnp.concatenate([xs_b[:, hs], st_ref[c, head].astype(BF), st_ref[c, hb].astype(BF)], axis=0)
                y_ref[:, hs] = _dot(lhs, rhs)
        y = y_ref[...] + xs * dskip_ref[...]
        y = y * _silu(z_ref[pl.ds(base, q), :])
        ms = jnp.mean(y * y, axis=-1, keepdims=True)
        y_out[pl.ds(base, q), :] = (y * lax.rsqrt(ms + EPS) * gn_ref[...]).astype(BF)
        return carry

    lax.fori_loop(0, n_chunks, chunk_out, 0)


def _ssd_call(xbc, z, dt, dtt, h0, layer, lw, *, seq, n_seq, row_block0, emit_state, name):
    n_chunks = seq // CHUNK
    nh2 = 2 * SSM_HEADS
    full = lambda a: pl.BlockSpec(a.shape, lambda b: (0,) * a.ndim)
    params = [lw["conv_w"], lw["conv_b"], lw["alog_r"], lw["alog_c"], lw["dtb_r"], lw["dtb_c"],
              lw["dskip"], lw["gn"]]
    in_specs = [pl.BlockSpec((seq, CONV_DIM), lambda b: (row_block0 + b, 0)),
                pl.BlockSpec((seq, SSM_WIDTH), lambda b: (row_block0 + b, 0)),
                pl.BlockSpec((seq, nh2), lambda b: (row_block0 + b, 0)),
                pl.BlockSpec((nh2, seq), lambda b: (0, row_block0 + b))]
    args = [xbc, z, dt, dtt]
    if h0 is not None:
        in_specs.append(pl.BlockSpec((None, None, 2, SSM_HEADS, SSM_HEAD_DIM, SSM_STATE),
                                     lambda b: (b, layer, 0, 0, 0, 0)))
        args.append(h0)
    in_specs += [full(p) for p in params]
    args += params
    out_specs = [pl.BlockSpec((seq, SSM_WIDTH), lambda b: (b, 0))]
    out_shape = [jax.ShapeDtypeStruct((n_seq * seq, SSM_WIDTH), BF)]
    if emit_state:
        out_specs.append(pl.BlockSpec((None, 2, SSM_HEADS, SSM_HEAD_DIM, SSM_STATE),
                                      lambda b: (b, 0, 0, 0, 0)))
        out_shape.append(jax.ShapeDtypeStruct((n_seq, 2, SSM_HEADS, SSM_HEAD_DIM, SSM_STATE), F32))
    scratch = [
        pltpu.VMEM((seq + 2 * HALO, CONV_DIM), F32),
        pltpu.VMEM((seq, CONV_DIM), F32),
        pltpu.VMEM((seq, nh2), F32),
        pltpu.VMEM((nh2, seq), F32),
        pltpu.VMEM((nh2, seq), F32),
        pltpu.VMEM((n_chunks, 1, nh2), F32),
        pltpu.VMEM((n_chunks, nh2, SSM_HEAD_DIM, SSM_STATE), F32),
        pltpu.VMEM((nh2, SSM_HEAD_DIM, SSM_STATE), F32),
        pltpu.VMEM((CHUNK, SSM_WIDTH), F32),
    ]
    return pl.pallas_call(
        functools.partial(_ssd_kernel, n_chunks=n_chunks, has_h0=h0 is not None, emit_state=emit_state),
        grid=(n_seq,),
        in_specs=in_specs,
        out_specs=out_specs,
        out_shape=out_shape,
        scratch_shapes=scratch,
        compiler_params=_params(("parallel",)),
        name=name,
    )(*args)


def _ssd_mixer(xbc, z, dt, dtt, state_ssm, layer, lw, geom):
    y_ctx, h_fin = _ssd_call(xbc, z, dt, dtt, None, layer, lw, seq=geom.seq, n_seq=geom.batch,
                             row_block0=0, emit_state=True, name="ssd_context")
    (y_lat,) = _ssd_call(xbc, z, dt, dtt, state_ssm, layer, lw, seq=geom.dec_seq,
                         n_seq=geom.dec_batch, row_block0=geom.n_ctx // geom.dec_seq,
                         emit_state=False, name="ssd_latent")
    return y_ctx, y_lat, h_fin


def _outproj_kernel(ac_ref, al_ref, sc_ref, sl_ref, xc_ref, xl_ref, mod_ref, n2_ref, woa_ref, wos_ref,
                    wrh_ref, wrl_ref, x1_out, h2_out, afft_out, *, ctx_tiles):
    d = xc_ref.shape[-1]
    mod = mod_ref[...]
    g1 = mod[:, 2 * d:3 * d]
    sh2 = mod[:, 3 * d:4 * d]
    sc2 = mod[:, 4 * d:5 * d]
    is_ctx = pl.program_id(0) < ctx_tiles
    attn = jnp.where(is_ctx, ac_ref[...], al_ref[...])
    ssm = jnp.where(is_ctx, sc_ref[...], sl_ref[...])
    y = _dot(attn, woa_ref[...]) + _dot(ssm, wos_ref[...])
    x1 = jnp.where(is_ctx, xc_ref[...], xl_ref[...]) + g1 * y
    x1_out[...] = x1
    ms = jnp.mean(x1 * x1, axis=-1, keepdims=True)
    h2 = x1 * lax.rsqrt(ms + EPS) * n2_ref[...]
    h2 = h2 * (1.0 + sc2) + sh2
    h2_out[...] = h2
    hh = h2.astype(BF)
    hl = (h2 - hh.astype(F32)).astype(BF)
    wh = wrh_ref[...]
    logits = _dot_nt(wh, hh) + _dot_nt(wh, hl) + _dot_nt(wrl_ref[...], hh)
    logits = logits - logits.max(axis=0, keepdims=True)
    e = jnp.exp(logits)
    afft_out[...] = e / e.sum(axis=0, keepdims=True)


def _out_projection(attn_c, attn_l, ssm_c, ssm_l, x, mod3, lw, geom):
    xc, xl, xc_spec, xl_spec = _group_views(x, geom)
    t, d = geom.n_ctx + geom.n_lat, xc.shape[-1]
    tm = TOKEN_TILE
    nc = geom.ctx_tiles
    full = lambda a: pl.BlockSpec(a.shape, lambda i: (0,) * a.ndim)
    row = lambda w: pl.BlockSpec((tm, w), lambda i: (i, 0))
    ctx_row = lambda w: pl.BlockSpec((tm, w), lambda i: (jnp.minimum(i, nc - 1), 0))
    lat_row = lambda w: pl.BlockSpec((tm, w), lambda i: (jnp.maximum(i - nc, 0), 0))
    weights = [lw["n2"], lw["woa"], lw["wos"], lw["wrh"], lw["wrl"]]
    return pl.pallas_call(
        functools.partial(_outproj_kernel, ctx_tiles=nc),
        grid=(t // tm,),
        in_specs=[ctx_row(Q_DIM), lat_row(Q_DIM), ctx_row(SSM_WIDTH), lat_row(SSM_WIDTH), xc_spec, xl_spec,
                  pl.BlockSpec((None, 1, mod3.shape[-1]), lambda i: (geom.mod_row(i), 0, 0))]
                 + [full(w) for w in weights],
        out_specs=[row(d), row(d), pl.BlockSpec((N_EXPERTS, tm), lambda i: (0, i))],
        out_shape=[jax.ShapeDtypeStruct((t, d), F32), jax.ShapeDtypeStruct((t, d), F32),
                   jax.ShapeDtypeStruct((N_EXPERTS, t), F32)],
        compiler_params=_params(("parallel",)),
        name="out_projection",
    )(attn_c, attn_l, ssm_c, ssm_l, xc, xl, mod3, *weights)


def _flat_cumsum(m2, rows):
    gr = m2.shape[0]
    li = lax.broadcasted_iota(I32, (LANES, LANES), 0)
    lj = lax.broadcasted_iota(I32, (LANES, LANES), 1)
    mb = m2.astype(BF)
    within = _dot(mb, (li <= lj).astype(BF))
    row_tot = _dot(mb, jnp.ones((LANES, LANES), BF))
    ri = lax.broadcasted_iota(I32, (gr, gr), 0)
    rj = lax.broadcasted_iota(I32, (gr, gr), 1)
    before = ((ri // rows == rj // rows) & (rj < ri)).astype(BF)
    hi, lo = _split_base128(row_tot)
    return within + (LANES * _dot(before, hi) + _dot(before, lo))


def _split_base128(v):
    hi = jnp.floor(v * (1.0 / LANES))
    return hi.astype(BF), (v - hi * LANES).astype(BF)


def _plan_kernel(aff_ref, idx_out, gate_out, dslot_out, start_out, end_out, *, cap):
    a = aff_ref[...]
    n_e, rows, _ = a.shape

    def count(hit):
        return hit.astype(I32).sum(axis=1, keepdims=True).sum(axis=2, keepdims=True)

    def kth_largest_bits(values, member, k):
        def search(step, lo):
            cand = lo | (1 << (30 - step))
            return jnp.where(count(member & (values >= lax.bitcast_convert_type(cand, F32))) >= k, cand, lo)
        return lax.fori_loop(0, 31, search, jnp.zeros((n_e, 1, 1), I32))

    everyone = jnp.full(a.shape, True)
    thr_bits = kth_largest_bits(a, everyone, cap)
    thr = lax.bitcast_convert_type(thr_bits, F32)
    above = a >= lax.bitcast_convert_type(thr_bits + 1, F32)
    cell = (a >= thr) & jnp.logical_not(above)
    need = cap - count(above)
    resid = jnp.where(cell, a - thr, 0.0)
    thr2 = lax.bitcast_convert_type(kth_largest_bits(resid, cell, need), F32)
    gt = above | (cell & (resid > thr2))
    eq = cell & (resid == thr2)
    need = cap - count(gt)
    eqf = eq.astype(F32).reshape(n_e * rows, LANES)
    rank = (_flat_cumsum(eqf, rows) - eqf).reshape(n_e, rows, LANES)
    mask = gt | (eq & (rank < need.astype(F32)))
    mf = mask.astype(F32)

    c3 = _flat_cumsum(mf.reshape(n_e * rows, LANES), rows).reshape(n_e, rows, LANES)
    ahead = []
    run = jnp.zeros((rows, LANES), F32)
    for e in range(n_e):
        ahead.append(run)
        run = run + mf[e]
    cnt = run
    start = _flat_cumsum(cnt, rows) - cnt
    set_id = pl.program_id(0)
    row0 = (set_id * (n_e * cap)).astype(F32)
    start_out[...] = start + row0
    end_out[...] = start + cnt + row0

    s_lane = lax.broadcasted_iota(I32, (1, cap), 1).astype(F32)
    r_col = lax.broadcasted_iota(I32, (LANES, cap), 0).astype(F32)
    pad = jnp.zeros((LANES - rows, LANES), F32)

    def lanes_first(xm):
        return (jnp.concatenate([xm, pad], axis=0) if rows < LANES else xm).T

    for e in range(n_e):
        c_e = c3[e]
        row_of = (c_e[:, LANES - 1:LANES] <= s_lane).astype(F32).sum(axis=0, keepdims=True)
        onehot = (r_col == row_of).astype(BF)
        pieces = (_split_base128(lanes_first(c_e)) + _bf16_pieces(lanes_first(a[e]))
                  + _split_base128(lanes_first(start + ahead[e])))
        sel = _dot(jnp.concatenate(pieces, axis=0), onehot)
        part = [sel[j * LANES:(j + 1) * LANES] for j in range(len(pieces))]
        sel_c = LANES * part[0] + part[1]
        sel_a = part[2] + (part[3] + part[4])
        sel_d = LANES * part[5] + part[6]
        lane_of = (sel_c <= s_lane).astype(F32).sum(axis=0, keepdims=True)
        pick = r_col == lane_of
        idx_out[e:e + 1, :] = (row_of * LANES + lane_of).astype(I32) + set_id * (rows * LANES)
        gate_out[e:e + 1, :] = jnp.where(pick, sel_a, 0.0).sum(axis=0, keepdims=True)
        dslot_out[e:e + 1, :] = (jnp.where(pick, sel_d, 0.0).sum(axis=0, keepdims=True).astype(I32)
                                 + set_id * (n_e * cap))


def _plan(aff_t, n_sets, cap):
    n_e, t = aff_t.shape
    n = t // n_sets
    rows = n // LANES
    assert rows <= LANES
    aff4 = aff_t.reshape(n_e, n_sets, rows, LANES)
    slot_spec = pl.BlockSpec((n_e, cap), lambda s: (0, s))
    tok_spec = pl.BlockSpec((None, rows, LANES), lambda s: (s, 0, 0))
    slot_shape = lambda dt: jax.ShapeDtypeStruct((n_e, n_sets * cap), dt)
    tok_shape = jax.ShapeDtypeStruct((n_sets, rows, LANES), F32)
    return pl.pallas_call(
        functools.partial(_plan_kernel, cap=cap),
        grid=(n_sets,),
        in_specs=[pl.BlockSpec((n_e, None, rows, LANES), lambda s: (0, s, 0, 0))],
        out_specs=[slot_spec, slot_spec, slot_spec, tok_spec, tok_spec],
        out_shape=[slot_shape(I32), slot_shape(F32), slot_shape(I32), tok_shape, tok_shape],
        compiler_params=_params(("parallel",)),
        name="expert_choice_plan",
    )(aff4)


DMA_UNROLL = 8
ROW_BLOCK = 512
EXPERT_FF_TILE = 512
COMBINE_ROWS = 256
COMBINE_BUFFERS = 4


def _expert_kernel(gidx_ref, dest_ref, h2_hbm, gate_ref, wg_ref, wu_ref, wd_ref, z_hbm,
                   xw_a, xw_b, xb_ref, acc_a, acc_b, wgb_ref, wub_ref, wdb_ref, gcol_ref, sem_g, sem_s,
                   *, n_f, row_block):
    e = pl.program_id(0)
    f = pl.program_id(1)
    n_e = pl.num_programs(0)
    m_rows = xw_a.shape[0]
    n_blocks = m_rows // row_block
    per_block = m_rows // (n_f * n_blocks)

    def vmem_row(buf, m8, k):
        tile = pl.multiple_of(m8 + (k // SUBLANES) * SUBLANES, SUBLANES)
        return buf.at[pl.ds(tile, SUBLANES), :].at[pl.ds(k % SUBLANES, 1), :]

    def gather_copy(entry, m8, k, xw):
        return pltpu.make_async_copy(h2_hbm.at[pl.ds(gidx_ref[entry], 1), :], vmem_row(xw, m8, k), sem_g)

    def scatter_copy(entry, m8, k, acc):
        return pltpu.make_async_copy(vmem_row(acc, m8, k), z_hbm.at[pl.ds(dest_ref[entry], 1), :], sem_s)

    def wait_gathers(xw):
        pltpu.make_async_copy(h2_hbm.at[pl.ds(0, m_rows), :], xw, sem_g).wait()

    def wait_scatters(acc):
        pltpu.make_async_copy(acc, z_hbm.at[pl.ds(0, m_rows), :], sem_s).wait()

    def issue_all(make):
        def body(j, carry):
            for k in range(DMA_UNROLL):
                make(j * DMA_UNROLL, k).start()
            return carry
        lax.fori_loop(0, m_rows // DMA_UNROLL, body, 0)

    def run(xw_cur, xw_next, acc_cur, acc_prev):
        @pl.when((e == 0) & (f == 0))
        def _():
            acc_prev[...] = jnp.zeros_like(acc_prev)
            issue_all(lambda m8, k: gather_copy(m8 + k, m8, k, xw_cur))

        @pl.when(f == 0)
        def _():
            wait_gathers(xw_cur)
            xb_ref[...] = xw_cur[...].astype(BF)
            eye = (lax.broadcasted_iota(I32, (LANES, LANES), 0)
                   == lax.broadcasted_iota(I32, (LANES, LANES), 1)).astype(BF)
            g1, g2, g3 = _bf16_pieces(gate_ref[...])
            cols = _dot_nt(eye, g1) + (_dot_nt(eye, g2) + _dot_nt(eye, g3))
            for j in range(gate_ref.shape[0]):
                gcol_ref[j * LANES:(j + 1) * LANES, :] = cols[:, j:j + 1]

        wgb_ref[...] = wg_ref[...].astype(BF)
        wub_ref[...] = wu_ref[...].astype(BF)
        wdb_ref[...] = wd_ref[...].astype(BF)
        last = f == n_f - 1

        def block(rb, carry, *, is_first, is_last):
            r0 = pl.multiple_of(rb * row_block, row_block)
            m0 = (f * n_blocks + rb) * per_block

            for k in range(per_block):
                gather_copy((e + 1) * m_rows + m0 + k, m0, k, xw_next).start()
                scatter_copy(e * m_rows + m0 + k, m0, k, acc_prev).start()
            x = xb_ref[pl.ds(r0, row_block), :]
            g = _dot(x, wgb_ref[...])
            u = _dot(x, wub_ref[...])
            hid = (_silu(g) * u).astype(BF)
            total = _dot(hid, wdb_ref[...])
            if not is_first:
                total = acc_cur[pl.ds(r0, row_block), :] + total
            if is_last:
                total = total * gcol_ref[pl.ds(r0, row_block), :]
            acc_cur[pl.ds(r0, row_block), :] = total
            return carry

        for is_first, is_last in ((True, n_f == 1), (False, False), (False, True)):
            if (n_f == 1 and not is_first) or (n_f == 2 and not (is_first or is_last)):
                continue
            step = (f == 0) if is_first else (last if is_last else (f > 0) & jnp.logical_not(last))

            @pl.when(step)
            def _(is_first=is_first, is_last=is_last):
                lax.fori_loop(0, n_blocks, functools.partial(block, is_first=is_first, is_last=is_last), 0)

        @pl.when(last)
        def _():
            wait_scatters(acc_prev)

        @pl.when(last & (e == n_e - 1))
        def _():
            wait_gathers(xw_next)
            issue_all(lambda m8, k: scatter_copy((e + 1) * m_rows + m8 + k, m8, k, acc_cur))
            wait_scatters(acc_cur)

    @pl.when(e % 2 == 0)
    def _():
        run(xw_a, xw_b, acc_a, acc_b)

    @pl.when(e % 2 == 1)
    def _():
        run(xw_b, xw_a, acc_b, acc_a)


def _expert_ffn(gidx, dest, h2, gates, w_gate, w_up, w_down, layer):
    n_e, m = gates.shape
    gates = gates.reshape(n_e, m // LANES, LANES)
    d = h2.shape[1]
    ff = w_gate.shape[-1]
    tf = min(EXPERT_FF_TILE, ff)
    n_f = ff // tf
    row_block = min(ROW_BLOCK, m)
    assert m % DMA_UNROLL == 0 and (n_e * m) % COMBINE_ROWS == 0 and m % (n_f * (m // row_block)) == 0
    gidx_ext = jnp.concatenate([gidx, jnp.arange(m, dtype=I32)])
    dest_ext = jnp.concatenate([n_e * m + jnp.arange(m, dtype=I32), dest])
    return pl.pallas_call(
        functools.partial(_expert_kernel, n_f=n_f, row_block=row_block),
        grid_spec=pltpu.PrefetchScalarGridSpec(
            num_scalar_prefetch=2,
            grid=(n_e, n_f),
            in_specs=[pl.BlockSpec(memory_space=pl.ANY),
                      pl.BlockSpec((None, m // LANES, LANES), lambda e, f, gi, de: (e, 0, 0)),
                      pl.BlockSpec((None, None, d, tf), lambda e, f, gi, de: (layer, e, 0, f)),
                      pl.BlockSpec((None, None, d, tf), lambda e, f, gi, de: (layer, e, 0, f)),
                      pl.BlockSpec((None, None, tf, d), lambda e, f, gi, de: (layer, e, f, 0))],
            out_specs=pl.BlockSpec(memory_space=pl.ANY),
            scratch_shapes=[pltpu.VMEM((m, d), F32), pltpu.VMEM((m, d), F32),
                            pltpu.VMEM((m, d), BF), pltpu.VMEM((m, d), F32), pltpu.VMEM((m, d), F32),
                            pltpu.VMEM((d, tf), BF), pltpu.VMEM((d, tf), BF), pltpu.VMEM((tf, d), BF),
                            pltpu.VMEM((m, 1), F32),
                            pltpu.SemaphoreType.DMA, pltpu.SemaphoreType.DMA]),
        out_shape=jax.ShapeDtypeStruct(((n_e + 1) * m, d), F32),
        compiler_params=_params(("arbitrary", "arbitrary")),
        name="expert_swiglu",
    )(gidx_ext, dest_ext, h2, gates, w_gate, w_up, w_down)


def _combine_kernel(lo_ref, hi_ref, z_hbm, start_ref, end_ref, x1_ref, mod_ref, oc_ref, ol_ref, zbuf, acc_ref,
                    col_ref, sem, state, *, n_chunks, ctx_tiles):
    i = pl.program_id(0)
    d = x1_ref.shape[-1]
    n_buf, ch, _ = zbuf.shape
    lo = lo_ref[i]
    hi = hi_ref[i]

    def chunk_copy(kc):
        slot = kc % n_buf
        return pltpu.make_async_copy(z_hbm.at[pl.ds(pl.multiple_of(kc * ch, ch), ch), :], zbuf.at[slot],
                                     sem.at[slot])

    @pl.when(i == 0)
    def _():
        state[0] = 0
        state[1] = 0

    acc_ref[...] = jnp.zeros_like(acc_ref)
    tm = acc_ref.shape[0]
    blocks = tm // LANES
    rows = jnp.concatenate([start_ref[i * blocks + j] for j in range(blocks)]
                           + [end_ref[i * blocks + j] for j in range(blocks)], axis=0)
    eye = (lax.broadcasted_iota(I32, (LANES, LANES), 0) == lax.broadcasted_iota(I32, (LANES, LANES), 1)).astype(BF)
    upper = jnp.floor(rows * (1.0 / 256.0))
    cols = 256.0 * _dot_nt(eye, upper.astype(BF)) + _dot_nt(eye, (rows - 256.0 * upper).astype(BF))
    for j in range(blocks):
        col_ref[j * LANES:(j + 1) * LANES, 0:1] = cols[:, j:j + 1]
        col_ref[j * LANES:(j + 1) * LANES, 1:2] = cols[:, blocks + j:blocks + j + 1]
    start = col_ref[:, 0:1]
    end = col_ref[:, 1:2]

    def body(kc, carry):
        upto = jnp.minimum(kc + n_buf, n_chunks)

        def start_one(j, c):
            chunk_copy(j).start()
            return c

        lax.fori_loop(state[0], upto, start_one, 0)
        state[0] = jnp.maximum(state[0], upto)

        @pl.when(kc >= state[1])
        def _():
            chunk_copy(kc).wait()
            state[1] = kc + 1

        slot = kc % n_buf
        k = (kc * ch + lax.broadcasted_iota(I32, (1, ch), 1)).astype(F32)
        own = jnp.where((k >= start) & (k < end), 1.0, 0.0).astype(BF)
        acc_ref[...] += _dot(own, zbuf[slot].astype(BF))
        return carry

    lax.fori_loop(lo, hi, body, 0)
    g2 = mod_ref[...][:, 5 * d:6 * d]
    out = x1_ref[...] + g2 * acc_ref[...]

    @pl.when(i < ctx_tiles)
    def _():
        oc_ref[...] = out

    @pl.when(i >= ctx_tiles)
    def _():
        ol_ref[...] = out


def _combine(z, n_rows, start, end, tile_lo, tile_hi, x1, mod3, geom):
    t, d = x1.shape
    tm = TOKEN_TILE
    nc = geom.ctx_tiles
    whole = lambda a: pl.BlockSpec(a.shape, lambda i, lo, hi: (0,) * a.ndim)
    return pl.pallas_call(
        functools.partial(_combine_kernel, n_chunks=n_rows // COMBINE_ROWS, ctx_tiles=nc),
        grid_spec=pltpu.PrefetchScalarGridSpec(
            num_scalar_prefetch=2,
            grid=(t // tm,),
            in_specs=[pl.BlockSpec(memory_space=pl.ANY), whole(start), whole(end),
                      pl.BlockSpec((tm, d), lambda i, lo, hi: (i, 0)),
                      pl.BlockSpec((None, 1, mod3.shape[-1]), lambda i, lo, hi: (geom.mod_row(i), 0, 0))],
            out_specs=[pl.BlockSpec((tm, d), lambda i, lo, hi: (jnp.minimum(i, nc - 1), 0)),
                       pl.BlockSpec((tm, d), lambda i, lo, hi: (jnp.maximum(i - nc, 0), 0))],
            scratch_shapes=[pltpu.VMEM((COMBINE_BUFFERS, COMBINE_ROWS, d), F32), pltpu.VMEM((tm, d), F32),
                            pltpu.VMEM((tm, 2), F32),
                            pltpu.SemaphoreType.DMA((COMBINE_BUFFERS,)), pltpu.SMEM((2,), I32)]),
        out_shape=[jax.ShapeDtypeStruct((geom.n_ctx, d), F32), jax.ShapeDtypeStruct((geom.n_lat, d), F32)],
        compiler_params=_params(("arbitrary",)),
        name="gated_combine",
    )(tile_lo, tile_hi, z, start, end, x1, mod3)


class _Geometry:
    def __init__(self, batch, seq, dec_batch, dec_seq):
        self.batch, self.seq, self.dec_batch, self.dec_seq = batch, seq, dec_batch, dec_seq
        self.n_ctx = batch * seq
        self.n_lat = dec_batch * dec_seq
        tm = TOKEN_TILE
        assert self.n_ctx % tm == 0 and dec_seq % tm == 0 and self.n_ctx % dec_seq == 0
        assert seq % CHUNK == 0 and dec_seq % CHUNK == 0 and dec_seq % QUERY_TILE == 0
        self.ctx_tiles = self.n_ctx // tm
        self.lat_tiles_per_seq = dec_seq // tm

    def mod_row(self, i):
        return jnp.where(i < self.ctx_tiles, 0, 1 + (i - self.ctx_tiles) // self.lat_tiles_per_seq)

    def rope_block(self, i):
        return jnp.where(i < self.ctx_tiles, 0, 1 + (i - self.ctx_tiles) % self.lat_tiles_per_seq)


def _rope_tables(dec_seq):
    t = jnp.arange(dec_seq)
    row = (t // GRID_W).astype(F32)
    col = (t % GRID_W).astype(F32)
    inv = ROPE_THETA ** (-jnp.arange(ROT_PAIRS, dtype=F32) / ROT_PAIRS)
    ang = jnp.concatenate([row[:, None] * inv, col[:, None] * inv], axis=-1)
    cos = jnp.tile(jnp.concatenate([jnp.cos(ang), jnp.cos(ang)], axis=-1), (1, N_HEADS))
    sin = jnp.tile(jnp.concatenate([-jnp.sin(ang), jnp.sin(ang)], axis=-1), (1, N_HEADS))
    ident = jnp.ones((TOKEN_TILE, Q_DIM), F32)
    return (jnp.concatenate([ident, cos], axis=0),
            jnp.concatenate([jnp.zeros_like(ident), sin], axis=0))


def _block_diag_ones(width):
    i = jnp.arange(width)
    return (i[:, None] // HEAD_DIM == i[None, :] // HEAD_DIM).astype(BF)


def _layer_weights(l, norm1, norm2, w_in, conv_w, conv_b, q_norm, k_norm, a_log, dt_bias, d_skip,
                   ssm_norm, w_out, w_router):
    w = w_in[l].astype(BF)
    o = 0
    parts = {}
    for name, width in (("wq", Q_DIM), ("wkv", 2 * KV_DIM), ("wx", CONV_DIM),
                        ("wz", SSM_WIDTH), ("wdt", 2 * SSM_HEADS)):
        parts[name] = w[:, o:o + width]
        o += width
    wo = w_out[l].astype(BF)
    wr = w_router[l].T
    wrh = wr.astype(BF)
    nh2 = 2 * SSM_HEADS
    parts.update(
        wdtt=parts["wdt"].T,
        n1=norm1[l][None, :], n2=norm2[l][None, :],
        gq=jnp.tile(q_norm[l], N_HEADS)[None, :], gk=jnp.tile(k_norm[l], N_KV_HEADS)[None, :],
        conv_w=conv_w[l], conv_b=conv_b[l][None, :],
        alog_r=a_log[l].reshape(1, nh2), alog_c=a_log[l].reshape(nh2, 1),
        dtb_r=dt_bias[l].reshape(1, nh2), dtb_c=dt_bias[l].reshape(nh2, 1),
        dskip=jnp.repeat(d_skip[l], SSM_HEAD_DIM)[None, :], gn=ssm_norm[l][None, :],
        woa=wo[:Q_DIM], wos=wo[Q_DIM:],
        wrh=wrh, wrl=(wr - wrh.astype(F32)).astype(BF),
    )
    return parts


def _moe(h2, aff_t, x1, mod3, w_gate, w_up, w_down, layer, n_sets, geom):
    t = x1.shape[0]
    n = t // n_sets
    cap = EC_FACTOR * n // N_EXPERTS
    pairs = N_EXPERTS * cap
    gidx, gates, dest, start, end = _plan(aff_t, n_sets, cap)
    z = _expert_ffn(gidx.reshape(-1), dest.reshape(-1), h2, gates, w_gate, w_up, w_down, layer)
    tile_lo = start.reshape(t)[::TOKEN_TILE].astype(I32) // COMBINE_ROWS
    tile_hi = (end.reshape(t)[TOKEN_TILE - 1::TOKEN_TILE].astype(I32) + COMBINE_ROWS - 1) // COMBINE_ROWS
    lanes_major = lambda a: a.reshape(t // LANES, 1, LANES)
    return _combine(z, n_sets * pairs, lanes_major(start), lanes_major(end), tile_lo, tile_hi, x1, mod3, geom)


def kernel(x_prompt, x_sample, cache_k, cache_v, state_ssm, c, c_ctx, norm1, norm2, ada_w, ada_b,
           w_in, conv_w, conv_b, q_norm, k_norm, A_log, dt_bias, d_skip, ssm_norm, w_out,
           w_router, w_gate, w_up, w_down):
    batch, seq, d = x_prompt.shape
    dec_batch, dec_seq, _ = x_sample.shape
    depth = w_in.shape[0]
    past = cache_k.shape[2]
    geom = _Geometry(batch, seq, dec_batch, dec_seq)
    assert geom.n_ctx == geom.n_lat, "routed sets are stacked as two equal halves"

    rows = -(-(1 + dec_batch) // SUBLANES) * SUBLANES
    cv = jnp.zeros((rows, d), F32).at[0].set(c_ctx).at[1:1 + dec_batch].set(c)
    mod = _ada_modulation(cv, ada_w, ada_b)
    cos, sin = _rope_tables(dec_seq)
    consts = dict(cos=cos, sin=sin, bdq=_block_diag_ones(Q_DIM), bdk=_block_diag_ones(KV_DIM))
    cache_k4 = cache_k.reshape(dec_batch, depth, past, KV_DIM)
    cache_v4 = cache_v.reshape(dec_batch, depth, past, KV_DIM)

    x = (x_prompt.reshape(geom.n_ctx, d), x_sample.reshape(geom.n_lat, d))
    ks_new, vs_new, hs_new = [], [], []
    for l in range(depth):
        lw = _layer_weights(l, norm1, norm2, w_in, conv_w, conv_b, q_norm, k_norm, A_log, dt_bias,
                            d_skip, ssm_norm, w_out, w_router)
        mod3 = mod[l][:, None, :]
        q, k, v, xbc, z, dt, dtt = _in_projection(x, mod3, lw, consts, geom)
        attn_c, attn_l = _attention(q, k, v, cache_k4, cache_v4, l, geom)
        ssm_c, ssm_l, h_fin = _ssd_mixer(xbc, z, dt, dtt, state_ssm, l, lw, geom)
        x1, h2, aff_t = _out_projection(attn_c, attn_l, ssm_c, ssm_l, x, mod3, lw, geom)
        x = _moe(h2, aff_t, x1, mod3, w_gate, w_up, w_down, l, 2, geom)
        ks_new.append(k[:geom.n_ctx].reshape(batch, seq, N_KV_HEADS, HEAD_DIM))
        vs_new.append(v[:geom.n_ctx].reshape(batch, seq, N_KV_HEADS, HEAD_DIM))
        hs_new.append(h_fin)
    y_prompt = x[0].reshape(batch, seq, d)
    y_sample = x[1].reshape(dec_batch, dec_seq, d)
    return (y_prompt, y_sample, jnp.stack(ks_new, axis=1), jnp.stack(vs_new, axis=1),
            jnp.stack(hs_new, axis=1))
```

```python
import functools

import jax
import jax.numpy as jnp
from jax import lax
from jax.experimental import pallas as pl
from jax.experimental.pallas import tpu as pltpu

F32 = jnp.float32
BF = jnp.bfloat16
I32 = jnp.int32

GRID_W = 64
N_HEADS = 8
N_KV_HEADS = 2
HEAD_DIM = 64
GQA = N_HEADS // N_KV_HEADS
ROPE_THETA = 10000.0
ROT_PAIRS = HEAD_DIM // 4
Q_DIM = N_HEADS * HEAD_DIM
KV_DIM = N_KV_HEADS * HEAD_DIM
SSM_WIDTH = 512
SSM_HEADS = 8
SSM_HEAD_DIM = 64
SSM_GROUPS = 2
SSM_STATE = 64
CONV_W = 5
CONV_PAD = CONV_W // 2
CHUNK = 128
BC_DIM = SSM_GROUPS * SSM_STATE
CONV_DIM = SSM_WIDTH + 2 * BC_DIM
N_EXPERTS = 16
EC_FACTOR = 2
EPS = 1e-6
SCALE = HEAD_DIM ** -0.5
LOG2_E = 1.4426950408889634

LANES = 128
SUBLANES = 8
TOKEN_TILE = 512
QUERY_TILE = 512
HALO = SUBLANES
VMEM_LIMIT = 56 * 1024 * 1024

NT_DIMS = (((1,), (1,)), ((), ()))


def _dot(a, b):
    return jnp.dot(a, b, preferred_element_type=F32)


def _dot_nt(a, b):
    return lax.dot_general(a, b, NT_DIMS, preferred_element_type=F32)


def _bf16_pieces(v):
    p1 = v.astype(BF)
    r1 = v - p1.astype(F32)
    p2 = r1.astype(BF)
    return p1, p2, (r1 - p2.astype(F32)).astype(BF)


def _dot_f32_by_01(a, b, *, f32_side):
    if f32_side == 0:
        d1, d2, d3 = (_dot(p, b) for p in _bf16_pieces(a))
    else:
        d1, d2, d3 = (_dot(a, p) for p in _bf16_pieces(b))
    return d1 + (d2 + d3)


def _silu(x):
    return x / (1.0 + jnp.exp(-x))


def _softplus(x):
    return jnp.maximum(x, 0.0) + jnp.log(1.0 + jnp.exp(-jnp.abs(x)))


def _params(sem):
    return pltpu.CompilerParams(dimension_semantics=sem, vmem_limit_bytes=VMEM_LIMIT)


def _ada_kernel(cv_ref, w_ref, b_ref, o_ref):
    s = _silu(cv_ref[...]).astype(BF)
    o_ref[...] = _dot(s, w_ref[...].astype(BF)) + b_ref[...]


def _ada_modulation(cv, ada_w, ada_b):
    depth, d, n6 = ada_w.shape
    rows = cv.shape[0]
    tn = n6 // 4
    return pl.pallas_call(
        _ada_kernel,
        grid=(depth, n6 // tn),
        in_specs=[
            pl.BlockSpec((rows, d), lambda l, j: (0, 0)),
            pl.BlockSpec((None, d, tn), lambda l, j: (l, 0, j)),
            pl.BlockSpec((None, 1, tn), lambda l, j: (l, 0, j)),
        ],
        out_specs=pl.BlockSpec((None, rows, tn), lambda l, j: (l, 0, j)),
        out_shape=jax.ShapeDtypeStruct((depth, rows, n6), F32),
        compiler_params=_params(("parallel", "parallel")),
        name="ada_modulation",
    )(cv, ada_w, ada_b.reshape(depth, 1, n6))


def _inproj_kernel(xc_ref, xl_ref, mod_ref, n1_ref, wq_ref, wkv_ref, wx_ref, wz_ref, wdtt_ref,
                   bdq_ref, bdk_ref, gq_ref, gk_ref, cos_ref, sin_ref,
                   q_out, k_out, v_out, xbc_out, z_out, dt_out, dtt_out, *, ctx_tiles):
    d = xc_ref.shape[-1]
    x = jnp.where(pl.program_id(0) < ctx_tiles, xc_ref[...], xl_ref[...])
    mod = mod_ref[...]
    sh1 = mod[:, 0:d]
    sc1 = mod[:, d:2 * d]
    ms = jnp.mean(x * x, axis=-1, keepdims=True)
    h = x * lax.rsqrt(ms + EPS) * n1_ref[...]
    hb = (h * (1.0 + sc1) + sh1).astype(BF)
    cos = cos_ref[...]
    sin = sin_ref[...]

    def norm_rope(t, bd_ref, g_ref, cos_t, sin_t):
        ss = _dot((t * t).astype(BF), bd_ref[...])
        tn = t * lax.rsqrt(ss * (1.0 / HEAD_DIM) + EPS) * g_ref[...]
        width = t.shape[-1]
        lane = lax.broadcasted_iota(I32, tn.shape, 1)
        first_half = (lane % HEAD_DIM) < (HEAD_DIM // 2)
        partner = jnp.where(first_half,
                            pltpu.roll(tn, width - HEAD_DIM // 2, 1),
                            pltpu.roll(tn, HEAD_DIM // 2, 1))
        return tn * cos_t + partner * sin_t

    q = norm_rope(_dot(hb, wq_ref[...]), bdq_ref, gq_ref, cos, sin)
    q_out[...] = (q * (SCALE * LOG2_E)).astype(BF)
    kv = _dot(hb, wkv_ref[...])
    k_out[...] = norm_rope(kv[:, :KV_DIM], bdk_ref, gk_ref, cos[:, :KV_DIM], sin[:, :KV_DIM])
    v_out[...] = kv[:, KV_DIM:]
    xbc_out[...] = _dot(hb, wx_ref[...])
    z_out[...] = _dot(hb, wz_ref[...])
    dtt = _dot_nt(wdtt_ref[...], hb)
    dtt_out[...] = dtt
    padded = jnp.concatenate([dtt, jnp.zeros((LANES - dtt.shape[0], dtt.shape[1]), F32)], axis=0)
    dt_out[...] = padded.T[:, :dtt.shape[0]]


def _group_views(x, geom):
    tm = TOKEN_TILE
    nc = geom.ctx_tiles
    if isinstance(x, tuple):
        xc, xl = x
        lat_index = lambda i: (jnp.maximum(i - nc, 0), 0)
    else:
        xc = xl = x
        lat_index = lambda i: (jnp.maximum(i, nc), 0)
    d = xc.shape[-1]
    return xc, xl, pl.BlockSpec((tm, d), lambda i: (jnp.minimum(i, nc - 1), 0)), pl.BlockSpec((tm, d), lat_index)


def _in_projection(x, mod3, lw, consts, geom):
    xc, xl, xc_spec, xl_spec = _group_views(x, geom)
    t, d = geom.n_ctx + geom.n_lat, xc.shape[-1]
    tm = TOKEN_TILE
    full = lambda a: pl.BlockSpec(a.shape, lambda i: (0,) * a.ndim)
    row = lambda w: pl.BlockSpec((tm, w), lambda i: (i, 0))
    weights = [lw["wq"], lw["wkv"], lw["wx"], lw["wz"], lw["wdtt"],
               consts["bdq"], consts["bdk"], lw["gq"], lw["gk"]]
    out_shape = [
        jax.ShapeDtypeStruct((t, Q_DIM), BF),
        jax.ShapeDtypeStruct((t, KV_DIM), F32),
        jax.ShapeDtypeStruct((t, KV_DIM), F32),
        jax.ShapeDtypeStruct((t, CONV_DIM), F32),
        jax.ShapeDtypeStruct((t, SSM_WIDTH), F32),
        jax.ShapeDtypeStruct((t, 2 * SSM_HEADS), F32),
        jax.ShapeDtypeStruct((2 * SSM_HEADS, t), F32),
    ]
    return pl.pallas_call(
        functools.partial(_inproj_kernel, ctx_tiles=geom.ctx_tiles),
        grid=(t // tm,),
        in_specs=[xc_spec, xl_spec,
                  pl.BlockSpec((None, 1, mod3.shape[-1]), lambda i: (geom.mod_row(i), 0, 0)),
                  full(lw["n1"])]
                 + [full(w) for w in weights]
                 + [pl.BlockSpec((tm, Q_DIM), lambda i: (geom.rope_block(i), 0)),
                    pl.BlockSpec((tm, Q_DIM), lambda i: (geom.rope_block(i), 0))],
        out_specs=[row(Q_DIM), row(KV_DIM), row(KV_DIM), row(CONV_DIM), row(SSM_WIDTH),
                   row(2 * SSM_HEADS), pl.BlockSpec((2 * SSM_HEADS, tm), lambda i: (0, i))],
        out_shape=out_shape,
        compiler_params=_params(("parallel",)),
        name="in_projection",
    )(xc, xl, mod3, lw["n1"], *weights, consts["cos"], consts["sin"])


def _head_pair_operands(k, v):
    lane = lax.broadcasted_iota(I32, k.shape, 1)
    low = lane < HEAD_DIM
    k_sw = pltpu.roll(k, HEAD_DIM, 1)
    v_sw = pltpu.roll(v, HEAD_DIM, 1)
    ones_last = jnp.where(lane == LANES - 1, 1.0, 0.0)
    ones_first = jnp.where(lane == 0, 1.0, 0.0)
    kk = jnp.stack([jnp.where(low, k, k_sw), jnp.where(low, k_sw, k)]).astype(BF)
    v_lo = jnp.stack([jnp.where(low, v, ones_last), jnp.where(low, v_sw, ones_last)]).astype(BF)
    v_hi = jnp.stack([jnp.where(low, ones_first, v_sw), jnp.where(low, ones_first, v)]).astype(BF)
    return kk, v_lo, v_hi


def _attend(q_ref, o_ref, key_sets):
    tq = q_ref.shape[0]
    lane = lax.broadcasted_iota(I32, (tq, LANES), 1)
    low = lane < HEAD_DIM
    for pair in range(N_HEADS // 2):
        g = (2 * pair) // GQA
        q2 = q_ref[:, pair * LANES:(pair + 1) * LANES]
        zero = jnp.zeros_like(q2)
        accs = []
        for half, qh in enumerate((jnp.where(low, q2, zero), jnp.where(low, zero, q2))):
            scores = [_dot_nt(qh, ks[0][g]) for ks in key_sets]
            m = scores[0].max(axis=-1, keepdims=True)
            for s in scores[1:]:
                m = jnp.maximum(m, s.max(axis=-1, keepdims=True))
            acc = jnp.zeros((tq, LANES), F32)
            for s, ks in zip(scores, key_sets):
                acc = acc + _dot(jnp.exp2(s - m).astype(BF), ks[1 + half][g])
            accs.append(acc)
        out = jnp.where(low, accs[0] * (1.0 / accs[0][:, LANES - 1:LANES]), accs[1] * (1.0 / accs[1][:, 0:1]))
        o_ref[:, pair * LANES:(pair + 1) * LANES] = out.astype(BF)


def _attn_ctx_kernel(q_ref, k_ref, v_ref, o_ref):
    _attend(q_ref, o_ref, [_head_pair_operands(k_ref[...], v_ref[...])])


def _attn_lat_kernel(q_ref, kc_ref, vc_ref, kl_ref, vl_ref, o_ref, *ops):
    @pl.when(pl.program_id(1) == 0)
    def _():
        for dst, src in zip(ops, _head_pair_operands(kc_ref[...], vc_ref[...])
                            + _head_pair_operands(kl_ref[...], vl_ref[...])):
            dst[...] = src

    _attend(q_ref, o_ref, [ops[0:3], ops[3:6]])


def _attention(q, k, v, cache_k4, cache_v4, layer, geom):
    seq = geom.seq
    o_ctx = pl.pallas_call(
        _attn_ctx_kernel,
        grid=(geom.batch,),
        in_specs=[pl.BlockSpec((seq, Q_DIM), lambda b: (b, 0)),
                  pl.BlockSpec((seq, KV_DIM), lambda b: (b, 0)),
                  pl.BlockSpec((seq, KV_DIM), lambda b: (b, 0))],
        out_specs=pl.BlockSpec((seq, Q_DIM), lambda b: (b, 0)),
        out_shape=jax.ShapeDtypeStruct((geom.n_ctx, Q_DIM), BF),
        compiler_params=_params(("parallel",)),
        name="attention_context",
    )(q, k, v)

    tq = QUERY_TILE
    dseq = geom.dec_seq
    past = cache_k4.shape[2]
    q_tiles = dseq // tq
    ctx_tiles = geom.n_ctx // tq
    ctx_seqs = geom.n_ctx // dseq
    o_lat = pl.pallas_call(
        _attn_lat_kernel,
        grid=(geom.dec_batch, q_tiles),
        in_specs=[pl.BlockSpec((tq, Q_DIM), lambda b, i: (ctx_tiles + b * q_tiles + i, 0)),
                  pl.BlockSpec((None, None, past, KV_DIM), lambda b, i: (b, layer, 0, 0)),
                  pl.BlockSpec((None, None, past, KV_DIM), lambda b, i: (b, layer, 0, 0)),
                  pl.BlockSpec((dseq, KV_DIM), lambda b, i: (ctx_seqs + b, 0)),
                  pl.BlockSpec((dseq, KV_DIM), lambda b, i: (ctx_seqs + b, 0))],
        out_specs=pl.BlockSpec((tq, Q_DIM), lambda b, i: (b * q_tiles + i, 0)),
        out_shape=jax.ShapeDtypeStruct((geom.n_lat, Q_DIM), BF),
        scratch_shapes=[pltpu.VMEM((N_KV_HEADS, s, KV_DIM), BF) for s in (past, past, past, dseq, dseq, dseq)],
        compiler_params=_params(("parallel", "arbitrary")),
        name="attention_latent",
    )(q, cache_k4, cache_v4, k, v)
    return o_ctx, o_lat


def _ssd_kernel(*refs, n_chunks, has_h0, emit_state):
    it = iter(refs)
    xbc_ref, z_ref, dt_ref, dtt_ref = next(it), next(it), next(it), next(it)
    h0_ref = next(it) if has_h0 else None
    (cw_ref, cb_ref, alog_r_ref, alog_c_ref, dtb_r_ref, dtb_c_ref,
     dskip_ref, gn_ref) = (next(it) for _ in range(8))
    y_out = next(it)
    hfin_out = next(it) if emit_state else None
    pad_ref, u_ref, csc_ref, csr_ref, dtr_ref, tot_ref, st_ref, hcur_ref, y_ref = (next(it) for _ in range(9))

    seq = xbc_ref.shape[0]
    nh = SSM_HEADS
    q = CHUNK

    pad_ref[0:HALO, :] = jnp.zeros((HALO, CONV_DIM), F32)
    pad_ref[HALO:HALO + seq, :] = xbc_ref[...]
    pad_ref[HALO + seq:2 * HALO + seq, :] = jnp.zeros((HALO, CONV_DIM), F32)

    a_row = -jnp.exp(alog_r_ref[...])
    a_col = -jnp.exp(alog_c_ref[...])
    ii = lax.broadcasted_iota(I32, (q, q), 0)
    jj = lax.broadcasted_iota(I32, (q, q), 1)
    lower = jj <= ii
    upper = jj >= ii
    low_half = jj < SSM_STATE
    tri_lower = lower.astype(BF)
    tri_upper = upper.astype(BF)
    fwd_lane = lax.broadcasted_iota(I32, (q, 2 * nh), 1) < nh
    fwd_row = lax.broadcasted_iota(I32, (2 * nh, q), 0) < nh

    def chunk_stats(c, carry):
        base = pl.multiple_of(c * q, q)
        win = pad_ref[pl.ds(base, q + 2 * HALO), :]
        acc = jnp.zeros((q, CONV_DIM), F32) + cb_ref[...]
        for tap in range(CONV_W):
            off = HALO - CONV_PAD + tap
            acc = acc + win[off:off + q, :] * cw_ref[tap:tap + 1, :]
        u = _silu(acc)
        u_ref[pl.ds(base, q), :] = u

        dtc = _softplus(dt_ref[pl.ds(base, q), :] + dtb_r_ref[...])
        dtr = _softplus(dtt_ref[:, pl.ds(base, q)] + dtb_c_ref[...])
        ac = dtc * a_row
        ar = dtr * a_col
        cs_col = jnp.where(fwd_lane, _dot_f32_by_01(tri_lower, ac, f32_side=1),
                           _dot_f32_by_01(tri_upper, ac, f32_side=1))
        cs_row = jnp.where(fwd_row, _dot_f32_by_01(ar, tri_upper, f32_side=0),
                           _dot_f32_by_01(ar, tri_lower, f32_side=0))
        csc_ref[pl.ds(base, q), :] = cs_col
        csr_ref[:, pl.ds(base, q)] = cs_row
        dtr_ref[:, pl.ds(base, q)] = dtr
        total = jnp.where(fwd_lane[0:1, :], cs_col[q - 1:q, :], cs_col[0:1, :])
        tot_ref[c] = total
        total_c = jnp.where(fwd_row[:, 0:1], cs_row[:, q - 1:q], cs_row[:, 0:1])
        w_end = jnp.exp(total_c - cs_row) * dtr

        xs_b = u[:, 0:SSM_WIDTH].astype(BF)
        b_t = u[:, SSM_WIDTH:SSM_WIDTH + BC_DIM].T
        for dh in range(2 * nh):
            head = dh % nh
            grp = head // (nh // SSM_GROUPS)
            bw = (b_t[grp * SSM_STATE:(grp + 1) * SSM_STATE, :] * w_end[dh:dh + 1, :]).astype(BF)
            st_ref[c, dh] = _dot(bw, xs_b[:, head * SSM_HEAD_DIM:(head + 1) * SSM_HEAD_DIM])
        return carry

    lax.fori_loop(0, n_chunks, chunk_stats, 0)

    eye = (lax.broadcasted_iota(I32, (SSM_STATE, SSM_STATE), 0)
           == lax.broadcasted_iota(I32, (SSM_STATE, SSM_STATE), 1)).astype(BF)

    def transposed_blocks(stack):
        p1, p2, p3 = _bf16_pieces(stack)
        return _dot_nt(eye, p1) + (_dot_nt(eye, p2) + _dot_nt(eye, p3))

    if has_h0:
        h0_t = transposed_blocks(h0_ref[...].reshape(2 * nh * SSM_HEAD_DIM, SSM_STATE))
    for dh in range(2 * nh):
        hcur_ref[dh] = (h0_t[:, dh * SSM_HEAD_DIM:(dh + 1) * SSM_HEAD_DIM] if has_h0
                        else jnp.zeros((SSM_STATE, SSM_HEAD_DIM), F32))

    def recur(step, carry):
        for direction in range(2):
            c = step if direction == 0 else n_chunks - 1 - step
            decay = jnp.exp(tot_ref[c])
            for head in range(nh):
                dh = direction * nh + head
                new = st_ref[c, dh]
                cur = hcur_ref[dh]
                st_ref[c, dh] = cur
                hcur_ref[dh] = cur * decay[:, dh:dh + 1] + new
        return carry

    lax.fori_loop(0, n_chunks, recur, 0)
    if emit_state:
        h_t = transposed_blocks(hcur_ref[...].reshape(2 * nh * SSM_STATE, SSM_HEAD_DIM))
        for dh in range(2 * nh):
            hfin_out[dh // nh, dh % nh] = h_t[:, dh * SSM_STATE:(dh + 1) * SSM_STATE]

    def chunk_out(c, carry):
        base = pl.multiple_of(c * q, q)
        u = u_ref[pl.ds(base, q), :]
        xs = u[:, 0:SSM_WIDTH]
        xs_b = xs.astype(BF)
        bmat = u[:, SSM_WIDTH:SSM_WIDTH + BC_DIM].astype(BF)
        cmat = u[:, SSM_WIDTH + BC_DIM:CONV_DIM]
        cmat_b = cmat.astype(BF)
        cs_col = csc_ref[pl.ds(base, q), :]
        cs_row = csr_ref[:, pl.ds(base, q)]
        dtr = dtr_ref[:, pl.ds(base, q)]
        c_sw = pltpu.roll(cmat, SSM_STATE, 1)
        for grp in range(SSM_GROUPS):
            gs = slice(grp * SSM_STATE, (grp + 1) * SSM_STATE)
            cb = _dot_nt(cmat_b[:, gs], bmat[:, gs])
            c_dup = jnp.where(low_half, cmat, c_sw) if grp == 0 else jnp.where(low_half, c_sw, cmat)
            for head in range(grp * (nh // SSM_GROUPS), (grp + 1) * (nh // SSM_GROUPS)):
                hb = nh + head
                col_f = jnp.broadcast_to(cs_col[:, head:head + 1], (q, q))
                col_b = jnp.broadcast_to(cs_col[:, hb:hb + 1], (q, q))
                seg_f = jnp.where(lower, col_f - cs_row[head:head + 1, :], -1e30)
                seg_b = jnp.where(upper, col_b - cs_row[hb:hb + 1, :], -1e30)
                mix = cb * (jnp.exp(seg_f) * dtr[head:head + 1, :] + jnp.exp(seg_b) * dtr[hb:hb + 1, :])
                carry_in = c_dup * jnp.exp(jnp.where(low_half, col_f, col_b))
                hs = slice(head * SSM_HEAD_DIM, (head + 1) * SSM_HEAD_DIM)
                lhs = jnp.concatenate([mix.astype(BF), carry_in.astype(BF)], axis=1)
                rhs = jnp.concatenate([xs_b[:, hs], st_ref[c, head].astype(BF), st_ref[c, hb].astype(BF)], axis=0)
                y_ref[:, hs] = _dot(lhs, rhs)
        y = y_ref[...] + xs * dskip_ref[...]
        y = y * _silu(z_ref[pl.ds(base, q), :])
        ms = jnp.mean(y * y, axis=-1, keepdims=True)
        y_out[pl.ds(base, q), :] = (y * lax.rsqrt(ms + EPS) * gn_ref[...]).astype(BF)
        return carry

    lax.fori_loop(0, n_chunks, chunk_out, 0)


def _ssd_call(xbc, z, dt, dtt, h0, layer, lw, *, seq, n_seq, row_block0, emit_state, name):
    n_chunks = seq // CHUNK
    nh2 = 2 * SSM_HEADS
    full = lambda a: pl.BlockSpec(a.shape, lambda b: (0,) * a.ndim)
    params = [lw["conv_w"], lw["conv_b"], lw["alog_r"], lw["alog_c"], lw["dtb_r"], lw["dtb_c"],
              lw["dskip"], lw["gn"]]
    in_specs = [pl.BlockSpec((seq, CONV_DIM), lambda b: (row_block0 + b, 0)),
                pl.BlockSpec((seq, SSM_WIDTH), lambda b: (row_block0 + b, 0)),
                pl.BlockSpec((seq, nh2), lambda b: (row_block0 + b, 0)),
                pl.BlockSpec((nh2, seq), lambda b: (0, row_block0 + b))]
    args = [xbc, z, dt, dtt]
    if h0 is not None:
        in_specs.append(pl.BlockSpec((None, None, 2, SSM_HEADS, SSM_HEAD_DIM, SSM_STATE),
                                     lambda b: (b, layer, 0, 0, 0, 0)))
        args.append(h0)
    in_specs += [full(p) for p in params]
    args += params
    out_specs = [pl.BlockSpec((seq, SSM_WIDTH), lambda b: (b, 0))]
    out_shape = [jax.ShapeDtypeStruct((n_seq * seq, SSM_WIDTH), BF)]
    if emit_state:
        out_specs.append(pl.BlockSpec((None, 2, SSM_HEADS, SSM_HEAD_DIM, SSM_STATE),
                                      lambda b: (b, 0, 0, 0, 0)))
        out_shape.append(jax.ShapeDtypeStruct((n_seq, 2, SSM_HEADS, SSM_HEAD_DIM, SSM_STATE), F32))
    scratch = [
        pltpu.VMEM((seq + 2 * HALO, CONV_DIM), F32),
        pltpu.VMEM((seq, CONV_DIM), F32),
        pltpu.VMEM((seq, nh2), F32),
        pltpu.VMEM((nh2, seq), F32),
        pltpu.VMEM((nh2, seq), F32),
        pltpu.VMEM((n_chunks, 1, nh2), F32),
        pltpu.VMEM((n_chunks, nh2, SSM_HEAD_DIM, SSM_STATE), F32),
        pltpu.VMEM((nh2, SSM_HEAD_DIM, SSM_STATE), F32),
        pltpu.VMEM((CHUNK, SSM_WIDTH), F32),
    ]
    return pl.pallas_call(
        functools.partial(_ssd_kernel, n_chunks=n_chunks, has_h0=h0 is not None, emit_state=emit_state),
        grid=(n_seq,),
        in_specs=in_specs,
        out_specs=out_specs,
        out_shape=out_shape,
        scratch_shapes=scratch,
        compiler_params=_params(("parallel",)),
        name=name,
    )(*args)


def _ssd_mixer(xbc, z, dt, dtt, state_ssm, layer, lw, geom):
    y_ctx, h_fin = _ssd_call(xbc, z, dt, dtt, None, layer, lw, seq=geom.seq, n_seq=geom.batch,
                             row_block0=0, emit_state=True, name="ssd_context")
    (y_lat,) = _ssd_call(xbc, z, dt, dtt, state_ssm, layer, lw, seq=geom.dec_seq,
                         n_seq=geom.dec_batch, row_block0=geom.n_ctx // geom.dec_seq,
                         emit_state=False, name="ssd_latent")
    return y_ctx, y_lat, h_fin


def _outproj_kernel(ac_ref, al_ref, sc_ref, sl_ref, xc_ref, xl_ref, mod_ref, n2_ref, woa_ref, wos_ref,
                    wrh_ref, wrl_ref, x1_out, h2_out, afft_out, *, ctx_tiles):
    d = xc_ref.shape[-1]
    mod = mod_ref[...]
    g1 = mod[:, 2 * d:3 * d]
    sh2 = mod[:, 3 * d:4 * d]
    sc2 = mod[:, 4 * d:5 * d]
    is_ctx = pl.program_id(0) < ctx_tiles
    attn = jnp.where(is_ctx, ac_ref[...], al_ref[...])
    ssm = jnp.where(is_ctx, sc_ref[...], sl_ref[...])
    y = _dot(attn, woa_ref[...]) + _dot(ssm, wos_ref[...])
    x1 = jnp.where(is_ctx, xc_ref[...], xl_ref[...]) + g1 * y
    x1_out[...] = x1
    ms = jnp.mean(x1 * x1, axis=-1, keepdims=True)
    h2 = x1 * lax.rsqrt(ms + EPS) * n2_ref[...]
    h2 = h2 * (1.0 + sc2) + sh2
    h2_out[...] = h2
    hh = h2.astype(BF)
    hl = (h2 - hh.astype(F32)).astype(BF)
    wh = wrh_ref[...]
    logits = _dot_nt(wh, hh) + _dot_nt(wh, hl) + _dot_nt(wrl_ref[...], hh)
    logits = logits - logits.max(axis=0, keepdims=True)
    e = jnp.exp(logits)
    afft_out[...] = e / e.sum(axis=0, keepdims=True)


def _out_projection(attn_c, attn_l, ssm_c, ssm_l, x, mod3, lw, geom):
    xc, xl, xc_spec, xl_spec = _group_views(x, geom)
    t, d = geom.n_ctx + geom.n_lat, xc.shape[-1]
    tm = TOKEN_TILE
    nc = geom.ctx_tiles
    full = lambda a: pl.BlockSpec(a.shape, lambda i: (0,) * a.ndim)
    row = lambda w: pl.BlockSpec((tm, w), lambda i: (i, 0))
    ctx_row = lambda w: pl.BlockSpec((tm, w), lambda i: (jnp.minimum(i, nc - 1), 0))
    lat_row = lambda w: pl.BlockSpec((tm, w), lambda i: (jnp.maximum(i - nc, 0), 0))
    weights = [lw["n2"], lw["woa"], lw["wos"], lw["wrh"], lw["wrl"]]
    return pl.pallas_call(
        functools.partial(_outproj_kernel, ctx_tiles=nc),
        grid=(t // tm,),
        in_specs=[ctx_row(Q_DIM), lat_row(Q_DIM), ctx_row(SSM_WIDTH), lat_row(SSM_WIDTH), xc_spec, xl_spec,
                  pl.BlockSpec((None, 1, mod3.shape[-1]), lambda i: (geom.mod_row(i), 0, 0))]
                 + [full(w) for w in weights],
        out_specs=[row(d), row(d), pl.BlockSpec((N_EXPERTS, tm), lambda i: (0, i))],
        out_shape=[jax.ShapeDtypeStruct((t, d), F32), jax.ShapeDtypeStruct((t, d), F32),
                   jax.ShapeDtypeStruct((N_EXPERTS, t), F32)],
        compiler_params=_params(("parallel",)),
        name="out_projection",
    )(attn_c, attn_l, ssm_c, ssm_l, xc, xl, mod3, *weights)


def _flat_cumsum(m2, rows):
    gr = m2.shape[0]
    li = lax.broadcasted_iota(I32, (LANES, LANES), 0)
    lj = lax.broadcasted_iota(I32, (LANES, LANES), 1)
    mb = m2.astype(BF)
    within = _dot(mb, (li <= lj).astype(BF))
    row_tot = _dot(mb, jnp.ones((LANES, LANES), BF))
    ri = lax.broadcasted_iota(I32, (gr, gr), 0)
    rj = lax.broadcasted_iota(I32, (gr, gr), 1)
    before = ((ri // rows == rj // rows) & (rj < ri)).astype(BF)
    hi, lo = _split_base128(row_tot)
    return within + (LANES * _dot(before, hi) + _dot(before, lo))


def _split_base128(v):
    hi = jnp.floor(v * (1.0 / LANES))
    return hi.astype(BF), (v - hi * LANES).astype(BF)


def _plan_kernel(aff_ref, idx_out, gate_out, dslot_out, start_out, end_out, *, cap):
    a = aff_ref[...]
    n_e, rows, _ = a.shape

    def count(hit):
        return hit.astype(I32).sum(axis=1, keepdims=True).sum(axis=2, keepdims=True)

    def kth_largest_bits(values, member, k):
        def search(step, lo):
            cand = lo | (1 << (30 - step))
            return jnp.where(count(member & (values >= lax.bitcast_convert_type(cand, F32))) >= k, cand, lo)
        return lax.fori_loop(0, 31, search, jnp.zeros((n_e, 1, 1), I32))

    everyone = jnp.full(a.shape, True)
    thr_bits = kth_largest_bits(a, everyone, cap)
    thr = lax.bitcast_convert_type(thr_bits, F32)
    above = a >= lax.bitcast_convert_type(thr_bits + 1, F32)
    cell = (a >= thr) & jnp.logical_not(above)
    need = cap - count(above)
    resid = jnp.where(cell, a - thr, 0.0)
    thr2 = lax.bitcast_convert_type(kth_largest_bits(resid, cell, need), F32)
    gt = above | (cell & (resid > thr2))
    eq = cell & (resid == thr2)
    need = cap - count(gt)
    eqf = eq.astype(F32).reshape(n_e * rows, LANES)
    rank = (_flat_cumsum(eqf, rows) - eqf).reshape(n_e, rows, LANES)
    mask = gt | (eq & (rank < need.astype(F32)))
    mf = mask.astype(F32)

    c3 = _flat_cumsum(mf.reshape(n_e * rows, LANES), rows).reshape(n_e, rows, LANES)
    ahead = []
    run = jnp.zeros((rows, LANES), F32)
    for e in range(n_e):
        ahead.append(run)
        run = run + mf[e]
    cnt = run
    start = _flat_cumsum(cnt, rows) - cnt
    set_id = pl.program_id(0)
    row0 = (set_id * (n_e * cap)).astype(F32)
    start_out[...] = start + row0
    end_out[...] = start + cnt + row0

    s_lane = lax.broadcasted_iota(I32, (1, cap), 1).astype(F32)
    r_col = lax.broadcasted_iota(I32, (LANES, cap), 0).astype(F32)
    pad = jnp.zeros((LANES - rows, LANES), F32)

    def lanes_first(xm):
        return (jnp.concatenate([xm, pad], axis=0) if rows < LANES else xm).T

    for e in range(n_e):
        c_e = c3[e]
        row_of = (c_e[:, LANES - 1:LANES] <= s_lane).astype(F32).sum(axis=0, keepdims=True)
        onehot = (r_col == row_of).astype(BF)
        pieces = (_split_base128(lanes_first(c_e)) + _bf16_pieces(lanes_first(a[e]))
                  + _split_base128(lanes_first(start + ahead[e])))
        sel = _dot(jnp.concatenate(pieces, axis=0), onehot)
        part = [sel[j * LANES:(j + 1) * LANES] for j in range(len(pieces))]
        sel_c = LANES * part[0] + part[1]
        sel_a = part[2] + (part[3] + part[4])
        sel_d = LANES * part[5] + part[6]
        lane_of = (sel_c <= s_lane).astype(F32).sum(axis=0, keepdims=True)
        pick = r_col == lane_of
        idx_out[e:e + 1, :] = (row_of * LANES + lane_of).astype(I32) + set_id * (rows * LANES)
        gate_out[e:e + 1, :] = jnp.where(pick, sel_a, 0.0).sum(axis=0, keepdims=True)
        dslot_out[e:e + 1, :] = (jnp.where(pick, sel_d, 0.0).sum(axis=0, keepdims=True).astype(I32)
                                 + set_id * (n_e * cap))


def _plan(aff_t, n_sets, cap):
    n_e, t = aff_t.shape
    n = t // n_sets
    rows = n // LANES
    assert rows <= LANES
    aff4 = aff_t.reshape(n_e, n_sets, rows, LANES)
    slot_spec = pl.BlockSpec((n_e, cap), lambda s: (0, s))
    tok_spec = pl.BlockSpec((None, rows, LANES), lambda s: (s, 0, 0))
    slot_shape = lambda dt: jax.ShapeDtypeStruct((n_e, n_sets * cap), dt)
    tok_shape = jax.ShapeDtypeStruct((n_sets, rows, LANES), F32)
    return pl.pallas_call(
        functools.partial(_plan_kernel, cap=cap),
        grid=(n_sets,),
        in_specs=[pl.BlockSpec((n_e, None, rows, LANES), lambda s: (0, s, 0, 0))],
        out_specs=[slot_spec, slot_spec, slot_spec, tok_spec, tok_spec],
        out_shape=[slot_shape(I32), slot_shape(F32), slot_shape(I32), tok_shape, tok_shape],
        compiler_params=_params(("parallel",)),
        name="expert_choice_plan",
    )(aff4)


DMA_UNROLL = 8
ROW_BLOCK = 512
EXPERT_FF_TILE = 512
COMBINE_ROWS = 256
COMBINE_BUFFERS = 4


def _expert_kernel(gidx_ref, dest_ref, h2_hbm, gate_ref, wg_ref, wu_ref, wd_ref, z_hbm,
                   xw_a, xw_b, xb_ref, acc_a, acc_b, wgb_ref, wub_ref, wdb_ref, gcol_ref, sem_g, sem_s,
                   *, n_f, row_block):
    e = pl.program_id(0)
    f = pl.program_id(1)
    n_e = pl.num_programs(0)
    m_rows = xw_a.shape[0]
    n_blocks = m_rows // row_block
    per_block = m_rows // (n_f * n_blocks)

    def vmem_row(buf, m8, k):
        tile = pl.multiple_of(m8 + (k // SUBLANES) * SUBLANES, SUBLANES)
        return buf.at[pl.ds(tile, SUBLANES), :].at[pl.ds(k % SUBLANES, 1), :]

    def gather_copy(entry, m8, k, xw):
        return pltpu.make_async_copy(h2_hbm.at[pl.ds(gidx_ref[entry], 1), :], vmem_row(xw, m8, k), sem_g)

    def scatter_copy(entry, m8, k, acc):
        return pltpu.make_async_copy(vmem_row(acc, m8, k), z_hbm.at[pl.ds(dest_ref[entry], 1), :], sem_s)

    def wait_gathers(xw):
        pltpu.make_async_copy(h2_hbm.at[pl.ds(0, m_rows), :], xw, sem_g).wait()

    def wait_scatters(acc):
        pltpu.make_async_copy(acc, z_hbm.at[pl.ds(0, m_rows), :], sem_s).wait()

    def issue_all(make):
        def body(j, carry):
            for k in range(DMA_UNROLL):
                make(j * DMA_UNROLL, k).start()
            return carry
        lax.fori_loop(0, m_rows // DMA_UNROLL, body, 0)

    def run(xw_cur, xw_next, acc_cur, acc_prev):
        @pl.when((e == 0) & (f == 0))
        def _():
            acc_prev[...] = jnp.zeros_like(acc_prev)
            issue_all(lambda m8, k: gather_copy(m8 + k, m8, k, xw_cur))

        @pl.when(f == 0)
        def _():
            wait_gathers(xw_cur)
            xb_ref[...] = xw_cur[...].astype(BF)
            eye = (lax.broadcasted_iota(I32, (LANES, LANES), 0)
                   == lax.broadcasted_iota(I32, (LANES, LANES), 1)).astype(BF)
            g1, g2, g3 = _bf16_pieces(gate_ref[...])
            cols = _dot_nt(eye, g1) + (_dot_nt(eye, g2) + _dot_nt(eye, g3))
            for j in range(gate_ref.shape[0]):
                gcol_ref[j * LANES:(j + 1) * LANES, :] = cols[:, j:j + 1]

        wgb_ref[...] = wg_ref[...].astype(BF)
        wub_ref[...] = wu_ref[...].astype(BF)
        wdb_ref[...] = wd_ref[...].astype(BF)
        last = f == n_f - 1

        def block(rb, carry, *, is_first, is_last):
            r0 = pl.multiple_of(rb * row_block, row_block)
            m0 = (f * n_blocks + rb) * per_block

            for k in range(per_block):
                gather_copy((e + 1) * m_rows + m0 + k, m0, k, xw_next).start()
                scatter_copy(e * m_rows + m0 + k, m0, k, acc_prev).start()
            x = xb_ref[pl.ds(r0, row_block), :]
            g = _dot(x, wgb_ref[...])
            u = _dot(x, wub_ref[...])
            hid = (_silu(g) * u).astype(BF)
            total = _dot(hid, wdb_ref[...])
            if not is_first:
                total = acc_cur[pl.ds(r0, row_block), :] + total
            if is_last:
                total = total * gcol_ref[pl.ds(r0, row_block), :]
            acc_cur[pl.ds(r0, row_block), :] = total
            return carry

        for is_first, is_last in ((True, n_f == 1), (False, False), (False, True)):
            if (n_f == 1 and not is_first) or (n_f == 2 and not (is_first or is_last)):
                continue
            step = (f == 0) if is_first else (last if is_last else (f > 0) & jnp.logical_not(last))

            @pl.when(step)
            def _(is_first=is_first, is_last=is_last):
                lax.fori_loop(0, n_blocks, functools.partial(block, is_first=is_first, is_last=is_last), 0)

        @pl.when(last)
        def _():
            wait_scatters(acc_prev)

        @pl.when(last & (e == n_e - 1))
        def _():
            wait_gathers(xw_next)
            issue_all(lambda m8, k: scatter_copy((e + 1) * m_rows + m8 + k, m8, k, acc_cur))
            wait_scatters(acc_cur)

    @pl.when(e % 2 == 0)
    def _():
        run(xw_a, xw_b, acc_a, acc_b)

    @pl.when(e % 2 == 1)
    def _():
        run(xw_b, xw_a, acc_b, acc_a)


def _expert_ffn(gidx, dest, h2, gates, w_gate, w_up, w_down, layer):
    n_e, m = gates.shape
    gates = gates.reshape(n_e, m // LANES, LANES)
    d = h2.shape[1]
    ff = w_gate.shape[-1]
    tf = min(EXPERT_FF_TILE, ff)
    n_f = ff // tf
    row_block = min(ROW_BLOCK, m)
    assert m % DMA_UNROLL == 0 and (n_e * m) % COMBINE_ROWS == 0 and m % (n_f * (m // row_block)) == 0
    gidx_ext = jnp.concatenate([gidx, jnp.arange(m, dtype=I32)])
    dest_ext = jnp.concatenate([n_e * m + jnp.arange(m, dtype=I32), dest])
    return pl.pallas_call(
        functools.partial(_expert_kernel, n_f=n_f, row_block=row_block),
        grid_spec=pltpu.PrefetchScalarGridSpec(
            num_scalar_prefetch=2,
            grid=(n_e, n_f),
            in_specs=[pl.BlockSpec(memory_space=pl.ANY),
                      pl.BlockSpec((None, m // LANES, LANES), lambda e, f, gi, de: (e, 0, 0)),
                      pl.BlockSpec((None, None, d, tf), lambda e, f, gi, de: (layer, e, 0, f)),
                      pl.BlockSpec((None, None, d, tf), lambda e, f, gi, de: (layer, e, 0, f)),
                      pl.BlockSpec((None, None, tf, d), lambda e, f, gi, de: (layer, e, f, 0))],
            out_specs=pl.BlockSpec(memory_space=pl.ANY),
            scratch_shapes=[pltpu.VMEM((m, d), F32), pltpu.VMEM((m, d), F32),
                            pltpu.VMEM((m, d), BF), pltpu.VMEM((m, d), F32), pltpu.VMEM((m, d), F32),
                            pltpu.VMEM((d, tf), BF), pltpu.VMEM((d, tf), BF), pltpu.VMEM((tf, d), BF),
                            pltpu.VMEM((m, 1), F32),
                            pltpu.SemaphoreType.DMA, pltpu.SemaphoreType.DMA]),
        out_shape=jax.ShapeDtypeStruct(((n_e + 1) * m, d), F32),
        compiler_params=_params(("arbitrary", "arbitrary")),
        name="expert_swiglu",
    )(gidx_ext, dest_ext, h2, gates, w_gate, w_up, w_down)


def _combine_kernel(lo_ref, hi_ref, z_hbm, start_ref, end_ref, x1_ref, mod_ref, oc_ref, ol_ref, zbuf, acc_ref,
                    col_ref, sem, state, *, n_chunks, ctx_tiles):
    i = pl.program_id(0)
    d = x1_ref.shape[-1]
    n_buf, ch, _ = zbuf.shape
    lo = lo_ref[i]
    hi = hi_ref[i]

    def chunk_copy(kc):
        slot = kc % n_buf
        return pltpu.make_async_copy(z_hbm.at[pl.ds(pl.multiple_of(kc * ch, ch), ch), :], zbuf.at[slot],
                                     sem.at[slot])

    @pl.when(i == 0)
    def _():
        state[0] = 0
        state[1] = 0

    acc_ref[...] = jnp.zeros_like(acc_ref)
    tm = acc_ref.shape[0]
    blocks = tm // LANES
    rows = jnp.concatenate([start_ref[i * blocks + j] for j in range(blocks)]
                           + [end_ref[i * blocks + j] for j in range(blocks)], axis=0)
    eye = (lax.broadcasted_iota(I32, (LANES, LANES), 0) == lax.broadcasted_iota(I32, (LANES, LANES), 1)).astype(BF)
    upper = jnp.floor(rows * (1.0 / 256.0))
    cols = 256.0 * _dot_nt(eye, upper.astype(BF)) + _dot_nt(eye, (rows - 256.0 * upper).astype(BF))
    for j in range(blocks):
        col_ref[j * LANES:(j + 1) * LANES, 0:1] = cols[:, j:j + 1]
        col_ref[j * LANES:(j + 1) * LANES, 1:2] = cols[:, blocks + j:blocks + j + 1]
    start = col_ref[:, 0:1]
    end = col_ref[:, 1:2]

    def body(kc, carry):
        upto = jnp.minimum(kc + n_buf, n_chunks)

        def start_one(j, c):
            chunk_copy(j).start()
            return c

        lax.fori_loop(state[0], upto, start_one, 0)
        state[0] = jnp.maximum(state[0], upto)

        @pl.when(kc >= state[1])
        def _():
            chunk_copy(kc).wait()
            state[1] = kc + 1

        slot = kc % n_buf
        k = (kc * ch + lax.broadcasted_iota(I32, (1, ch), 1)).astype(F32)
        own = jnp.where((k >= start) & (k < end), 1.0, 0.0).astype(BF)
        acc_ref[...] += _dot(own, zbuf[slot].astype(BF))
        return carry

    lax.fori_loop(lo, hi, body, 0)
    g2 = mod_ref[...][:, 5 * d:6 * d]
    out = x1_ref[...] + g2 * acc_ref[...]

    @pl.when(i < ctx_tiles)
    def _():
        oc_ref[...] = out

    @pl.when(i >= ctx_tiles)
    def _():
        ol_ref[...] = out


def _combine(z, n_rows, start, end, tile_lo, tile_hi, x1, mod3, geom):
    t, d = x1.shape
    tm = TOKEN_TILE
    nc = geom.ctx_tiles
    whole = lambda a: pl.BlockSpec(a.shape, lambda i, lo, hi: (0,) * a.ndim)
    return pl.pallas_call(
        functools.partial(_combine_kernel, n_chunks=n_rows // COMBINE_ROWS, ctx_tiles=nc),
        grid_spec=pltpu.PrefetchScalarGridSpec(
            num_scalar_prefetch=2,
            grid=(t // tm,),
            in_specs=[pl.BlockSpec(memory_space=pl.ANY), whole(start), whole(end),
                      pl.BlockSpec((tm, d), lambda i, lo, hi: (i, 0)),
                      pl.BlockSpec((None, 1, mod3.shape[-1]), lambda i, lo, hi: (geom.mod_row(i), 0, 0))],
            out_specs=[pl.BlockSpec((tm, d), lambda i, lo, hi: (jnp.minimum(i, nc - 1), 0)),
                       pl.BlockSpec((tm, d), lambda i, lo, hi: (jnp.maximum(i - nc, 0), 0))],
            scratch_shapes=[pltpu.VMEM((COMBINE_BUFFERS, COMBINE_ROWS, d), F32), pltpu.VMEM((tm, d), F32),
                            pltpu.VMEM((tm, 2), F32),
                            pltpu.SemaphoreType.DMA((COMBINE_BUFFERS,)), pltpu.SMEM((2,), I32)]),
        out_shape=[jax.ShapeDtypeStruct((geom.n_ctx, d), F32), jax.ShapeDtypeStruct((geom.n_lat, d), F32)],
        compiler_params=_params(("arbitrary",)),
        name="gated_combine",
    )(tile_lo, tile_hi, z, start, end, x1, mod3)


class _Geometry:
    def __init__(self, batch, seq, dec_batch, dec_seq):
        self.batch, self.seq, self.dec_batch, self.dec_seq = batch, seq, dec_batch, dec_seq
        self.n_ctx = batch * seq
        self.n_lat = dec_batch * dec_seq
        tm = TOKEN_TILE
        assert self.n_ctx % tm == 0 and dec_seq % tm == 0 and self.n_ctx % dec_seq == 0
        assert seq % CHUNK == 0 and dec_seq % CHUNK == 0 and dec_seq % QUERY_TILE == 0
        self.ctx_tiles = self.n_ctx // tm
        self.lat_tiles_per_seq = dec_seq // tm

    def mod_row(self, i):
        return jnp.where(i < self.ctx_tiles, 0, 1 + (i - self.ctx_tiles) // self.lat_tiles_per_seq)

    def rope_block(self, i):
        return jnp.where(i < self.ctx_tiles, 0, 1 + (i - self.ctx_tiles) % self.lat_tiles_per_seq)


def _rope_tables(dec_seq):
    t = jnp.arange(dec_seq)
    row = (t // GRID_W).astype(F32)
    col = (t % GRID_W).astype(F32)
    inv = ROPE_THETA ** (-jnp.arange(ROT_PAIRS, dtype=F32) / ROT_PAIRS)
    ang = jnp.concatenate([row[:, None] * inv, col[:, None] * inv], axis=-1)
    cos = jnp.tile(jnp.concatenate([jnp.cos(ang), jnp.cos(ang)], axis=-1), (1, N_HEADS))
    sin = jnp.tile(jnp.concatenate([-jnp.sin(ang), jnp.sin(ang)], axis=-1), (1, N_HEADS))
    ident = jnp.ones((TOKEN_TILE, Q_DIM), F32)
    return (jnp.concatenate([ident, cos], axis=0),
            jnp.concatenate([jnp.zeros_like(ident), sin], axis=0))


def _block_diag_ones(width):
    i = jnp.arange(width)
    return (i[:, None] // HEAD_DIM == i[None, :] // HEAD_DIM).astype(BF)


def _layer_weights(l, norm1, norm2, w_in, conv_w, conv_b, q_norm, k_norm, a_log, dt_bias, d_skip,
                   ssm_norm, w_out, w_router):
    w = w_in[l].astype(BF)
    o = 0
    parts = {}
    for name, width in (("wq", Q_DIM), ("wkv", 2 * KV_DIM), ("wx", CONV_DIM),
                        ("wz", SSM_WIDTH), ("wdt", 2 * SSM_HEADS)):
        parts[name] = w[:, o:o + width]
        o += width
    wo = w_out[l].astype(BF)
    wr = w_router[l].T
    wrh = wr.astype(BF)
    nh2 = 2 * SSM_HEADS
    parts.update(
        wdtt=parts["wdt"].T,
        n1=norm1[l][None, :], n2=norm2[l][None, :],
        gq=jnp.tile(q_norm[l], N_HEADS)[None, :], gk=jnp.tile(k_norm[l], N_KV_HEADS)[None, :],
        conv_w=conv_w[l], conv_b=conv_b[l][None, :],
        alog_r=a_log[l].reshape(1, nh2), alog_c=a_log[l].reshape(nh2, 1),
        dtb_r=dt_bias[l].reshape(1, nh2), dtb_c=dt_bias[l].reshape(nh2, 1),
        dskip=jnp.repeat(d_skip[l], SSM_HEAD_DIM)[None, :], gn=ssm_norm[l][None, :],
        woa=wo[:Q_DIM], wos=wo[Q_DIM:],
        wrh=wrh, wrl=(wr - wrh.astype(F32)).astype(BF),
    )
    return parts


def _moe(h2, aff_t, x1, mod3, w_gate, w_up, w_down, layer, n_sets, geom):
    t = x1.shape[0]
    n = t // n_sets
    cap = EC_FACTOR * n // N_EXPERTS
    pairs = N_EXPERTS * cap
    gidx, gates, dest, start, end = _plan(aff_t, n_sets, cap)
    z = _expert_ffn(gidx.reshape(-1), dest.reshape(-1), h2, gates, w_gate, w_up, w_down, layer)
    tile_lo = start.reshape(t)[::TOKEN_TILE].astype(I32) // COMBINE_ROWS
    tile_hi = (end.reshape(t)[TOKEN_TILE - 1::TOKEN_TILE].astype(I32) + COMBINE_ROWS - 1) // COMBINE_ROWS
    lanes_major = lambda a: a.reshape(t // LANES, 1, LANES)
    return _combine(z, n_sets * pairs, lanes_major(start), lanes_major(end), tile_lo, tile_hi, x1, mod3, geom)


def kernel(x_prompt, x_sample, cache_k, cache_v, state_ssm, c, c_ctx, norm1, norm2, ada_w, ada_b,
           w_in, conv_w, conv_b, q_norm, k_norm, A_log, dt_bias, d_skip, ssm_norm, w_out,
           w_router, w_gate, w_up, w_down):
    batch, seq, d = x_prompt.shape
    dec_batch, dec_seq, _ = x_sample.shape
    depth = w_in.shape[0]
    past = cache_k.shape[2]
    geom = _Geometry(batch, seq, dec_batch, dec_seq)
    assert geom.n_ctx == geom.n_lat, "routed sets are stacked as two equal halves"

    rows = -(-(1 + dec_batch) // SUBLANES) * SUBLANES
    cv = jnp.zeros((rows, d), F32).at[0].set(c_ctx).at[1:1 + dec_batch].set(c)
    mod = _ada_modulation(cv, ada_w, ada_b)
    cos, sin = _rope_tables(dec_seq)
    consts = dict(cos=cos, sin=sin, bdq=_block_diag_ones(Q_DIM), bdk=_block_diag_ones(KV_DIM))
    cache_k4 = cache_k.reshape(dec_batch, depth, past, KV_DIM)
    cache_v4 = cache_v.reshape(dec_batch, depth, past, KV_DIM)

    x = (x_prompt.reshape(geom.n_ctx, d), x_sample.reshape(geom.n_lat, d))
    ks_new, vs_new, hs_new = [], [], []
    for l in range(depth):
        lw = _layer_weights(l, norm1, norm2, w_in, conv_w, conv_b, q_norm, k_norm, A_log, dt_bias,
                            d_skip, ssm_norm, w_out, w_router)
        mod3 = mod[l][:, None, :]
        q, k, v, xbc, z, dt, dtt = _in_projection(x, mod3, lw, consts, geom)
        attn_c, attn_l = _attention(q, k, v, cache_k4, cache_v4, l, geom)
        ssm_c, ssm_l, h_fin = _ssd_mixer(xbc, z, dt, dtt, state_ssm, l, lw, geom)
        x1, h2, aff_t = _out_projection(attn_c, attn_l, ssm_c, ssm_l, x, mod3, lw, geom)
        x = _moe(h2, aff_t, x1, mod3, w_gate, w_up, w_down, l, 2, geom)
        ks_new.append(k[:geom.n_ctx].reshape(batch, seq, N_KV_HEADS, HEAD_DIM))
        vs_new.append(v[:geom.n_ctx].reshape(batch, seq, N_KV_HEADS, HEAD_DIM))
        hs_new.append(h_fin)
    y_prompt = x[0].reshape(batch, seq, d)
    y_sample = x[1].reshape(dec_batch, dec_seq, d)
    return (y_prompt, y_sample, jnp.stack(ks_new, axis=1), jnp.stack(vs_new, axis=1),
            jnp.stack(hs_new, axis=1))
```

```python
import functools

import jax
import jax.numpy as jnp
from jax import lax
from jax.experimental import pallas as pl
from jax.experimental.pallas import tpu as pltpu

F32 = jnp.float32
BF = jnp.bfloat16
I32 = jnp.int32

GRID_W = 64
N_HEADS = 8
N_KV_HEADS = 2
HEAD_DIM = 64
GQA = N_HEADS // N_KV_HEADS
ROPE_THETA = 10000.0
ROT_PAIRS = HEAD_DIM // 4
Q_DIM = N_HEADS * HEAD_DIM
KV_DIM = N_KV_HEADS * HEAD_DIM
SSM_WIDTH = 512
SSM_HEADS = 8
SSM_HEAD_DIM = 64
SSM_GROUPS = 2
SSM_STATE = 64
CONV_W = 5
CONV_PAD = CONV_W // 2
CHUNK = 128
BC_DIM = SSM_GROUPS * SSM_STATE
CONV_DIM = SSM_WIDTH + 2 * BC_DIM
N_EXPERTS = 16
EC_FACTOR = 2
EPS = 1e-6
SCALE = HEAD_DIM ** -0.5
LOG2_E = 1.4426950408889634

LANES = 128
SUBLANES = 8
TOKEN_TILE = 1024
QUERY_TILE = 512
HALO = SUBLANES
VMEM_LIMIT = 56 * 1024 * 1024

NT_DIMS = (((1,), (1,)), ((), ()))


def _dot(a, b):
    return jnp.dot(a, b, preferred_element_type=F32)


def _dot_nt(a, b):
    return lax.dot_general(a, b, NT_DIMS, preferred_element_type=F32)


def _bf16_pieces(v):
    p1 = v.astype(BF)
    r1 = v - p1.astype(F32)
    p2 = r1.astype(BF)
    return p1, p2, (r1 - p2.astype(F32)).astype(BF)


def _dot_f32_by_01(a, b, *, f32_side):
    if f32_side == 0:
        d1, d2, d3 = (_dot(p, b) for p in _bf16_pieces(a))
    else:
        d1, d2, d3 = (_dot(a, p) for p in _bf16_pieces(b))
    return d1 + (d2 + d3)


def _silu(x):
    return x / (1.0 + jnp.exp(-x))


def _softplus(x):
    return jnp.maximum(x, 0.0) + jnp.log(1.0 + jnp.exp(-jnp.abs(x)))


def _params(sem):
    return pltpu.CompilerParams(dimension_semantics=sem, vmem_limit_bytes=VMEM_LIMIT)


def _ada_kernel(cv_ref, w_ref, b_ref, o_ref):
    s = _silu(cv_ref[...]).astype(BF)
    o_ref[...] = _dot(s, w_ref[...].astype(BF)) + b_ref[...]


def _ada_modulation(cv, ada_w, ada_b):
    depth, d, n6 = ada_w.shape
    rows = cv.shape[0]
    tn = n6 // 4
    return pl.pallas_call(
        _ada_kernel,
        grid=(depth, n6 // tn),
        in_specs=[
            pl.BlockSpec((rows, d), lambda l, j: (0, 0)),
            pl.BlockSpec((None, d, tn), lambda l, j: (l, 0, j)),
            pl.BlockSpec((None, 1, tn), lambda l, j: (l, 0, j)),
        ],
        out_specs=pl.BlockSpec((None, rows, tn), lambda l, j: (l, 0, j)),
        out_shape=jax.ShapeDtypeStruct((depth, rows, n6), F32),
        compiler_params=_params(("parallel", "parallel")),
        name="ada_modulation",
    )(cv, ada_w, ada_b.reshape(depth, 1, n6))


def _inproj_kernel(xc_ref, xl_ref, mod_ref, n1_ref, wq_ref, wkv_ref, wx_ref, wz_ref, wdtt_ref,
                   bdq_ref, bdk_ref, gq_ref, gk_ref, cos_ref, sin_ref,
                   q_out, k_out, v_out, xbc_out, z_out, dt_out, dtt_out, *, ctx_tiles):
    d = xc_ref.shape[-1]
    x = jnp.where(pl.program_id(0) < ctx_tiles, xc_ref[...], xl_ref[...])
    mod = mod_ref[...]
    sh1 = mod[:, 0:d]
    sc1 = mod[:, d:2 * d]
    ms = jnp.mean(x * x, axis=-1, keepdims=True)
    h = x * lax.rsqrt(ms + EPS) * n1_ref[...]
    hb = (h * (1.0 + sc1) + sh1).astype(BF)
    cos = cos_ref[...]
    sin = sin_ref[...]

    def norm_rope(t, bd_ref, g_ref, cos_t, sin_t):
        ss = _dot((t * t).astype(BF), bd_ref[...])
        tn = t * lax.rsqrt(ss * (1.0 / HEAD_DIM) + EPS) * g_ref[...]
        width = t.shape[-1]
        lane = lax.broadcasted_iota(I32, tn.shape, 1)
        first_half = (lane % HEAD_DIM) < (HEAD_DIM // 2)
        partner = jnp.where(first_half,
                            pltpu.roll(tn, width - HEAD_DIM // 2, 1),
                            pltpu.roll(tn, HEAD_DIM // 2, 1))
        return tn * cos_t + partner * sin_t

    q = norm_rope(_dot(hb, wq_ref[...]), bdq_ref, gq_ref, cos, sin)
    q_out[...] = (q * (SCALE * LOG2_E)).astype(BF)
    kv = _dot(hb, wkv_ref[...])
    k_out[...] = norm_rope(kv[:, :KV_DIM], bdk_ref, gk_ref, cos[:, :KV_DIM], sin[:, :KV_DIM])
    v_out[...] = kv[:, KV_DIM:]
    xbc_out[...] = _dot(hb, wx_ref[...])
    z_out[...] = _dot(hb, wz_ref[...])
    dtt = _dot_nt(wdtt_ref[...], hb)
    dtt_out[...] = dtt
    padded = jnp.concatenate([dtt, jnp.zeros((LANES - dtt.shape[0], dtt.shape[1]), F32)], axis=0)
    dt_out[...] = padded.T[:, :dtt.shape[0]]


def _group_views(x, geom):
    tm = TOKEN_TILE
    nc = geom.ctx_tiles
    if isinstance(x, tuple):
        xc, xl = x
        lat_index = lambda i: (jnp.maximum(i - nc, 0), 0)
    else:
        xc = xl = x
        lat_index = lambda i: (jnp.maximum(i, nc), 0)
    d = xc.shape[-1]
    return xc, xl, pl.BlockSpec((tm, d), lambda i: (jnp.minimum(i, nc - 1), 0)), pl.BlockSpec((tm, d), lat_index)


def _in_projection(x, mod3, lw, consts, geom):
    xc, xl, xc_spec, xl_spec = _group_views(x, geom)
    t, d = geom.n_ctx + geom.n_lat, xc.shape[-1]
    tm = TOKEN_TILE
    full = lambda a: pl.BlockSpec(a.shape, lambda i: (0,) * a.ndim)
    row = lambda w: pl.BlockSpec((tm, w), lambda i: (i, 0))
    weights = [lw["wq"], lw["wkv"], lw["wx"], lw["wz"], lw["wdtt"],
               consts["bdq"], consts["bdk"], lw["gq"], lw["gk"]]
    out_shape = [
        jax.ShapeDtypeStruct((t, Q_DIM), BF),
        jax.ShapeDtypeStruct((t, KV_DIM), F32),
        jax.ShapeDtypeStruct((t, KV_DIM), F32),
        jax.ShapeDtypeStruct((t, CONV_DIM), F32),
        jax.ShapeDtypeStruct((t, SSM_WIDTH), F32),
        jax.ShapeDtypeStruct((t, 2 * SSM_HEADS), F32),
        jax.ShapeDtypeStruct((2 * SSM_HEADS, t), F32),
    ]
    return pl.pallas_call(
        functools.partial(_inproj_kernel, ctx_tiles=geom.ctx_tiles),
        grid=(t // tm,),
        in_specs=[xc_spec, xl_spec,
                  pl.BlockSpec((None, 1, mod3.shape[-1]), lambda i: (geom.mod_row(i), 0, 0)),
                  full(lw["n1"])]
                 + [full(w) for w in weights]
                 + [pl.BlockSpec((tm, Q_DIM), lambda i: (geom.rope_block(i), 0)),
                    pl.BlockSpec((tm, Q_DIM), lambda i: (geom.rope_block(i), 0))],
        out_specs=[row(Q_DIM), row(KV_DIM), row(KV_DIM), row(CONV_DIM), row(SSM_WIDTH),
                   row(2 * SSM_HEADS), pl.BlockSpec((2 * SSM_HEADS, tm), lambda i: (0, i))],
        out_shape=out_shape,
        compiler_params=_params(("parallel",)),
        name="in_projection",
    )(xc, xl, mod3, lw["n1"], *weights, consts["cos"], consts["sin"])


def _head_pair_operands(k, v):
    lane = lax.broadcasted_iota(I32, k.shape, 1)
    low = lane < HEAD_DIM
    k_sw = pltpu.roll(k, HEAD_DIM, 1)
    v_sw = pltpu.roll(v, HEAD_DIM, 1)
    ones_last = jnp.where(lane == LANES - 1, 1.0, 0.0)
    ones_first = jnp.where(lane == 0, 1.0, 0.0)
    kk = jnp.stack([jnp.where(low, k, k_sw), jnp.where(low, k_sw, k)]).astype(BF)
    v_lo = jnp.stack([jnp.where(low, v, ones_last), jnp.where(low, v_sw, ones_last)]).astype(BF)
    v_hi = jnp.stack([jnp.where(low, ones_first, v_sw), jnp.where(low, ones_first, v)]).astype(BF)
    return kk, v_lo, v_hi


def _attend(q_ref, o_ref, key_sets):
    tq = q_ref.shape[0]
    lane = lax.broadcasted_iota(I32, (tq, LANES), 1)
    low = lane < HEAD_DIM
    for pair in range(N_HEADS // 2):
        g = (2 * pair) // GQA
        q2 = q_ref[:, pair * LANES:(pair + 1) * LANES]
        zero = jnp.zeros_like(q2)
        accs = []
        for half, qh in enumerate((jnp.where(low, q2, zero), jnp.where(low, zero, q2))):
            scores = [_dot_nt(qh, ks[0][g]) for ks in key_sets]
            m = scores[0].max(axis=-1, keepdims=True)
            for s in scores[1:]:
                m = jnp.maximum(m, s.max(axis=-1, keepdims=True))
            acc = jnp.zeros((tq, LANES), F32)
            for s, ks in zip(scores, key_sets):
                acc = acc + _dot(jnp.exp2(s - m).astype(BF), ks[1 + half][g])
            accs.append(acc)
        out = jnp.where(low, accs[0] * (1.0 / accs[0][:, LANES - 1:LANES]), accs[1] * (1.0 / accs[1][:, 0:1]))
        o_ref[:, pair * LANES:(pair + 1) * LANES] = out.astype(BF)


def _attn_ctx_kernel(q_ref, k_ref, v_ref, o_ref):
    _attend(q_ref, o_ref, [_head_pair_operands(k_ref[...], v_ref[...])])


def _attn_lat_kernel(q_ref, kc_ref, vc_ref, kl_ref, vl_ref, o_ref, *ops):
    @pl.when(pl.program_id(1) == 0)
    def _():
        for dst, src in zip(ops, _head_pair_operands(kc_ref[...], vc_ref[...])
                            + _head_pair_operands(kl_ref[...], vl_ref[...])):
            dst[...] = src

    _attend(q_ref, o_ref, [ops[0:3], ops[3:6]])


def _attention(q, k, v, cache_k4, cache_v4, layer, geom):
    seq = geom.seq
    o_ctx = pl.pallas_call(
        _attn_ctx_kernel,
        grid=(geom.batch,),
        in_specs=[pl.BlockSpec((seq, Q_DIM), lambda b: (b, 0)),
                  pl.BlockSpec((seq, KV_DIM), lambda b: (b, 0)),
                  pl.BlockSpec((seq, KV_DIM), lambda b: (b, 0))],
        out_specs=pl.BlockSpec((seq, Q_DIM), lambda b: (b, 0)),
        out_shape=jax.ShapeDtypeStruct((geom.n_ctx, Q_DIM), BF),
        compiler_params=_params(("parallel",)),
        name="attention_context",
    )(q, k, v)

    tq = QUERY_TILE
    dseq = geom.dec_seq
    past = cache_k4.shape[2]
    q_tiles = dseq // tq
    ctx_tiles = geom.n_ctx // tq
    ctx_seqs = geom.n_ctx // dseq
    o_lat = pl.pallas_call(
        _attn_lat_kernel,
        grid=(geom.dec_batch, q_tiles),
        in_specs=[pl.BlockSpec((tq, Q_DIM), lambda b, i: (ctx_tiles + b * q_tiles + i, 0)),
                  pl.BlockSpec((None, None, past, KV_DIM), lambda b, i: (b, layer, 0, 0)),
                  pl.BlockSpec((None, None, past, KV_DIM), lambda b, i: (b, layer, 0, 0)),
                  pl.BlockSpec((dseq, KV_DIM), lambda b, i: (ctx_seqs + b, 0)),
                  pl.BlockSpec((dseq, KV_DIM), lambda b, i: (ctx_seqs + b, 0))],
        out_specs=pl.BlockSpec((tq, Q_DIM), lambda b, i: (b * q_tiles + i, 0)),
        out_shape=jax.ShapeDtypeStruct((geom.n_lat, Q_DIM), BF),
        scratch_shapes=[pltpu.VMEM((N_KV_HEADS, s, KV_DIM), BF) for s in (past, past, past, dseq, dseq, dseq)],
        compiler_params=_params(("parallel", "arbitrary")),
        name="attention_latent",
    )(q, cache_k4, cache_v4, k, v)
    return o_ctx, o_lat


def _ssd_kernel(*refs, n_chunks, has_h0, emit_state):
    it = iter(refs)
    xbc_ref, z_ref, dt_ref, dtt_ref = next(it), next(it), next(it), next(it)
    h0_ref = next(it) if has_h0 else None
    (cw_ref, cb_ref, alog_r_ref, alog_c_ref, dtb_r_ref, dtb_c_ref,
     dskip_ref, gn_ref) = (next(it) for _ in range(8))
    y_out = next(it)
    hfin_out = next(it) if emit_state else None
    pad_ref, u_ref, csc_ref, csr_ref, dtr_ref, tot_ref, st_ref, hcur_ref, y_ref = (next(it) for _ in range(9))

    seq = xbc_ref.shape[0]
    nh = SSM_HEADS
    q = CHUNK

    pad_ref[0:HALO, :] = jnp.zeros((HALO, CONV_DIM), F32)
    pad_ref[HALO:HALO + seq, :] = xbc_ref[...]
    pad_ref[HALO + seq:2 * HALO + seq, :] = jnp.zeros((HALO, CONV_DIM), F32)

    a_row = -jnp.exp(alog_r_ref[...])
    a_col = -jnp.exp(alog_c_ref[...])
    ii = lax.broadcasted_iota(I32, (q, q), 0)
    jj = lax.broadcasted_iota(I32, (q, q), 1)
    lower = jj <= ii
    upper = jj >= ii
    low_half = jj < SSM_STATE
    tri_lower = lower.astype(BF)
    tri_upper = upper.astype(BF)
    fwd_lane = lax.broadcasted_iota(I32, (q, 2 * nh), 1) < nh
    fwd_row = lax.broadcasted_iota(I32, (2 * nh, q), 0) < nh

    def chunk_stats(c, carry):
        base = pl.multiple_of(c * q, q)
        win = pad_ref[pl.ds(base, q + 2 * HALO), :]
        acc = jnp.zeros((q, CONV_DIM), F32) + cb_ref[...]
        for tap in range(CONV_W):
            off = HALO - CONV_PAD + tap
            acc = acc + win[off:off + q, :] * cw_ref[tap:tap + 1, :]
        u = _silu(acc)
        u_ref[pl.ds(base, q), :] = u

        dtc = _softplus(dt_ref[pl.ds(base, q), :] + dtb_r_ref[...])
        dtr = _softplus(dtt_ref[:, pl.ds(base, q)] + dtb_c_ref[...])
        ac = dtc * a_row
        ar = dtr * a_col
        cs_col = jnp.where(fwd_lane, _dot_f32_by_01(tri_lower, ac, f32_side=1),
                           _dot_f32_by_01(tri_upper, ac, f32_side=1))
        cs_row = jnp.where(fwd_row, _dot_f32_by_01(ar, tri_upper, f32_side=0),
                           _dot_f32_by_01(ar, tri_lower, f32_side=0))
        csc_ref[pl.ds(base, q), :] = cs_col
        csr_ref[:, pl.ds(base, q)] = cs_row
        dtr_ref[:, pl.ds(base, q)] = dtr
        total = jnp.where(fwd_lane[0:1, :], cs_col[q - 1:q, :], cs_col[0:1, :])
        tot_ref[c] = total
        total_c = jnp.where(fwd_row[:, 0:1], cs_row[:, q - 1:q], cs_row[:, 0:1])
        w_end = jnp.exp(total_c - cs_row) * dtr

        xs_b = u[:, 0:SSM_WIDTH].astype(BF)
        b_t = u[:, SSM_WIDTH:SSM_WIDTH + BC_DIM].T
        for dh in range(2 * nh):
            head = dh % nh
            grp = head // (nh // SSM_GROUPS)
            bw = (b_t[grp * SSM_STATE:(grp + 1) * SSM_STATE, :] * w_end[dh:dh + 1, :]).astype(BF)
            st_ref[c, dh] = _dot(bw, xs_b[:, head * SSM_HEAD_DIM:(head + 1) * SSM_HEAD_DIM])
        return carry

    lax.fori_loop(0, n_chunks, chunk_stats, 0)

    eye = (lax.broadcasted_iota(I32, (SSM_STATE, SSM_STATE), 0)
           == lax.broadcasted_iota(I32, (SSM_STATE, SSM_STATE), 1)).astype(BF)

    def transposed_blocks(stack):
        p1, p2, p3 = _bf16_pieces(stack)
        return _dot_nt(eye, p1) + (_dot_nt(eye, p2) + _dot_nt(eye, p3))

    if has_h0:
        h0_t = transposed_blocks(h0_ref[...].reshape(2 * nh * SSM_HEAD_DIM, SSM_STATE))
    for dh in range(2 * nh):
        hcur_ref[dh] = (h0_t[:, dh * SSM_HEAD_DIM:(dh + 1) * SSM_HEAD_DIM] if has_h0
                        else jnp.zeros((SSM_STATE, SSM_HEAD_DIM), F32))

    def recur(step, carry):
        for direction in range(2):
            c = step if direction == 0 else n_chunks - 1 - step
            decay = jnp.exp(tot_ref[c])
            for head in range(nh):
                dh = direction * nh + head
                new = st_ref[c, dh]
                cur = hcur_ref[dh]
                st_ref[c, dh] = cur
                hcur_ref[dh] = cur * decay[:, dh:dh + 1] + new
        return carry

    lax.fori_loop(0, n_chunks, recur, 0)
    if emit_state:
        h_t = transposed_blocks(hcur_ref[...].reshape(2 * nh * SSM_STATE, SSM_HEAD_DIM))
        for dh in range(2 * nh):
            hfin_out[dh // nh, dh % nh] = h_t[:, dh * SSM_STATE:(dh + 1) * SSM_STATE]

    def chunk_out(c, carry):
        base = pl.multiple_of(c * q, q)
        u = u_ref[pl.ds(base, q), :]
        xs = u[:, 0:SSM_WIDTH]
        xs_b = xs.astype(BF)
        bmat = u[:, SSM_WIDTH:SSM_WIDTH + BC_DIM].astype(BF)
        cmat = u[:, SSM_WIDTH + BC_DIM:CONV_DIM]
        cmat_b = cmat.astype(BF)
        cs_col = csc_ref[pl.ds(base, q), :]
        cs_row = csr_ref[:, pl.ds(base, q)]
        dtr = dtr_ref[:, pl.ds(base, q)]
        c_sw = pltpu.roll(cmat, SSM_STATE, 1)
        for grp in range(SSM_GROUPS):
            gs = slice(grp * SSM_STATE, (grp + 1) * SSM_STATE)
            cb = _dot_nt(cmat_b[:, gs], bmat[:, gs])
            c_dup = jnp.where(low_half, cmat, c_sw) if grp == 0 else jnp.where(low_half, c_sw, cmat)
            for head in range(grp * (nh // SSM_GROUPS), (grp + 1) * (nh // SSM_GROUPS)):
                hb = nh + head
                col_f = jnp.broadcast_to(cs_col[:, head:head + 1], (q, q))
                col_b = jnp.broadcast_to(cs_col[:, hb:hb + 1], (q, q))
                seg_f = jnp.where(lower, col_f - cs_row[head:head + 1, :], -1e30)
                seg_b = jnp.where(upper, col_b - cs_row[hb:hb + 1, :], -1e30)
                mix = cb * (jnp.exp(seg_f) * dtr[head:head + 1, :] + jnp.exp(seg_b) * dtr[hb:hb + 1, :])
                carry_in = c_dup * jnp.exp(jnp.where(low_half, col_f, col_b))
                hs = slice(head * SSM_HEAD_DIM, (head + 1) * SSM_HEAD_DIM)
                lhs = jnp.concatenate([mix.astype(BF), carry_in.astype(BF)], axis=1)
                rhs = jnp.concatenate([xs_b[:, hs], st_ref[c, head].astype(BF), st_ref[c, hb].astype(BF)], axis=0)
                y_ref[:, hs] = _dot(lhs, rhs)
        y = y_ref[...] + xs * dskip_ref[...]
        y = y * _silu(z_ref[pl.ds(base, q), :])
        ms = jnp.mean(y * y, axis=-1, keepdims=True)
        y_out[pl.ds(base, q), :] = (y * lax.rsqrt(ms + EPS) * gn_ref[...]).astype(BF)
        return carry

    lax.fori_loop(0, n_chunks, chunk_out, 0)


def _ssd_call(xbc, z, dt, dtt, h0, layer, lw, *, seq, n_seq, row_block0, emit_state, name):
    n_chunks = seq // CHUNK
    nh2 = 2 * SSM_HEADS
    full = lambda a: pl.BlockSpec(a.shape, lambda b: (0,) * a.ndim)
    params = [lw["conv_w"], lw["conv_b"], lw["alog_r"], lw["alog_c"], lw["dtb_r"], lw["dtb_c"],
              lw["dskip"], lw["gn"]]
    in_specs = [pl.BlockSpec((seq, CONV_DIM), lambda b: (row_block0 + b, 0)),
                pl.BlockSpec((seq, SSM_WIDTH), lambda b: (row_block0 + b, 0)),
                pl.BlockSpec((seq, nh2), lambda b: (row_block0 + b, 0)),
                pl.BlockSpec((nh2, seq), lambda b: (0, row_block0 + b))]
    args = [xbc, z, dt, dtt]
    if h0 is not None:
        in_specs.append(pl.BlockSpec((None, None, 2, SSM_HEADS, SSM_HEAD_DIM, SSM_STATE),
                                     lambda b: (b, layer, 0, 0, 0, 0)))
        args.append(h0)
    in_specs += [full(p) for p in params]
    args += params
    out_specs = [pl.BlockSpec((seq, SSM_WIDTH), lambda b: (b, 0))]
    out_shape = [jax.ShapeDtypeStruct((n_seq * seq, SSM_WIDTH), BF)]
    if emit_state:
        out_specs.append(pl.BlockSpec((None, 2, SSM_HEADS, SSM_HEAD_DIM, SSM_STATE),
                                      lambda b: (b, 0, 0, 0, 0)))
        out_shape.append(jax.ShapeDtypeStruct((n_seq, 2, SSM_HEADS, SSM_HEAD_DIM, SSM_STATE), F32))
    scratch = [
        pltpu.VMEM((seq + 2 * HALO, CONV_DIM), F32),
        pltpu.VMEM((seq, CONV_DIM), F32),
        pltpu.VMEM((seq, nh2), F32),
        pltpu.VMEM((nh2, seq), F32),
        pltpu.VMEM((nh2, seq), F32),
        pltpu.VMEM((n_chunks, 1, nh2), F32),
        pltpu.VMEM((n_chunks, nh2, SSM_HEAD_DIM, SSM_STATE), F32),
        pltpu.VMEM((nh2, SSM_HEAD_DIM, SSM_STATE), F32),
        pltpu.VMEM((CHUNK, SSM_WIDTH), F32),
    ]
    return pl.pallas_call(
        functools.partial(_ssd_kernel, n_chunks=n_chunks, has_h0=h0 is not None, emit_state=emit_state),
        grid=(n_seq,),
        in_specs=in_specs,
        out_specs=out_specs,
        out_shape=out_shape,
        scratch_shapes=scratch,
        compiler_params=_params(("parallel",)),
        name=name,
    )(*args)


def _ssd_mixer(xbc, z, dt, dtt, state_ssm, layer, lw, geom):
    y_ctx, h_fin = _ssd_call(xbc, z, dt, dtt, None, layer, lw, seq=geom.seq, n_seq=geom.batch,
                             row_block0=0, emit_state=True, name="ssd_context")
    (y_lat,) = _ssd_call(xbc, z, dt, dtt, state_ssm, layer, lw, seq=geom.dec_seq,
                         n_seq=geom.dec_batch, row_block0=geom.n_ctx // geom.dec_seq,
                         emit_state=False, name="ssd_latent")
    return y_ctx, y_lat, h_fin


def _outproj_kernel(ac_ref, al_ref, sc_ref, sl_ref, xc_ref, xl_ref, mod_ref, n2_ref, woa_ref, wos_ref,
                    wrh_ref, wrl_ref, x1_out, h2_out, afft_out, *, ctx_tiles):
    d = xc_ref.shape[-1]
    mod = mod_ref[...]
    g1 = mod[:, 2 * d:3 * d]
    sh2 = mod[:, 3 * d:4 * d]
    sc2 = mod[:, 4 * d:5 * d]
    is_ctx = pl.program_id(0) < ctx_tiles
    attn = jnp.where(is_ctx, ac_ref[...], al_ref[...])
    ssm = jnp.where(is_ctx, sc_ref[...], sl_ref[...])
    y = _dot(attn, woa_ref[...]) + _dot(ssm, wos_ref[...])
    x1 = jnp.where(is_ctx, xc_ref[...], xl_ref[...]) + g1 * y
    x1_out[...] = x1
    ms = jnp.mean(x1 * x1, axis=-1, keepdims=True)
    h2 = x1 * lax.rsqrt(ms + EPS) * n2_ref[...]
    h2 = h2 * (1.0 + sc2) + sh2
    h2_out[...] = h2
    hh = h2.astype(BF)
    hl = (h2 - hh.astype(F32)).astype(BF)
    wh = wrh_ref[...]
    logits = _dot_nt(wh, hh) + _dot_nt(wh, hl) + _dot_nt(wrl_ref[...], hh)
    logits = logits - logits.max(axis=0, keepdims=True)
    e = jnp.exp(logits)
    afft_out[...] = e / e.sum(axis=0, keepdims=True)


def _out_projection(attn_c, attn_l, ssm_c, ssm_l, x, mod3, lw, geom):
    xc, xl, xc_spec, xl_spec = _group_views(x, geom)
    t, d = geom.n_ctx + geom.n_lat, xc.shape[-1]
    tm = TOKEN_TILE
    nc = geom.ctx_tiles
    full = lambda a: pl.BlockSpec(a.shape, lambda i: (0,) * a.ndim)
    row = lambda w: pl.BlockSpec((tm, w), lambda i: (i, 0))
    ctx_row = lambda w: pl.BlockSpec((tm, w), lambda i: (jnp.minimum(i, nc - 1), 0))
    lat_row = lambda w: pl.BlockSpec((tm, w), lambda i: (jnp.maximum(i - nc, 0), 0))
    weights = [lw["n2"], lw["woa"], lw["wos"], lw["wrh"], lw["wrl"]]
    return pl.pallas_call(
        functools.partial(_outproj_kernel, ctx_tiles=nc),
        grid=(t // tm,),
        in_specs=[ctx_row(Q_DIM), lat_row(Q_DIM), ctx_row(SSM_WIDTH), lat_row(SSM_WIDTH), xc_spec, xl_spec,
                  pl.BlockSpec((None, 1, mod3.shape[-1]), lambda i: (geom.mod_row(i), 0, 0))]
                 + [full(w) for w in weights],
        out_specs=[row(d), row(d), pl.BlockSpec((N_EXPERTS, tm), lambda i: (0, i))],
        out_shape=[jax.ShapeDtypeStruct((t, d), F32), jax.ShapeDtypeStruct((t, d), F32),
                   jax.ShapeDtypeStruct((N_EXPERTS, t), F32)],
        compiler_params=_params(("parallel",)),
        name="out_projection",
    )(attn_c, attn_l, ssm_c, ssm_l, xc, xl, mod3, *weights)


def _flat_cumsum(m2, rows):
    gr = m2.shape[0]
    li = lax.broadcasted_iota(I32, (LANES, LANES), 0)
    lj = lax.broadcasted_iota(I32, (LANES, LANES), 1)
    mb = m2.astype(BF)
    within = _dot(mb, (li <= lj).astype(BF))
    row_tot = _dot(mb, jnp.ones((LANES, LANES), BF))
    ri = lax.broadcasted_iota(I32, (gr, gr), 0)
    rj = lax.broadcasted_iota(I32, (gr, gr), 1)
    before = ((ri // rows == rj // rows) & (rj < ri)).astype(BF)
    hi, lo = _split_base128(row_tot)
    return within + (LANES * _dot(before, hi) + _dot(before, lo))


def _split_base128(v):
    hi = jnp.floor(v * (1.0 / LANES))
    return hi.astype(BF), (v - hi * LANES).astype(BF)


def _plan_kernel(aff_ref, idx_out, gate_out, dslot_out, start_out, end_out, *, cap):
    a = aff_ref[...]
    n_e, rows, _ = a.shape

    def count(hit):
        return hit.astype(I32).sum(axis=1, keepdims=True).sum(axis=2, keepdims=True)

    def kth_largest_bits(values, member, k):
        def search(step, lo):
            cand = lo | (1 << (30 - step))
            return jnp.where(count(member & (values >= lax.bitcast_convert_type(cand, F32))) >= k, cand, lo)
        return lax.fori_loop(0, 31, search, jnp.zeros((n_e, 1, 1), I32))

    everyone = jnp.full(a.shape, True)
    thr_bits = kth_largest_bits(a, everyone, cap)
    thr = lax.bitcast_convert_type(thr_bits, F32)
    above = a >= lax.bitcast_convert_type(thr_bits + 1, F32)
    cell = (a >= thr) & jnp.logical_not(above)
    need = cap - count(above)
    resid = jnp.where(cell, a - thr, 0.0)
    thr2 = lax.bitcast_convert_type(kth_largest_bits(resid, cell, need), F32)
    gt = above | (cell & (resid > thr2))
    eq = cell & (resid == thr2)
    need = cap - count(gt)
    eqf = eq.astype(F32).reshape(n_e * rows, LANES)
    rank = (_flat_cumsum(eqf, rows) - eqf).reshape(n_e, rows, LANES)
    mask = gt | (eq & (rank < need.astype(F32)))
    mf = mask.astype(F32)

    c3 = _flat_cumsum(mf.reshape(n_e * rows, LANES), rows).reshape(n_e, rows, LANES)
    ahead = []
    run = jnp.zeros((rows, LANES), F32)
    for e in range(n_e):
        ahead.append(run)
        run = run + mf[e]
    cnt = run
    start = _flat_cumsum(cnt, rows) - cnt
    set_id = pl.program_id(0)
    row0 = (set_id * (n_e * cap)).astype(F32)
    start_out[...] = start + row0
    end_out[...] = start + cnt + row0

    s_lane = lax.broadcasted_iota(I32, (1, cap), 1).astype(F32)
    r_col = lax.broadcasted_iota(I32, (LANES, cap), 0).astype(F32)
    pad = jnp.zeros((LANES - rows, LANES), F32)

    def lanes_first(xm):
        return (jnp.concatenate([xm, pad], axis=0) if rows < LANES else xm).T

    for e in range(n_e):
        c_e = c3[e]
        row_of = (c_e[:, LANES - 1:LANES] <= s_lane).astype(F32).sum(axis=0, keepdims=True)
        onehot = (r_col == row_of).astype(BF)
        pieces = (_split_base128(lanes_first(c_e)) + _bf16_pieces(lanes_first(a[e]))
                  + _split_base128(lanes_first(start + ahead[e])))
        sel = _dot(jnp.concatenate(pieces, axis=0), onehot)
        part = [sel[j * LANES:(j + 1) * LANES] for j in range(len(pieces))]
        sel_c = LANES * part[0] + part[1]
        sel_a = part[2] + (part[3] + part[4])
        sel_d = LANES * part[5] + part[6]
        lane_of = (sel_c <= s_lane).astype(F32).sum(axis=0, keepdims=True)
        pick = r_col == lane_of
        idx_out[e:e + 1, :] = (row_of * LANES + lane_of).astype(I32) + set_id * (rows * LANES)
        gate_out[e:e + 1, :] = jnp.where(pick, sel_a, 0.0).sum(axis=0, keepdims=True)
        dslot_out[e:e + 1, :] = (jnp.where(pick, sel_d, 0.0).sum(axis=0, keepdims=True).astype(I32)
                                 + set_id * (n_e * cap))


def _plan(aff_t, n_sets, cap):
    n_e, t = aff_t.shape
    n = t // n_sets
    rows = n // LANES
    assert rows <= LANES
    aff4 = aff_t.reshape(n_e, n_sets, rows, LANES)
    slot_spec = pl.BlockSpec((n_e, cap), lambda s: (0, s))
    tok_spec = pl.BlockSpec((None, rows, LANES), lambda s: (s, 0, 0))
    slot_shape = lambda dt: jax.ShapeDtypeStruct((n_e, n_sets * cap), dt)
    tok_shape = jax.ShapeDtypeStruct((n_sets, rows, LANES), F32)
    return pl.pallas_call(
        functools.partial(_plan_kernel, cap=cap),
        grid=(n_sets,),
        in_specs=[pl.BlockSpec((n_e, None, rows, LANES), lambda s: (0, s, 0, 0))],
        out_specs=[slot_spec, slot_spec, slot_spec, tok_spec, tok_spec],
        out_shape=[slot_shape(I32), slot_shape(F32), slot_shape(I32), tok_shape, tok_shape],
        compiler_params=_params(("parallel",)),
        name="expert_choice_plan",
    )(aff4)


DMA_UNROLL = 8
ROW_BLOCK = 512
EXPERT_FF_TILE = 512
COMBINE_TILE = 256
COMBINE_ROWS = 256
COMBINE_BUFFERS = 4


def _expert_kernel(gidx_ref, dest_ref, h2_hbm, gate_ref, wg_ref, wu_ref, wd_ref, z_hbm,
                   xw_a, xw_b, xb_ref, acc_a, acc_b, wgb_ref, wub_ref, wdb_ref, gcol_ref, sem_g, sem_s,
                   *, n_f, row_block):
    e = pl.program_id(0)
    f = pl.program_id(1)
    n_e = pl.num_programs(0)
    m_rows = xw_a.shape[0]
    n_blocks = m_rows // row_block
    per_block = m_rows // (n_f * n_blocks)

    def vmem_row(buf, m8, k):
        tile = pl.multiple_of(m8 + (k // SUBLANES) * SUBLANES, SUBLANES)
        return buf.at[pl.ds(tile, SUBLANES), :].at[pl.ds(k % SUBLANES, 1), :]

    def gather_copy(entry, m8, k, xw):
        return pltpu.make_async_copy(h2_hbm.at[pl.ds(gidx_ref[entry], 1), :], vmem_row(xw, m8, k), sem_g)

    def scatter_copy(entry, m8, k, acc):
        return pltpu.make_async_copy(vmem_row(acc, m8, k), z_hbm.at[pl.ds(dest_ref[entry], 1), :], sem_s)

    def wait_gathers(xw):
        pltpu.make_async_copy(h2_hbm.at[pl.ds(0, m_rows), :], xw, sem_g).wait()

    def wait_scatters(acc):
        pltpu.make_async_copy(acc, z_hbm.at[pl.ds(0, m_rows), :], sem_s).wait()

    def issue_all(make):
        def body(j, carry):
            for k in range(DMA_UNROLL):
                make(j * DMA_UNROLL, k).start()
            return carry
        lax.fori_loop(0, m_rows // DMA_UNROLL, body, 0)

    def run(xw_cur, xw_next, acc_cur, acc_prev):
        @pl.when((e == 0) & (f == 0))
        def _():
            acc_prev[...] = jnp.zeros_like(acc_prev)
            issue_all(lambda m8, k: gather_copy(m8 + k, m8, k, xw_cur))

        @pl.when(f == 0)
        def _():
            wait_gathers(xw_cur)
            xb_ref[...] = xw_cur[...].astype(BF)
            eye = (lax.broadcasted_iota(I32, (LANES, LANES), 0)
                   == lax.broadcasted_iota(I32, (LANES, LANES), 1)).astype(BF)
            g1, g2, g3 = _bf16_pieces(gate_ref[...])
            cols = _dot_nt(eye, g1) + (_dot_nt(eye, g2) + _dot_nt(eye, g3))
            for j in range(gate_ref.shape[0]):
                gcol_ref[j * LANES:(j + 1) * LANES, :] = cols[:, j:j + 1]

        wgb_ref[...] = wg_ref[...].astype(BF)
        wub_ref[...] = wu_ref[...].astype(BF)
        wdb_ref[...] = wd_ref[...].astype(BF)
        last = f == n_f - 1

        def block(rb, carry, *, is_first, is_last):
            r0 = pl.multiple_of(rb * row_block, row_block)
            m0 = (f * n_blocks + rb) * per_block

            for k in range(per_block):
                gather_copy((e + 1) * m_rows + m0 + k, m0, k, xw_next).start()
                scatter_copy(e * m_rows + m0 + k, m0, k, acc_prev).start()
            x = xb_ref[pl.ds(r0, row_block), :]
            g = _dot(x, wgb_ref[...])
            u = _dot(x, wub_ref[...])
            hid = (_silu(g) * u).astype(BF)
            total = _dot(hid, wdb_ref[...])
            if not is_first:
                total = acc_cur[pl.ds(r0, row_block), :] + total
            if is_last:
                total = total * gcol_ref[pl.ds(r0, row_block), :]
            acc_cur[pl.ds(r0, row_block), :] = total
            return carry

        for is_first, is_last in ((True, n_f == 1), (False, False), (False, True)):
            if (n_f == 1 and not is_first) or (n_f == 2 and not (is_first or is_last)):
                continue
            step = (f == 0) if is_first else (last if is_last else (f > 0) & jnp.logical_not(last))

            @pl.when(step)
            def _(is_first=is_first, is_last=is_last):
                lax.fori_loop(0, n_blocks, functools.partial(block, is_first=is_first, is_last=is_last), 0)

        @pl.when(last)
        def _():
            wait_scatters(acc_prev)

        @pl.when(last & (e == n_e - 1))
        def _():
            wait_gathers(xw_next)
            issue_all(lambda m8, k: scatter_copy((e + 1) * m_rows + m8 + k, m8, k, acc_cur))
            wait_scatters(acc_cur)

    @pl.when(e % 2 == 0)
    def _():
        run(xw_a, xw_b, acc_a, acc_b)

    @pl.when(e % 2 == 1)
    def _():
        run(xw_b, xw_a, acc_b, acc_a)


def _expert_ffn(gidx, dest, h2, gates, w_gate, w_up, w_down, layer):
    n_e, m = gates.shape
    gates = gates.reshape(n_e, m // LANES, LANES)
    d = h2.shape[1]
    ff = w_gate.shape[-1]
    tf = min(EXPERT_FF_TILE, ff)
    n_f = ff // tf
    row_block = min(ROW_BLOCK, m)
    assert m % DMA_UNROLL == 0 and (n_e * m) % COMBINE_ROWS == 0 and m % (n_f * (m // row_block)) == 0
    gidx_ext = jnp.concatenate([gidx, jnp.arange(m, dtype=I32)])
    dest_ext = jnp.concatenate([n_e * m + jnp.arange(m, dtype=I32), dest])
    return pl.pallas_call(
        functools.partial(_expert_kernel, n_f=n_f, row_block=row_block),
        grid_spec=pltpu.PrefetchScalarGridSpec(
            num_scalar_prefetch=2,
            grid=(n_e, n_f),
            in_specs=[pl.BlockSpec(memory_space=pl.ANY),
                      pl.BlockSpec((None, m // LANES, LANES), lambda e, f, gi, de: (e, 0, 0)),
                      pl.BlockSpec((None, None, d, tf), lambda e, f, gi, de: (layer, e, 0, f)),
                      pl.BlockSpec((None, None, d, tf), lambda e, f, gi, de: (layer, e, 0, f)),
                      pl.BlockSpec((None, None, tf, d), lambda e, f, gi, de: (layer, e, f, 0))],
            out_specs=pl.BlockSpec(memory_space=pl.ANY),
            scratch_shapes=[pltpu.VMEM((m, d), F32), pltpu.VMEM((m, d), F32),
                            pltpu.VMEM((m, d), BF), pltpu.VMEM((m, d), F32), pltpu.VMEM((m, d), F32),
                            pltpu.VMEM((d, tf), BF), pltpu.VMEM((d, tf), BF), pltpu.VMEM((tf, d), BF),
                            pltpu.VMEM((m, 1), F32),
                            pltpu.SemaphoreType.DMA, pltpu.SemaphoreType.DMA]),
        out_shape=jax.ShapeDtypeStruct(((n_e + 1) * m, d), F32),
        compiler_params=_params(("arbitrary", "arbitrary")),
        name="expert_swiglu",
    )(gidx_ext, dest_ext, h2, gates, w_gate, w_up, w_down)


def _combine_kernel(lo_ref, hi_ref, z_hbm, start_ref, end_ref, x1_ref, mod_ref, oc_ref, ol_ref, zbuf, acc_ref,
                    col_ref, sem, state, *, n_chunks, ctx_tiles):
    i = pl.program_id(0)
    d = x1_ref.shape[-1]
    n_buf, ch, _ = zbuf.shape
    lo = lo_ref[i]
    hi = hi_ref[i]

    def chunk_copy(kc):
        slot = kc % n_buf
        return pltpu.make_async_copy(z_hbm.at[pl.ds(pl.multiple_of(kc * ch, ch), ch), :], zbuf.at[slot],
                                     sem.at[slot])

    @pl.when(i == 0)
    def _():
        state[0] = 0
        state[1] = 0

    acc_ref[...] = jnp.zeros_like(acc_ref)
    tm = acc_ref.shape[0]
    blocks = tm // LANES
    rows = jnp.concatenate([start_ref[i * blocks + j] for j in range(blocks)]
                           + [end_ref[i * blocks + j] for j in range(blocks)], axis=0)
    eye = (lax.broadcasted_iota(I32, (LANES, LANES), 0) == lax.broadcasted_iota(I32, (LANES, LANES), 1)).astype(BF)
    upper = jnp.floor(rows * (1.0 / 256.0))
    cols = 256.0 * _dot_nt(eye, upper.astype(BF)) + _dot_nt(eye, (rows - 256.0 * upper).astype(BF))
    for j in range(blocks):
        col_ref[j * LANES:(j + 1) * LANES, 0:1] = cols[:, j:j + 1]
        col_ref[j * LANES:(j + 1) * LANES, 1:2] = cols[:, blocks + j:blocks + j + 1]
    start = col_ref[:, 0:1]
    end = col_ref[:, 1:2]

    def body(kc, carry):
        upto = jnp.minimum(kc + n_buf, n_chunks)

        def start_one(j, c):
            chunk_copy(j).start()
            return c

        lax.fori_loop(state[0], upto, start_one, 0)
        state[0] = jnp.maximum(state[0], upto)

        @pl.when(kc >= state[1])
        def _():
            chunk_copy(kc).wait()
            state[1] = kc + 1

        slot = kc % n_buf
        k = (kc * ch + lax.broadcasted_iota(I32, (1, ch), 1)).astype(F32)
        own = jnp.where((k >= start) & (k < end), 1.0, 0.0).astype(BF)
        acc_ref[...] += _dot(own, zbuf[slot].astype(BF))
        return carry

    lax.fori_loop(lo, hi, body, 0)
    g2 = mod_ref[...][:, 5 * d:6 * d]
    out = x1_ref[...] + g2 * acc_ref[...]

    @pl.when(i < ctx_tiles)
    def _():
        oc_ref[...] = out

    @pl.when(i >= ctx_tiles)
    def _():
        ol_ref[...] = out


def _combine(z, n_rows, start, end, tile_lo, tile_hi, x1, mod3, geom):
    t, d = x1.shape
    tm = COMBINE_TILE
    nc = geom.n_ctx // tm
    per_seq = geom.dec_seq // tm
    mod_row = lambda i: jnp.where(i < nc, 0, 1 + (i - nc) // per_seq)
    whole = lambda a: pl.BlockSpec(a.shape, lambda i, lo, hi: (0,) * a.ndim)
    return pl.pallas_call(
        functools.partial(_combine_kernel, n_chunks=n_rows // COMBINE_ROWS, ctx_tiles=nc),
        grid_spec=pltpu.PrefetchScalarGridSpec(
            num_scalar_prefetch=2,
            grid=(t // tm,),
            in_specs=[pl.BlockSpec(memory_space=pl.ANY), whole(start), whole(end),
                      pl.BlockSpec((tm, d), lambda i, lo, hi: (i, 0)),
                      pl.BlockSpec((None, 1, mod3.shape[-1]), lambda i, lo, hi: (mod_row(i), 0, 0))],
            out_specs=[pl.BlockSpec((tm, d), lambda i, lo, hi: (jnp.minimum(i, nc - 1), 0)),
                       pl.BlockSpec((tm, d), lambda i, lo, hi: (jnp.maximum(i - nc, 0), 0))],
            scratch_shapes=[pltpu.VMEM((COMBINE_BUFFERS, COMBINE_ROWS, d), F32), pltpu.VMEM((tm, d), F32),
                            pltpu.VMEM((tm, 2), F32),
                            pltpu.SemaphoreType.DMA((COMBINE_BUFFERS,)), pltpu.SMEM((2,), I32)]),
        out_shape=[jax.ShapeDtypeStruct((geom.n_ctx, d), F32), jax.ShapeDtypeStruct((geom.n_lat, d), F32)],
        compiler_params=_params(("arbitrary",)),
        name="gated_combine",
    )(tile_lo, tile_hi, z, start, end, x1, mod3)


class _Geometry:
    def __init__(self, batch, seq, dec_batch, dec_seq):
        self.batch, self.seq, self.dec_batch, self.dec_seq = batch, seq, dec_batch, dec_seq
        self.n_ctx = batch * seq
        self.n_lat = dec_batch * dec_seq
        tm = TOKEN_TILE
        assert self.n_ctx % tm == 0 and dec_seq % tm == 0 and self.n_ctx % dec_seq == 0
        assert seq % CHUNK == 0 and dec_seq % CHUNK == 0 and dec_seq % QUERY_TILE == 0
        self.ctx_tiles = self.n_ctx // tm
        self.lat_tiles_per_seq = dec_seq // tm

    def mod_row(self, i):
        return jnp.where(i < self.ctx_tiles, 0, 1 + (i - self.ctx_tiles) // self.lat_tiles_per_seq)

    def rope_block(self, i):
        return jnp.where(i < self.ctx_tiles, 0, 1 + (i - self.ctx_tiles) % self.lat_tiles_per_seq)


def _rope_tables(dec_seq):
    t = jnp.arange(dec_seq)
    row = (t // GRID_W).astype(F32)
    col = (t % GRID_W).astype(F32)
    inv = ROPE_THETA ** (-jnp.arange(ROT_PAIRS, dtype=F32) / ROT_PAIRS)
    ang = jnp.concatenate([row[:, None] * inv, col[:, None] * inv], axis=-1)
    cos = jnp.tile(jnp.concatenate([jnp.cos(ang), jnp.cos(ang)], axis=-1), (1, N_HEADS))
    sin = jnp.tile(jnp.concatenate([-jnp.sin(ang), jnp.sin(ang)], axis=-1), (1, N_HEADS))
    ident = jnp.ones((TOKEN_TILE, Q_DIM), F32)
    return (jnp.concatenate([ident, cos], axis=0),
            jnp.concatenate([jnp.zeros_like(ident), sin], axis=0))


def _block_diag_ones(width):
    i = jnp.arange(width)
    return (i[:, None] // HEAD_DIM == i[None, :] // HEAD_DIM).astype(BF)


def _layer_weights(l, norm1, norm2, w_in, conv_w, conv_b, q_norm, k_norm, a_log, dt_bias, d_skip,
                   ssm_norm, w_out, w_router):
    w = w_in[l].astype(BF)
    o = 0
    parts = {}
    for name, width in (("wq", Q_DIM), ("wkv", 2 * KV_DIM), ("wx", CONV_DIM),
                        ("wz", SSM_WIDTH), ("wdt", 2 * SSM_HEADS)):
        parts[name] = w[:, o:o + width]
        o += width
    wo = w_out[l].astype(BF)
    wr = w_router[l].T
    wrh = wr.astype(BF)
    nh2 = 2 * SSM_HEADS
    parts.update(
        wdtt=parts["wdt"].T,
        n1=norm1[l][None, :], n2=norm2[l][None, :],
        gq=jnp.tile(q_norm[l], N_HEADS)[None, :], gk=jnp.tile(k_norm[l], N_KV_HEADS)[None, :],
        conv_w=conv_w[l], conv_b=conv_b[l][None, :],
        alog_r=a_log[l].reshape(1, nh2), alog_c=a_log[l].reshape(nh2, 1),
        dtb_r=dt_bias[l].reshape(1, nh2), dtb_c=dt_bias[l].reshape(nh2, 1),
        dskip=jnp.repeat(d_skip[l], SSM_HEAD_DIM)[None, :], gn=ssm_norm[l][None, :],
        woa=wo[:Q_DIM], wos=wo[Q_DIM:],
        wrh=wrh, wrl=(wr - wrh.astype(F32)).astype(BF),
    )
    return parts


def _moe(h2, aff_t, x1, mod3, w_gate, w_up, w_down, layer, n_sets, geom):
    t = x1.shape[0]
    n = t // n_sets
    cap = EC_FACTOR * n // N_EXPERTS
    pairs = N_EXPERTS * cap
    gidx, gates, dest, start, end = _plan(aff_t, n_sets, cap)
    z = _expert_ffn(gidx.reshape(-1), dest.reshape(-1), h2, gates, w_gate, w_up, w_down, layer)
    tile_lo = start.reshape(t)[::COMBINE_TILE].astype(I32) // COMBINE_ROWS
    tile_hi = (end.reshape(t)[COMBINE_TILE - 1::COMBINE_TILE].astype(I32) + COMBINE_ROWS - 1) // COMBINE_ROWS
    lanes_major = lambda a: a.reshape(t // LANES, 1, LANES)
    return _combine(z, n_sets * pairs, lanes_major(start), lanes_major(end), tile_lo, tile_hi, x1, mod3, geom)


def kernel(x_prompt, x_sample, cache_k, cache_v, state_ssm, c, c_ctx, norm1, norm2, ada_w, ada_b,
           w_in, conv_w, conv_b, q_norm, k_norm, A_log, dt_bias, d_skip, ssm_norm, w_out,
           w_router, w_gate, w_up, w_down):
    batch, seq, d = x_prompt.shape
    dec_batch, dec_seq, _ = x_sample.shape
    depth = w_in.shape[0]
    past = cache_k.shape[2]
    geom = _Geometry(batch, seq, dec_batch, dec_seq)
    assert geom.n_ctx == geom.n_lat, "routed sets are stacked as two equal halves"

    rows = -(-(1 + dec_batch) // SUBLANES) * SUBLANES
    cv = jnp.zeros((rows, d), F32).at[0].set(c_ctx).at[1:1 + dec_batch].set(c)
    mod = _ada_modulation(cv, ada_w, ada_b)
    cos, sin = _rope_tables(dec_seq)
    consts = dict(cos=cos, sin=sin, bdq=_block_diag_ones(Q_DIM), bdk=_block_diag_ones(KV_DIM))
    cache_k4 = cache_k.reshape(dec_batch, depth, past, KV_DIM)
    cache_v4 = cache_v.reshape(dec_batch, depth, past, KV_DIM)

    x = (x_prompt.reshape(geom.n_ctx, d), x_sample.reshape(geom.n_lat, d))
    ks_new, vs_new, hs_new = [], [], []
    for l in range(depth):
        lw = _layer_weights(l, norm1, norm2, w_in, conv_w, conv_b, q_norm, k_norm, A_log, dt_bias,
                            d_skip, ssm_norm, w_out, w_router)
        mod3 = mod[l][:, None, :]
        q, k, v, xbc, z, dt, dtt = _in_projection(x, mod3, lw, consts, geom)
        attn_c, attn_l = _attention(q, k, v, cache_k4, cache_v4, l, geom)
        ssm_c, ssm_l, h_fin = _ssd_mixer(xbc, z, dt, dtt, state_ssm, l, lw, geom)
        x1, h2, aff_t = _out_projection(attn_c, attn_l, ssm_c, ssm_l, x, mod3, lw, geom)
        x = _moe(h2, aff_t, x1, mod3, w_gate, w_up, w_down, l, 2, geom)
        ks_new.append(k[:geom.n_ctx].reshape(batch, seq, N_KV_HEADS, HEAD_DIM))
        vs_new.append(v[:geom.n_ctx].reshape(batch, seq, N_KV_HEADS, HEAD_DIM))
        hs_new.append(h_fin)
    y_prompt = x[0].reshape(batch, seq, d)
    y_sample = x[1].reshape(dec_batch, dec_seq, d)
    return (y_prompt, y_sample, jnp.stack(ks_new, axis=1), jnp.stack(vs_new, axis=1),
            jnp.stack(hs_new, axis=1))
```
